```python
import jax
import jax.numpy as jnp
from jax import lax
import numpy as np

D_MODEL = 1024
BATCH = 8
SEQ = 4096
DEPTH = 4

CHUNK = 64
N_EVEN = (DEPTH + 1) // 2
N_ODD = DEPTH // 2
PLE_DIM = 256
D_FF = ((-(-8 * D_MODEL // 3) + 255) // 256) * 256
GROUP_WIDTH = D_MODEL // 2
RMS_EPS = 1e-6
L2_EPS = 1e-6
GN_EPS = 64e-5

RG_WIDTH = GROUP_WIDTH
RG_BLOCKS = 8
RG_BLOCK = RG_WIDTH // RG_BLOCKS
RG_CONV = 4
RG_C = 8.0

GDN_DK = 128
GDN_DV = 128
GDN_HEADS = GROUP_WIDTH // GDN_DV
GDN_WIDTH = GDN_HEADS * GDN_DV
GDN_CONV = 4

HG_DK = 128
HG_DV = 128
HG_HEADS = GROUP_WIDTH // HG_DV
HG_WIDTH = HG_HEADS * HG_DV

R7_N = 64
R7_HEADS = GROUP_WIDTH // R7_N
R7_WIDTH = R7_HEADS * R7_N
R7_DECAY_LORA = 64
R7_AAA_LORA = 64
R7_GATE_LORA = 128

EVEN_SIZES = (RG_WIDTH, RG_WIDTH, GDN_HEADS * GDN_DK, GDN_HEADS * GDN_DK, GDN_WIDTH, GDN_WIDTH, GDN_HEADS, GDN_HEADS)
GDN_QKV_SIZES = (GDN_HEADS * GDN_DK, GDN_HEADS * GDN_DK, GDN_WIDTH)
HG_SIZES = (HG_HEADS * HG_DK, HG_HEADS * HG_DK, HG_WIDTH, HG_WIDTH)
R7_SIZES = (R7_WIDTH, R7_WIDTH, R7_WIDTH, R7_DECAY_LORA, R7_AAA_LORA, R7_GATE_LORA)
EVEN_IN = sum(EVEN_SIZES)
HG_IN = sum(HG_SIZES)
R7_IN = sum(R7_SIZES)
ODD_IN = HG_IN + R7_IN
MIX_OUT = 2 * GROUP_WIDTH

kernel_name = "hybrid_rglru_gdn_hgrn2_rwkv7_trunk"


def _rms_norm(x, w):
    xf = x.astype(jnp.float32)
    y = xf * lax.rsqrt(jnp.mean(xf * xf, axis=-1, keepdims=True) + RMS_EPS)
    return (y * w.astype(jnp.float32)).astype(x.dtype)


def _l2norm(x):
    return x * lax.rsqrt(jnp.sum(x * x, axis=-1, keepdims=True) + L2_EPS)


def _split(z, sizes):
    return jnp.split(z, np.cumsum(sizes)[:-1].tolist(), axis=-1)


def _causal_dwconv(x, w):
    k_width, ch = w.shape
    return lax.conv_general_dilated(
        x, w.astype(x.dtype)[:, None, :], window_strides=(1,), padding=[(k_width - 1, 0)],
        dimension_numbers=("NWC", "WIO", "NWC"), feature_group_count=ch)


def _token_shift(x):
    return jnp.pad(x[:, :-1], ((0, 0), (1, 0), (0, 0)))


def _to_chunks(t):
    b, s = t.shape[:2]
    t = t.reshape(b, s // CHUNK, CHUNK, *t.shape[2:])
    return jnp.transpose(t, (1, 0, 3, 2) + tuple(range(4, t.ndim)))


def _from_chunks(t):
    t = jnp.transpose(t, (1, 0, 3, 2) + tuple(range(4, t.ndim)))
    return t.reshape(t.shape[0], t.shape[1] * t.shape[2], *t.shape[3:])


def _rg_lru(x, w_r, b_r, w_i, b_i, lam):
    bn, s, _ = x.shape
    xb = x.reshape(bn, s, RG_BLOCKS, RG_BLOCK)
    r = jax.nn.sigmoid(jnp.einsum("bsgi,gij->bsgj", xb, w_r).reshape(bn, s, RG_WIDTH) + b_r)
    i = jax.nn.sigmoid(jnp.einsum("bsgi,gij->bsgj", xb, w_i).reshape(bn, s, RG_WIDTH) + b_i)
    log_a = -RG_C * r * jax.nn.softplus(-lam)
    mult = jnp.sqrt(-jnp.expm1(2.0 * log_a))
    mult = jnp.where((jnp.arange(s) == 0)[None, :, None], 1.0, mult)
    u = mult * (i * x)

    def combine(left, right):
        a_l, h_l = left
        a_r, h_r = right
        return a_l * a_r, a_r * h_l + h_r

    _, h = lax.associative_scan(combine, (jnp.exp(log_a), u), axis=1)
    return h


def _gated_delta_rule(q, k, v, g, beta):
    bn, s, h, dk = q.shape
    dv = v.shape[-1]
    qc = _to_chunks(q * dk ** -0.5)
    kc, vc = _to_chunks(k), _to_chunks(v)
    gc = jnp.cumsum(_to_chunks(g), axis=-1)
    bc = _to_chunks(beta)
    causal = jnp.tril(jnp.ones((CHUNK, CHUNK), dtype=bool))
    strict = jnp.tril(jnp.ones((CHUNK, CHUNK), dtype=bool), -1)
    decay = jnp.exp(jnp.where(causal, gc[..., :, None] - gc[..., None, :], -jnp.inf))
    kb = kc * bc[..., None]
    a_mat = jnp.where(strict, jnp.einsum("nbhcd,nbhsd->nbhcs", kb, kc) * decay, 0.0)
    rhs = jnp.concatenate([vc * bc[..., None], kb * jnp.exp(gc)[..., None]], axis=-1)
    sol = lax.linalg.triangular_solve(a_mat + jnp.eye(CHUNK, dtype=a_mat.dtype), rhs,
                                      left_side=True, lower=True, unit_diagonal=True)
    u, w = sol[..., :dv], sol[..., dv:]
    qk = jnp.einsum("nbhcd,nbhsd->nbhcs", qc, kc) * decay
    q_dec = qc * jnp.exp(gc)[..., None]
    g_last = gc[..., -1]
    k_dec = kc * jnp.exp(g_last[..., None] - gc)[..., None]

    def step(state, xs):
        u_c, w_c, qk_c, qd_c, kd_c, gl_c = xs
        v_new = u_c - jnp.einsum("bhcd,bhde->bhce", w_c, state)
        o = jnp.einsum("bhcd,bhde->bhce", qd_c, state) + jnp.einsum("bhcs,bhse->bhce", qk_c, v_new)
        state = jnp.exp(gl_c)[..., None, None] * state + jnp.einsum("bhcd,bhce->bhde", kd_c, v_new)
        return state, o

    _, o = lax.scan(step, jnp.zeros((bn, h, dk, dv), jnp.float32), (u, w, qk, q_dec, k_dec, g_last))
    return _from_chunks(o)


def _hgrn2_chunked(q, k, v, log_f):
    bn, s, h, dk = q.shape
    dv = v.shape[-1]
    causal = jnp.tril(jnp.ones((CHUNK, CHUNK), dtype=bool))
    bc = jnp.cumsum(_to_chunks(log_f), axis=3)

    def step(state, xs):
        q_c, k_c, v_c, b_c = xs
        rel = jnp.exp(jnp.where(causal[:, :, None], b_c[:, :, :, None, :] - b_c[:, :, None, :, :], -jnp.inf))
        scores = jnp.einsum("bhtd,bhsd,bhtsd->bhts", q_c, k_c, rel)
        b_last = b_c[:, :, -1]
        o = (jnp.einsum("bhts,bhse->bhte", scores, v_c)
             + jnp.einsum("bhtd,bhde->bhte", q_c * jnp.exp(b_c), state))
        state = (jnp.exp(b_last)[..., None] * state
                 + jnp.einsum("bhsd,bhse->bhde", k_c * jnp.exp(b_last[:, :, None] - b_c), v_c))
        return state, o

    xs = (_to_chunks(q), _to_chunks(k), _to_chunks(v), bc)
    _, o = lax.scan(step, jnp.zeros((bn, h, dk, dv), jnp.float32), xs)
    return _from_chunks(o)


def _rwkv7_scan(r, w, k, v, kk, a):
    bn, s, h, n = r.shape

    def step(state, xs):
        r_t, w_t, k_t, v_t, kk_t, a_t = xs
        sa = jnp.einsum("bhij,bhj->bhi", state, -kk_t)
        state = (state * w_t[:, :, None, :] + sa[..., None] * (kk_t * a_t)[:, :, None, :]
                 + v_t[..., None] * k_t[:, :, None, :])
        return state, jnp.einsum("bhij,bhj->bhi", state, r_t)

    xs = tuple(jnp.moveaxis(t, 1, 0) for t in (r, w, k, v, kk, a))
    _, y = lax.scan(step, jnp.zeros((bn, h, n, n), jnp.float32), xs)
    return jnp.moveaxis(y, 0, 1)


def _even_mixer(h, w_in, rg_conv_w, rg_conv_b, rg_w_r, rg_b_r, rg_w_i, rg_b_i, rg_lambda,
                gdn_conv_w, gdn_a_log, gdn_dt_bias, gdn_norm_w, w_out):
    bn, s, _ = h.shape
    z = (h @ w_in).astype(jnp.float32)
    xa, ya, q, k, v, zg, a_in, b_in = _split(z, EVEN_SIZES)
    xa = _causal_dwconv(xa, rg_conv_w) + rg_conv_b
    out_a = _rg_lru(xa, rg_w_r, rg_b_r, rg_w_i, rg_b_i, rg_lambda) * jax.nn.gelu(ya)
    qkv = jax.nn.silu(_causal_dwconv(jnp.concatenate([q, k, v], axis=-1), gdn_conv_w))
    q, k, v = _split(qkv, GDN_QKV_SIZES)
    q = _l2norm(q.reshape(bn, s, GDN_HEADS, GDN_DK))
    k = _l2norm(k.reshape(bn, s, GDN_HEADS, GDN_DK))
    v = v.reshape(bn, s, GDN_HEADS, GDN_DV)
    beta = jax.nn.sigmoid(b_in)
    g = -jnp.exp(gdn_a_log) * jax.nn.softplus(a_in + gdn_dt_bias)
    o = _gated_delta_rule(q, k, v, g, beta)
    o = _rms_norm(o, gdn_norm_w) * jax.nn.silu(zg.reshape(bn, s, GDN_HEADS, GDN_DV))
    out_b = o.reshape(bn, s, GDN_WIDTH)
    mixed = jnp.concatenate([out_a, out_b], axis=-1).astype(h.dtype)
    return mixed @ w_out


def _odd_mixer(h, w_in, lower_bound, hg_norm_w, r7_mu, r7_w0, r7_w2, r7_a0, r7_a2, r7_g2,
               r7_k_k, r7_k_a, r7_r_k, r7_ln_w, r7_ln_b, w_out):
    bn, s, _ = h.shape
    z = (h @ w_in).astype(jnp.float32)
    zc, zd = z[..., :HG_IN], z[..., HG_IN:]

    def heads(t, d):
        return t.reshape(bn, s, -1, d)

    q, f, i, g = _split(zc, HG_SIZES)
    lb = jnp.maximum(lower_bound.astype(jnp.float32), 0.0)
    key_in = (1.0 - lb) * jax.nn.sigmoid(-f)
    log_f = jnp.logaddexp(jnp.log(lb), jnp.log1p(-lb) + jax.nn.log_sigmoid(f))
    o = _hgrn2_chunked(heads(jax.nn.silu(q), HG_DK), heads(key_in, HG_DK), heads(i, HG_DV), heads(log_f, HG_DK))
    out_c = _rms_norm(o.reshape(bn, s, HG_WIDTH), hg_norm_w) * jax.nn.sigmoid(g)
    zd = zd + r7_mu * (_token_shift(zd) - zd)
    r, k, v, w_l, a_l, g_l = _split(zd, R7_SIZES)
    w = jnp.exp(-jnp.exp(-jax.nn.softplus(-(r7_w0 + jnp.tanh(w_l) @ r7_w2)) - 0.5))
    a = jax.nn.sigmoid(r7_a0 + a_l @ r7_a2)
    gate = jax.nn.sigmoid(g_l) @ r7_g2
    kk = _l2norm(heads(k * r7_k_k, R7_N))
    k = k * (1.0 + (a - 1.0) * r7_k_a)
    r_h, k_h, v_h = heads(r, R7_N), heads(k, R7_N), heads(v, R7_N)
    y = _rwkv7_scan(r_h, heads(w, R7_N), k_h, v_h, kk, heads(a, R7_N))
    mean = jnp.mean(y, axis=-1, keepdims=True)
    var = jnp.mean(jnp.square(y - mean), axis=-1, keepdims=True)
    y = ((y - mean) * lax.rsqrt(var + GN_EPS)).reshape(bn, s, R7_WIDTH) * r7_ln_w + r7_ln_b
    bonus = jnp.sum(r_h * k_h * r7_r_k, axis=-1, keepdims=True) * v_h
    out_d = (y + bonus.reshape(bn, s, R7_WIDTH)) * gate
    mixed = jnp.concatenate([out_c, out_d], axis=-1).astype(h.dtype)
    return mixed @ w_out


def _swiglu(h, w_gate, w_up, w_down):
    return (jax.nn.silu(h @ w_gate) * (h @ w_up)) @ w_down


def setup_inputs(seed: int = 0) -> dict:
    key = jax.random.key(seed)
    keys = jax.random.split(key, 64)
    counter = [0]

    def nk():
        counter[0] += 1
        return keys[counter[0] - 1]

    def nrm(shape, scale):
        return jax.random.normal(nk(), shape, jnp.float32) * scale

    def unif(shape, lo, hi):
        return jax.random.uniform(nk(), shape, jnp.float32, lo, hi)

    def gain(shape):
        return 1.0 + nrm(shape, 0.01)

    a_rg = unif((N_EVEN, RG_WIDTH), 0.9, 0.999) ** (1.0 / RG_C)
    dt = jnp.exp(unif((N_EVEN, GDN_HEADS), float(np.log(1e-3)), float(np.log(1e-1))))
    return {
        "x": nrm((BATCH, SEQ, D_MODEL), 1.0),
        "p": nrm((DEPTH, BATCH, SEQ, PLE_DIM), 1.0),
        "norm_mix": gain((DEPTH, D_MODEL)),
        "norm_ffn": gain((DEPTH, D_MODEL)),
        "norm_ple": gain((DEPTH, D_MODEL)),
        "norm_final": gain((D_MODEL,)),
        "e_w_in": nrm((N_EVEN, D_MODEL, EVEN_IN), D_MODEL ** -0.5),
        "rg_conv_w": nrm((N_EVEN, RG_CONV, RG_WIDTH), RG_CONV ** -0.5),
        "rg_conv_b": nrm((N_EVEN, RG_WIDTH), 0.01),
        "rg_w_r": nrm((N_EVEN, RG_BLOCKS, RG_BLOCK, RG_BLOCK), RG_BLOCK ** -0.5),
        "rg_b_r": nrm((N_EVEN, RG_WIDTH), 0.01),
        "rg_w_i": nrm((N_EVEN, RG_BLOCKS, RG_BLOCK, RG_BLOCK), RG_BLOCK ** -0.5),
        "rg_b_i": nrm((N_EVEN, RG_WIDTH), 0.01),
        "rg_lambda": jnp.log(a_rg) - jnp.log1p(-a_rg),
        "gdn_conv_w": nrm((N_EVEN, GDN_CONV, sum(GDN_QKV_SIZES)), GDN_CONV ** -0.5),
        "gdn_a_log": jnp.log(unif((N_EVEN, GDN_HEADS), 1.0, 16.0)),
        "gdn_dt_bias": dt + jnp.log(-jnp.expm1(-dt)),
        "gdn_norm_w": gain((N_EVEN, GDN_DV)),
        "e_w_out": nrm((N_EVEN, MIX_OUT, D_MODEL), MIX_OUT ** -0.5),
        "o_w_in": nrm((N_ODD, D_MODEL, ODD_IN), D_MODEL ** -0.5),
        "hg_lower_bounds": nrm((N_ODD, HG_HEADS * HG_DK), 0.1),
        "hg_norm_w": gain((N_ODD, HG_WIDTH)),
        "r7_mu": unif((N_ODD, R7_IN), 0.0, 1.0),
        "r7_w0": unif((N_ODD, R7_WIDTH), -6.0, 0.0),
        "r7_w2": nrm((N_ODD, R7_DECAY_LORA, R7_WIDTH), 0.5 * R7_DECAY_LORA ** -0.5),
        "r7_a0": nrm((N_ODD, R7_WIDTH), 0.1),
        "r7_a2": nrm((N_ODD, R7_AAA_LORA, R7_WIDTH), 0.5 * R7_AAA_LORA ** -0.5),
        "r7_g2": nrm((N_ODD, R7_GATE_LORA, R7_WIDTH), R7_GATE_LORA ** -0.5),
        "r7_k_k": 0.85 + nrm((N_ODD, R7_WIDTH), 0.01),
        "r7_k_a": gain((N_ODD, R7_WIDTH)),
        "r7_r_k": nrm((N_ODD, R7_HEADS, R7_N), 0.1),
        "r7_ln_w": gain((N_ODD, R7_WIDTH)),
        "r7_ln_b": nrm((N_ODD, R7_WIDTH), 0.01),
        "o_w_out": nrm((N_ODD, MIX_OUT, D_MODEL), MIX_OUT ** -0.5),
        "ffn_w_gate": nrm((DEPTH, D_MODEL, D_FF), D_MODEL ** -0.5),
        "ffn_w_up": nrm((DEPTH, D_MODEL, D_FF), D_MODEL ** -0.5),
        "ffn_w_down": nrm((DEPTH, D_FF, D_MODEL), D_FF ** -0.5),
        "ple_w_up": nrm((DEPTH, PLE_DIM, D_MODEL), PLE_DIM ** -0.5),
        "ple_w_gate": nrm((DEPTH, D_MODEL, D_MODEL), D_MODEL ** -0.5),
    }


def reference(x, p, norm_mix, norm_ffn, norm_ple, norm_final,
              e_w_in, rg_conv_w, rg_conv_b, rg_w_r, rg_b_r, rg_w_i, rg_b_i, rg_lambda,
              gdn_conv_w, gdn_a_log, gdn_dt_bias, gdn_norm_w, e_w_out,
              o_w_in, hg_lower_bounds, hg_norm_w, r7_mu, r7_w0, r7_w2, r7_a0, r7_a2, r7_g2,
              r7_k_k, r7_k_a, r7_r_k, r7_ln_w, r7_ln_b, o_w_out,
              ffn_w_gate, ffn_w_up, ffn_w_down, ple_w_up, ple_w_gate):
    lb_soft = jax.nn.softmax(hg_lower_bounds.astype(jnp.float32), axis=0)
    hg_lb = jnp.cumsum(lb_soft, axis=0) - lb_soft[0]
    h = x
    for layer in range(DEPTH):
        j = layer // 2
        hn = _rms_norm(h, norm_mix[layer])
        if layer % 2 == 0:
            h = h + _even_mixer(hn, e_w_in[j], rg_conv_w[j], rg_conv_b[j], rg_w_r[j], rg_b_r[j],
                                rg_w_i[j], rg_b_i[j], rg_lambda[j], gdn_conv_w[j], gdn_a_log[j],
                                gdn_dt_bias[j], gdn_norm_w[j], e_w_out[j])
        else:
            h = h + _odd_mixer(hn, o_w_in[j], hg_lb[j], hg_norm_w[j], r7_mu[j], r7_w0[j], r7_w2[j],
                               r7_a0[j], r7_a2[j], r7_g2[j], r7_k_k[j], r7_k_a[j], r7_r_k[j],
                               r7_ln_w[j], r7_ln_b[j], o_w_out[j])
        h = h + _swiglu(_rms_norm(h, norm_ffn[layer]), ffn_w_gate[layer], ffn_w_up[layer], ffn_w_down[layer])
        ple_gate = jax.nn.sigmoid(_rms_norm(h, norm_ple[layer]) @ ple_w_gate[layer])
        h = h + ple_gate * (p[layer] @ ple_w_up[layer])
    return _rms_norm(h, norm_final)
```

```python
import functools
import math

import jax
import jax.numpy as jnp
from jax import lax
from jax.experimental import pallas as pl
from jax.experimental.pallas import tpu as pltpu

F32 = jnp.float32
BF16 = jnp.bfloat16

D_MODEL = 1024
PLE_DIM = 256
D_FF = 2816
GROUP_WIDTH = 512
RMS_EPS = 1e-6
L2_EPS = 1e-6
GN_EPS = 64e-5
RG_C = 8.0
RG_BLOCKS = 8
CONV_TAPS = 4
HEAD128 = 128
R7_N = 64
R7_LORA_PAD = 128
R7_IN_PAD = 3 * GROUP_WIDTH + 3 * R7_LORA_PAD

GDN_CHUNK = 128
HG_SUB = 16
R7_CHUNK = 64
HALO = 8

VMEM_LIMIT = 56 * 1024 * 1024


def _sigmoid(x):
    return 1.0 / (1.0 + jnp.exp(-x))


def _softplus(x):
    return jnp.maximum(x, 0.0) + jnp.log1p(jnp.exp(-jnp.abs(x)))


def _silu(x):
    return x * _sigmoid(x)


def _gelu_tanh(x):
    c = math.sqrt(2.0 / math.pi)
    return 0.5 * x * (1.0 + jnp.tanh(c * (x + 0.044715 * (x * x * x))))


def _rms(x, w):
    ms = jnp.mean(x * x, axis=-1, keepdims=True)
    return x * lax.rsqrt(ms + RMS_EPS) * w


def _mm(a, b):
    return jnp.dot(a.astype(BF16), b.astype(BF16), preferred_element_type=F32)


def _mm_nt(a, b):
    return lax.dot_general(a.astype(BF16), b.astype(BF16), (((1,), (1,)), ((), ())),
                           preferred_element_type=F32)


def _split(x, terms):
    parts = []
    r = x
    for _ in range(terms):
        hi = r.astype(BF16)
        parts.append(hi)
        r = r - hi.astype(F32)
    return parts


def _sel_mm(sel, x, terms=3):
    acc = None
    for part in _split(x, terms):
        d = jnp.dot(sel, part, preferred_element_type=F32)
        acc = d if acc is None else acc + d
    return acc


def _mm_sel(x, sel, terms=2):
    acc = None
    for part in _split(x, terms):
        d = jnp.dot(part, sel, preferred_element_type=F32)
        acc = d if acc is None else acc + d
    return acc


def _mm_hi(a, b):
    a_hi, a_lo = _split(a, 2)
    b_hi, b_lo = _split(b, 2)
    return (jnp.dot(a_hi, b_hi, preferred_element_type=F32)
            + jnp.dot(a_lo, b_hi, preferred_element_type=F32)
            + jnp.dot(a_hi, b_lo, preferred_element_type=F32))


def _iota2(shape, axis):
    return lax.broadcasted_iota(jnp.int32, shape, axis)


def _block_tri(n, block, kind):
    t = _iota2((n, n), 0)
    s = _iota2((n, n), 1)
    same = (t // block) == (s // block)
    cond = (s <= t) if kind == "incl" else (s > t)
    return jnp.where(same & cond, 1.0, 0.0).astype(BF16)


def _block_ones(n, block):
    t = _iota2((n, n), 0)
    s = _iota2((n, n), 1)
    return jnp.where((t // block) == (s // block), 1.0, 0.0).astype(BF16)


def _nilpotent_inverse(x, n):
    eye = jnp.where(_iota2(x.shape, 0) == _iota2(x.shape, 1), 1.0, 0.0).astype(F32)
    t = eye + x
    p = x
    k = 2
    while k < n:
        p = _mm_hi(p, p)
        t = t + _mm_hi(t, p)
        k *= 2
    return t


def _norm_proj_kernel(h_ref, nw_ref, wa_ref, wb_ref, za_ref, zb_ref):
    xn = _rms(h_ref[...], nw_ref[...]).astype(BF16)
    za_ref[...] = jnp.dot(xn, wa_ref[...], preferred_element_type=F32)
    zb_ref[...] = jnp.dot(xn, wb_ref[...], preferred_element_type=F32)


def _const_spec(shape):
    nd = len(shape)
    return pl.BlockSpec(shape, lambda *_: (0,) * nd, pipeline_mode=pl.Buffered(1))


def _norm_proj(h, norm_w, wa, wb, tm):
    t, d = h.shape
    fa, fb = wa.shape[1], wb.shape[1]
    return pl.pallas_call(
        _norm_proj_kernel,
        grid=(t // tm,),
        in_specs=[
            pl.BlockSpec((tm, d), lambda i: (i, 0)),
            _const_spec((1, d)),
            _const_spec((d, fa)),
            _const_spec((d, fb)),
        ],
        out_specs=[
            pl.BlockSpec((tm, fa), lambda i: (i, 0)),
            pl.BlockSpec((tm, fb), lambda i: (i, 0)),
        ],
        out_shape=[jax.ShapeDtypeStruct((t, fa), F32), jax.ShapeDtypeStruct((t, fb), F32)],
        compiler_params=pltpu.CompilerParams(
            dimension_semantics=("parallel",), vmem_limit_bytes=VMEM_LIMIT),
        name="norm_proj",
    )(h, norm_w.reshape(1, d), wa, wb)


def _post_kernel(final_norm, h_ref, ma_ref, mb_ref, p_ref, wo_ref, nf_ref, wg_ref, wu_ref, wd_ref,
                 npl_ref, wpg_ref, wpu_ref, nfin_ref, out_ref):
    h = h_ref[...]
    mix = (jnp.dot(ma_ref[...], wo_ref[0:GROUP_WIDTH, :], preferred_element_type=F32)
           + jnp.dot(mb_ref[...], wo_ref[GROUP_WIDTH:2 * GROUP_WIDTH, :], preferred_element_type=F32))
    h1 = h + mix
    hn = _rms(h1, nf_ref[...]).astype(BF16)
    g = jnp.dot(hn, wg_ref[...], preferred_element_type=F32)
    u = jnp.dot(hn, wu_ref[...], preferred_element_type=F32)
    act = (_silu(g) * u).astype(BF16)
    h2 = h1 + jnp.dot(act, wd_ref[...], preferred_element_type=F32)
    hp = _rms(h2, npl_ref[...]).astype(BF16)
    gate = _sigmoid(jnp.dot(hp, wpg_ref[...], preferred_element_type=F32))
    up = jnp.dot(p_ref[...].astype(BF16), wpu_ref[...], preferred_element_type=F32)
    h3 = h2 + gate * up
    if final_norm:
        h3 = _rms(h3, nfin_ref[...])
    out_ref[...] = h3


def _post(h, ma, mb, p, wo, nf, wg, wu, wd, npl, wpg, wpu, nfin, final_norm, tm):
    t, d = h.shape
    row = lambda w: pl.BlockSpec((tm, w), lambda i: (i, 0))
    return pl.pallas_call(
        functools.partial(_post_kernel, final_norm),
        grid=(t // tm,),
        in_specs=[
            row(d), row(GROUP_WIDTH), row(GROUP_WIDTH), row(PLE_DIM),
            _const_spec(wo.shape), _const_spec((1, d)),
            _const_spec(wg.shape), _const_spec(wu.shape), _const_spec(wd.shape),
            _const_spec((1, d)), _const_spec(wpg.shape), _const_spec(wpu.shape),
            _const_spec((1, d)),
        ],
        out_specs=row(d),
        out_shape=jax.ShapeDtypeStruct((t, d), F32),
        compiler_params=pltpu.CompilerParams(
            dimension_semantics=("parallel",), vmem_limit_bytes=VMEM_LIMIT),
        name="post_mixer",
    )(h, ma, mb, p, wo, nf.reshape(1, d), wg, wu, wd, npl.reshape(1, d), wpg, wpu, nfin.reshape(1, d))


def _causal_conv(xbuf_ref, x, w_ref, ts):
    xbuf_ref[HALO:HALO + ts, :] = x
    acc = None
    for j in range(CONV_TAPS):
        off = HALO - (CONV_TAPS - 1) + j
        term = w_ref[j:j + 1, :] * xbuf_ref[off:off + ts, :]
        acc = term if acc is None else acc + term
    xbuf_ref[0:HALO, :] = xbuf_ref[ts:ts + HALO, :]
    return acc


def _rglru_kernel(ts, z_ref, cw_ref, cb_ref, wri_ref, bri_ref, lam_ref, out_ref, xbuf_ref, hc_ref):
    w = GROUP_WIDTH
    tstep = pl.program_id(1)

    @pl.when(tstep == 0)
    def _():
        xbuf_ref[0:HALO, :] = jnp.zeros((HALO, w), F32)
        hc_ref[...] = jnp.zeros_like(hc_ref)

    xa = z_ref[0, :, 0:w]
    ya = z_ref[0, :, w:2 * w]
    x = _causal_conv(xbuf_ref, xa, cw_ref, ts) + cb_ref[...]
    ri = jnp.dot(x.astype(BF16), wri_ref[...], preferred_element_type=F32) + bri_ref[...]
    r = _sigmoid(ri[:, 0:w])
    i = _sigmoid(ri[:, w:2 * w])
    log_a = -RG_C * r * _softplus(-lam_ref[...])
    a = jnp.exp(log_a)
    mult = jnp.sqrt(-jnp.tanh(log_a) * (a * a + 1.0))
    row = _iota2((ts, w), 0)
    mult = jnp.where((row == 0) & (tstep == 0), 1.0, mult)
    u = mult * (i * x)
    acc_a, acc_h = a, u
    d = 1
    while d < ts:
        sh_a = jnp.where(row >= d, pltpu.roll(acc_a, d, 0), 1.0)
        sh_h = jnp.where(row >= d, pltpu.roll(acc_h, d, 0), 0.0)
        acc_h = acc_a * sh_h + acc_h
        acc_a = acc_a * sh_a
        d *= 2
    h = acc_h + acc_a * hc_ref[...]
    hc_ref[...] = h[ts - 1:ts, :]
    out_ref[0] = (h * _gelu_tanh(ya)).astype(out_ref.dtype)


def _rglru(z, cw, cb, wri, bri, lam, ts):
    b, s, _ = z.shape
    w = GROUP_WIDTH
    return pl.pallas_call(
        functools.partial(_rglru_kernel, ts),
        grid=(b, s // ts),
        in_specs=[
            pl.BlockSpec((1, ts, 2 * w), lambda bi, ti: (bi, ti, 0)),
            _const_spec((CONV_TAPS, w)), _const_spec((1, w)),
            _const_spec((w, 2 * w)), _const_spec((1, 2 * w)), _const_spec((1, w)),
        ],
        out_specs=pl.BlockSpec((1, ts, w), lambda bi, ti: (bi, ti, 0)),
        out_shape=jax.ShapeDtypeStruct((b, s, w), BF16),
        scratch_shapes=[pltpu.VMEM((HALO + ts, w), F32), pltpu.VMEM((1, w), F32)],
        compiler_params=pltpu.CompilerParams(
            dimension_semantics=("parallel", "arbitrary"), vmem_limit_bytes=VMEM_LIMIT),
        name="rglru",
    )(z, cw, cb, wri, bri, lam)


def _gdn_chunk(q, k, v, gc, beta, state):
    c = q.shape[0]
    t = _iota2((c, c), 0)
    s = _iota2((c, c), 1)
    eg = jnp.exp(gc)
    kb = k * beta
    diff = gc - gc.T
    dec = jnp.exp(jnp.where(s <= t, diff, -jnp.inf))
    a_mat = jnp.where(s < t, _mm_nt(kb, k) * dec, 0.0)
    tinv = _nilpotent_inverse(-a_mat, c)
    u = _mm_hi(tinv, v * beta)
    w = _mm_hi(tinv, kb * eg)
    qk = _mm_nt(q, k) * dec
    g_last = gc[c - 1:c, :]
    k_dec = k * jnp.exp(g_last - gc)
    v_new = u - _mm(w, state)
    o = _mm(q * eg, state) + _mm(qk, v_new)
    new_state = jnp.exp(g_last) * state + _mm(k_dec.T, v_new)
    return o, new_state


def _gdn_kernel(ts, z_ref, cw_ref, alog_ref, dtb_ref, nw_ref, out_ref, xbuf_ref, st_ref):
    w = GROUP_WIDTH
    hd = HEAD128
    nh = w // hd
    tstep = pl.program_id(1)

    @pl.when(tstep == 0)
    def _():
        xbuf_ref[0:HALO, :] = jnp.zeros((HALO, 3 * w), F32)
        st_ref[...] = jnp.zeros_like(st_ref)

    qkv = _silu(_causal_conv(xbuf_ref, z_ref[0, :, 0:3 * w], cw_ref, ts))
    zg = z_ref[0, :, 3 * w:4 * w]
    a_in = z_ref[0, :, 4 * w:5 * w]
    b_in = z_ref[0, :, 5 * w:6 * w]
    beta = _sigmoid(b_in)
    g = -jnp.exp(alog_ref[...]) * _softplus(a_in + dtb_ref[...])
    gc_all = _sel_mm(_block_tri(ts, GDN_CHUNK, "incl"), g)
    scale = hd ** -0.5
    for h in range(nh):
        lo = h * hd
        qh = qkv[:, lo:lo + hd]
        kh = qkv[:, w + lo:w + lo + hd]
        vh = qkv[:, 2 * w + lo:2 * w + lo + hd]
        qh = qh * lax.rsqrt(jnp.sum(qh * qh, axis=-1, keepdims=True) + L2_EPS) * scale
        kh = kh * lax.rsqrt(jnp.sum(kh * kh, axis=-1, keepdims=True) + L2_EPS)
        state = st_ref[h]
        outs = []
        for c in range(ts // GDN_CHUNK):
            r0 = c * GDN_CHUNK
            r1 = r0 + GDN_CHUNK
            o, state = _gdn_chunk(qh[r0:r1], kh[r0:r1], vh[r0:r1], gc_all[r0:r1, lo:lo + hd],
                                  beta[r0:r1, lo:lo + hd], state)
            outs.append(o)
        st_ref[h] = state
        o = jnp.concatenate(outs, axis=0) if len(outs) > 1 else outs[0]
        o = _rms(o, nw_ref[...]) * _silu(zg[:, lo:lo + hd])
        out_ref[0, :, lo:lo + hd] = o.astype(out_ref.dtype)


def _gdn(z, cw, alog, dtb, nw, ts):
    b, s, _ = z.shape
    w = GROUP_WIDTH
    nh = w // HEAD128
    return pl.pallas_call(
        functools.partial(_gdn_kernel, ts),
        grid=(b, s // ts),
        in_specs=[
            pl.BlockSpec((1, ts, 6 * w), lambda bi, ti: (bi, ti, 0)),
            _const_spec((CONV_TAPS, 3 * w)), _const_spec((1, w)), _const_spec((1, w)),
            _const_spec((1, HEAD128)),
        ],
        out_specs=pl.BlockSpec((1, ts, w), lambda bi, ti: (bi, ti, 0)),
        out_shape=jax.ShapeDtypeStruct((b, s, w), BF16),
        scratch_shapes=[pltpu.VMEM((HALO + ts, 3 * w), F32), pltpu.VMEM((nh, HEAD128, HEAD128), F32)],
        compiler_params=pltpu.CompilerParams(
            dimension_semantics=("parallel", "arbitrary"), vmem_limit_bytes=VMEM_LIMIT),
        name="gated_deltanet",
    )(z, cw, alog, dtb, nw)


def _hgrn2_kernel(ts, layer_j, z_ref, lbraw_ref, nw_ref, out_ref, st_ref, o_ref):
    w = GROUP_WIDTH
    hd = HEAD128
    nh = w // hd
    sub = HG_SUB
    nsub = ts // sub
    tstep = pl.program_id(1)

    @pl.when(tstep == 0)
    def _():
        st_ref[...] = jnp.zeros_like(st_ref)

    lbraw = lbraw_ref[...]
    e = jnp.exp(lbraw - jnp.max(lbraw, axis=0, keepdims=True))
    soft = e / jnp.sum(e, axis=0, keepdims=True)
    lb = jnp.sum(soft[0:layer_j + 1, :], axis=0, keepdims=True) - soft[0:1, :]
    lb = jnp.maximum(lb, 0.0)

    q = _silu(z_ref[0, :, 0:w])
    f = z_ref[0, :, w:2 * w]
    v = z_ref[0, :, 2 * w:3 * w]
    gate = z_ref[0, :, 3 * w:4 * w]
    kin = (1.0 - lb) * _sigmoid(-f)
    la = jnp.log(lb)
    lbm = jnp.log1p(-lb) - _softplus(-f)
    mx = jnp.maximum(la, lbm)
    log_f = mx + jnp.log1p(jnp.exp(-jnp.abs(la - lbm)))

    bl = _sel_mm(_block_tri(ts, sub, "incl"), log_f)
    bsuf = _sel_mm(_block_tri(ts, sub, "suffix"), log_f)
    qe = q * jnp.exp(bl)
    kh = kin * jnp.exp(bsuf)

    ones_bd = _block_ones(w, hd)
    q3 = q.reshape(nsub, sub, w)
    k3 = kin.reshape(nsub, sub, w)
    v3 = v.reshape(nsub, sub, w)
    b3 = bl.reshape(nsub, sub, w)
    tpos = _iota2((nsub, sub, w), 1)
    o3 = jnp.zeros((nsub, sub, w), F32)
    for s in range(sub):
        rel = jnp.exp(jnp.where(tpos >= s, b3 - b3[:, s:s + 1, :], -jnp.inf))
        prod = q3 * k3[:, s:s + 1, :] * rel
        sc = _mm_sel(prod.reshape(ts, w), ones_bd).reshape(nsub, sub, w)
        o3 = o3 + sc * v3[:, s:s + 1, :]
    o_intra = o3.reshape(ts, w)

    blast = bl + bsuf
    blk = 128
    for h in range(nh):
        lo = h * hd
        st = st_ref[h]
        outs = []
        for b0 in range(0, ts, blk):
            vt = v[b0:b0 + blk, lo:lo + hd].T
            for i in range(blk // sub):
                r0 = b0 + i * sub
                outs.append(_mm_nt(qe[r0:r0 + sub, lo:lo + hd], st))
                parts = []
                if i > 0:
                    parts.append(jnp.zeros((i * sub, hd), F32))
                parts.append(kh[r0:r0 + sub, lo:lo + hd])
                if (i + 1) * sub < blk:
                    parts.append(jnp.zeros((blk - (i + 1) * sub, hd), F32))
                kpad = jnp.concatenate(parts, axis=0) if len(parts) > 1 else parts[0]
                st = st * jnp.exp(blast[r0:r0 + 1, lo:lo + hd]) + _mm(vt, kpad)
        st_ref[h] = st
        o_inter = jnp.concatenate(outs, axis=0)
        o_ref[:, lo:lo + hd] = o_intra[:, lo:lo + hd] + o_inter

    out_ref[0] = (_rms(o_ref[...], nw_ref[...]) * _sigmoid(gate)).astype(out_ref.dtype)


def _hgrn2(z, lbraw, nw, layer_j, ts):
    b, s, _ = z.shape
    w = GROUP_WIDTH
    nh = w // HEAD128
    return pl.pallas_call(
        functools.partial(_hgrn2_kernel, ts, layer_j),
        grid=(b, s // ts),
        in_specs=[
            pl.BlockSpec((1, ts, 4 * w), lambda bi, ti: (bi, ti, 0)),
            _const_spec(lbraw.shape), _const_spec((1, w)),
        ],
        out_specs=pl.BlockSpec((1, ts, w), lambda bi, ti: (bi, ti, 0)),
        out_shape=jax.ShapeDtypeStruct((b, s, w), BF16),
        scratch_shapes=[pltpu.VMEM((nh, HEAD128, HEAD128), F32), pltpu.VMEM((ts, w), F32)],
        compiler_params=pltpu.CompilerParams(
            dimension_semantics=("parallel", "arbitrary"), vmem_limit_bytes=VMEM_LIMIT),
        name="hgrn2",
    )(z, lbraw, nw)


def _r7_pair_chunk(r_t, a_t, b_h, k_h, v, b_dc, k_dc, d_c, n_state):
    c, lanes = r_t.shape
    n = lanes // 2
    lane = _iota2((c, lanes), 1)
    in_h0 = lane < n

    def stack_heads(x):
        return jnp.concatenate([jnp.where(in_h0, x, 0.0), jnp.where(in_h0, 0.0, x)], axis=0)

    b_s, k_s, v_s = stack_heads(b_h), stack_heads(k_h), stack_heads(v)
    gram = _mm_nt(jnp.concatenate([a_t, r_t], axis=0), jnp.concatenate([b_s, k_s], axis=0))
    t = _iota2((c, lanes), 0)
    s = lane % n
    a_ab = jnp.where(s < t, gram[0:c, 0:lanes], 0.0)
    a_ak = jnp.where(s < t, gram[0:c, lanes:2 * lanes], 0.0)
    a_rb = jnp.where(s <= t, gram[c:2 * c, 0:lanes], 0.0)
    a_rk = jnp.where(s <= t, gram[c:2 * c, lanes:2 * lanes], 0.0)
    tinv = _nilpotent_inverse(stack_heads(a_ab), c)
    rhs = _mm(a_ak, v_s) + _mm_nt(a_t, n_state)
    u_s = _mm_hi(tinv, stack_heads(rhs))
    u = u_s[0:c] + u_s[c:2 * c]
    y = _mm_nt(r_t, n_state) + _mm(a_rb, u_s) + _mm(a_rk, v_s)
    upd = _mm(jnp.concatenate([u, v], axis=0).T, jnp.concatenate([b_dc, k_dc], axis=0))
    i2 = _iota2((lanes, lanes), 0)
    j2 = _iota2((lanes, lanes), 1)
    upd = jnp.where((i2 < n) == (j2 < n), upd, 0.0)
    return y, n_state * d_c + upd


def _rwkv7_kernel(ts, z_ref, mu_ref, w0_ref, w2_ref, a0_ref, a2_ref, g2_ref, kk_ref, ka_ref, rk_ref,
                  lnw_ref, lnb_ref, out_ref, xbuf_ref, st_ref, y_ref):
    w = GROUP_WIDTH
    lp = R7_LORA_PAD
    ch = R7_CHUNK
    npair = w // (2 * R7_N)
    tstep = pl.program_id(1)

    @pl.when(tstep == 0)
    def _():
        xbuf_ref[0:HALO, :] = jnp.zeros((HALO, R7_IN_PAD), F32)
        st_ref[...] = jnp.zeros_like(st_ref)

    z = z_ref[0]
    xbuf_ref[HALO:HALO + ts, :] = z
    z_prev = xbuf_ref[HALO - 1:HALO - 1 + ts, :]
    xbuf_ref[0:HALO, :] = xbuf_ref[ts:ts + HALO, :]
    zd = z + mu_ref[...] * (z_prev - z)
    r = zd[:, 0:w]
    k = zd[:, w:2 * w]
    v = zd[:, 2 * w:3 * w]
    w_l = zd[:, 3 * w:3 * w + lp]
    a_l = zd[:, 3 * w + lp:3 * w + 2 * lp]
    g_l = zd[:, 3 * w + 2 * lp:3 * w + 3 * lp]
    log_w = -jnp.exp(-_softplus(-(w0_ref[...] + _mm(jnp.tanh(w_l), w2_ref[...]))) - 0.5)
    a = _sigmoid(a0_ref[...] + _mm(a_l, a2_ref[...]))
    gate = _mm(_sigmoid(g_l), g2_ref[...])
    ones_bd = _block_ones(w, R7_N)
    kk = k * kk_ref[...]
    kk = kk * lax.rsqrt(_mm_sel(kk * kk, ones_bd) + L2_EPS)
    k = k * (1.0 + (a - 1.0) * ka_ref[...])

    cum = _sel_mm(_block_tri(ts, ch, "incl"), log_w)
    suf = _sel_mm(_block_tri(ts, ch, "suffix"), log_w)
    e_inv = jnp.exp(-cum)
    r_t = r * jnp.exp(cum)
    a_t = -kk * jnp.exp(cum - log_w)
    kka = kk * a
    b_h = kka * e_inv
    k_h = k * e_inv
    e_suf = jnp.exp(suf)
    b_dc = kka * e_suf
    k_dc = k * e_suf
    d_end = jnp.exp(cum + suf)

    pw = 2 * R7_N
    for p in range(npair):
        lo = p * pw
        st = st_ref[p]
        outs = []
        for c in range(ts // ch):
            r0 = c * ch
            r1 = r0 + ch
            sl = (slice(r0, r1), slice(lo, lo + pw))
            y, st = _r7_pair_chunk(r_t[sl], a_t[sl], b_h[sl], k_h[sl], v[sl], b_dc[sl], k_dc[sl],
                                   d_end[r0:r0 + 1, lo:lo + pw], st)
            outs.append(y)
        st_ref[p] = st
        y_ref[:, lo:lo + pw] = jnp.concatenate(outs, axis=0) if len(outs) > 1 else outs[0]

    y = y_ref[...]
    mean = _mm_sel(y, ones_bd) * (1.0 / R7_N)
    yc = y - mean
    var = _mm_sel(yc * yc, ones_bd) * (1.0 / R7_N)
    yn = yc * lax.rsqrt(var + GN_EPS) * lnw_ref[...] + lnb_ref[...]
    bonus = _mm_sel(r * k * rk_ref[...], ones_bd) * v
    out_ref[0] = ((yn + bonus) * gate).astype(out_ref.dtype)


def _rwkv7(z, mu, w0, w2, a0, a2, g2, k_k, k_a, r_k, ln_w, ln_b, ts):
    b, s, fin = z.shape
    w = GROUP_WIDTH
    npair = w // (2 * R7_N)
    vec = _const_spec((1, w))
    return pl.pallas_call(
        functools.partial(_rwkv7_kernel, ts),
        grid=(b, s // ts),
        in_specs=[
            pl.BlockSpec((1, ts, fin), lambda bi, ti: (bi, ti, 0)),
            _const_spec((1, fin)), vec, _const_spec(w2.shape), vec, _const_spec(a2.shape),
            _const_spec(g2.shape), vec, vec, vec, vec, vec,
        ],
        out_specs=pl.BlockSpec((1, ts, w), lambda bi, ti: (bi, ti, 0)),
        out_shape=jax.ShapeDtypeStruct((b, s, w), BF16),
        scratch_shapes=[pltpu.VMEM((HALO + ts, fin), F32), pltpu.VMEM((npair, 2 * R7_N, 2 * R7_N), F32),
                        pltpu.VMEM((ts, w), F32)],
        compiler_params=pltpu.CompilerParams(
            dimension_semantics=("parallel", "arbitrary"), vmem_limit_bytes=VMEM_LIMIT),
        name="rwkv7",
    )(z, mu, w0, w2, a0, a2, g2, k_k, k_a, r_k, ln_w, ln_b)


def _even_weights(w_in, rg_w_r, rg_w_i, rg_b_r, rg_b_i, gdn_a_log, gdn_dt_bias):
    w = GROUP_WIDTH
    nh = w // HEAD128
    rep = lambda cols: jnp.repeat(cols, HEAD128, axis=1)
    w_rg = w_in[:, 0:2 * w]
    w_gdn = jnp.concatenate(
        [w_in[:, 2 * w:6 * w], rep(w_in[:, 6 * w:6 * w + nh]), rep(w_in[:, 6 * w + nh:6 * w + 2 * nh])], axis=1)
    wri = jnp.concatenate([jax.scipy.linalg.block_diag(*rg_w_r), jax.scipy.linalg.block_diag(*rg_w_i)], axis=1)
    bri = jnp.concatenate([rg_b_r, rg_b_i]).reshape(1, 2 * w)
    alog = jnp.repeat(gdn_a_log, HEAD128).reshape(1, w)
    dtb = jnp.repeat(gdn_dt_bias, HEAD128).reshape(1, w)
    return w_rg.astype(BF16), w_gdn.astype(BF16), wri.astype(BF16), bri, alog, dtb


def _pad_groups(x, axis):
    w = GROUP_WIDTH
    sizes = (w, w, w, 64, 64, 128)
    parts = []
    start = 0
    for sz in sizes:
        piece = lax.slice_in_dim(x, start, start + sz, axis=axis)
        pad = (-sz) % R7_LORA_PAD
        if pad:
            cfg = [(0, 0)] * x.ndim
            cfg[axis] = (0, pad)
            piece = jnp.pad(piece, cfg)
        parts.append(piece)
        start += sz
    return jnp.concatenate(parts, axis=axis)


def _pad_rows(x, rows):
    return jnp.pad(x, ((0, rows - x.shape[0]), (0, 0)))


def _tiles(b, s):
    t = b * s
    tm_proj = 512 if t % 512 == 0 else 256
    tm_post = 256
    ts = 256 if s % 256 == 0 else 128
    return tm_proj, tm_post, ts


def kernel(x, p, norm_mix, norm_ffn, norm_ple, norm_final, e_w_in, rg_conv_w, rg_conv_b, rg_w_r, rg_b_r, rg_w_i, rg_b_i, rg_lambda, gdn_conv_w, gdn_a_log, gdn_dt_bias, gdn_norm_w, e_w_out, o_w_in, hg_lower_bounds, hg_norm_w, r7_mu, r7_w0, r7_w2, r7_a0, r7_a2, r7_g2, r7_k_k, r7_k_a, r7_r_k, r7_ln_w, r7_ln_b, o_w_out, ffn_w_gate, ffn_w_up, ffn_w_down, ple_w_up, ple_w_gate):
    b, s, d = x.shape
    depth = p.shape[0]
    t = b * s
    w = GROUP_WIDTH
    tm_proj, tm_post, ts = _tiles(b, s)
    h = x.reshape(t, d)
    row = lambda vec: vec.reshape(1, -1)
    for layer in range(depth):
        j = layer // 2
        if layer % 2 == 0:
            w_rg, w_gdn, wri, bri, alog, dtb = _even_weights(
                e_w_in[j], rg_w_r[j], rg_w_i[j], rg_b_r[j], rg_b_i[j], gdn_a_log[j], gdn_dt_bias[j])
            z_a, z_b = _norm_proj(h, norm_mix[layer], w_rg, w_gdn, tm_proj)
            out_a = _rglru(z_a.reshape(b, s, -1), rg_conv_w[j], row(rg_conv_b[j]), wri, bri,
                           row(rg_lambda[j]), ts)
            out_b = _gdn(z_b.reshape(b, s, -1), gdn_conv_w[j], alog, dtb, row(gdn_norm_w[j]), ts)
            w_out = e_w_out[j]
        else:
            hg_in = 4 * w
            w_hg = o_w_in[j][:, 0:hg_in].astype(BF16)
            w_r7 = _pad_groups(o_w_in[j][:, hg_in:], axis=1).astype(BF16)
            z_a, z_b = _norm_proj(h, norm_mix[layer], w_hg, w_r7, tm_proj)
            out_a = _hgrn2(z_a.reshape(b, s, -1), hg_lower_bounds, row(hg_norm_w[j]), j, ts)
            out_b = _rwkv7(z_b.reshape(b, s, -1), row(_pad_groups(r7_mu[j], axis=0)), row(r7_w0[j]),
                           _pad_rows(r7_w2[j], R7_LORA_PAD).astype(BF16), row(r7_a0[j]),
                           _pad_rows(r7_a2[j], R7_LORA_PAD).astype(BF16), r7_g2[j].astype(BF16),
                           row(r7_k_k[j]), row(r7_k_a[j]), row(r7_r_k[j].reshape(-1)),
                           row(r7_ln_w[j]), row(r7_ln_b[j]), ts)
            w_out = o_w_out[j]
        h = _post(h, out_a.reshape(t, w), out_b.reshape(t, w), p[layer].reshape(t, PLE_DIM),
                  w_out.astype(BF16), norm_ffn[layer], ffn_w_gate[layer].astype(BF16),
                  ffn_w_up[layer].astype(BF16), ffn_w_down[layer].astype(BF16), norm_ple[layer],
                  ple_w_gate[layer].astype(BF16), ple_w_up[layer].astype(BF16), norm_final,
                  layer == depth - 1, tm_post)
    return h.reshape(b, s, d)
```

```python
import functools
import math

import jax
import jax.numpy as jnp
from jax import lax
from jax.experimental import pallas as pl
from jax.experimental.pallas import tpu as pltpu

F32 = jnp.float32
BF16 = jnp.bfloat16

D_MODEL = 1024
PLE_DIM = 256
D_FF = 2816
GROUP_WIDTH = 512
RMS_EPS = 1e-6
L2_EPS = 1e-6
GN_EPS = 64e-5
RG_C = 8.0
RG_BLOCKS = 8
CONV_TAPS = 4
HEAD128 = 128
R7_N = 64
R7_LORA_PAD = 128
R7_IN_PAD = 3 * GROUP_WIDTH + 3 * R7_LORA_PAD

GDN_CHUNK = 128
HG_SUB = 16
R7_CHUNK = 64
HALO = 8

VMEM_LIMIT = 56 * 1024 * 1024


def _sigmoid(x):
    return 1.0 / (1.0 + jnp.exp(-x))


def _softplus(x):
    return jnp.maximum(x, 0.0) + jnp.log1p(jnp.exp(-jnp.abs(x)))


def _silu(x):
    return x * _sigmoid(x)


def _gelu_tanh(x):
    c = math.sqrt(2.0 / math.pi)
    return 0.5 * x * (1.0 + jnp.tanh(c * (x + 0.044715 * (x * x * x))))


def _rms(x, w):
    ms = jnp.mean(x * x, axis=-1, keepdims=True)
    return x * lax.rsqrt(ms + RMS_EPS) * w


def _mm(a, b):
    return jnp.dot(a.astype(BF16), b.astype(BF16), preferred_element_type=F32)


def _mm_nt(a, b):
    return lax.dot_general(a.astype(BF16), b.astype(BF16), (((1,), (1,)), ((), ())),
                           preferred_element_type=F32)


def _split(x, terms):
    parts = []
    r = x
    for _ in range(terms):
        hi = r.astype(BF16)
        parts.append(hi)
        r = r - hi.astype(F32)
    return parts


def _sel_mm(sel, x, terms=3):
    acc = None
    for part in _split(x, terms):
        d = jnp.dot(sel, part, preferred_element_type=F32)
        acc = d if acc is None else acc + d
    return acc


def _mm_sel(x, sel, terms=2):
    acc = None
    for part in _split(x, terms):
        d = jnp.dot(part, sel, preferred_element_type=F32)
        acc = d if acc is None else acc + d
    return acc


def _mm_hi(a, b):
    a_hi, a_lo = _split(a, 2)
    b_hi, b_lo = _split(b, 2)
    return (jnp.dot(a_hi, b_hi, preferred_element_type=F32)
            + jnp.dot(a_lo, b_hi, preferred_element_type=F32)
            + jnp.dot(a_hi, b_lo, preferred_element_type=F32))


def _iota2(shape, axis):
    return lax.broadcasted_iota(jnp.int32, shape, axis)


def _block_tri(n, block, kind):
    t = _iota2((n, n), 0)
    s = _iota2((n, n), 1)
    same = (t // block) == (s // block)
    cond = (s <= t) if kind == "incl" else (s > t)
    return jnp.where(same & cond, 1.0, 0.0).astype(BF16)


def _block_ones(n, block):
    t = _iota2((n, n), 0)
    s = _iota2((n, n), 1)
    return jnp.where((t // block) == (s // block), 1.0, 0.0).astype(BF16)


def _nilpotent_inverses(xs, n):
    shape = xs[0].shape
    eye = jnp.where(_iota2(shape, 0) == _iota2(shape, 1), 1.0, 0.0).astype(F32)
    ts = [eye + x for x in xs]
    ps = list(xs)
    k = 2
    while k < n:
        ps = [_mm_hi(p, p) for p in ps]
        ts = [t + _mm_hi(t, p) for t, p in zip(ts, ps)]
        k *= 2
    return ts


def _norm_proj_kernel(h_ref, nw_ref, wa_ref, wb_ref, za_ref, zb_ref):
    xn = _rms(h_ref[...], nw_ref[...]).astype(BF16)
    za_ref[...] = jnp.dot(xn, wa_ref[...], preferred_element_type=F32)
    zb_ref[...] = jnp.dot(xn, wb_ref[...], preferred_element_type=F32)


def _const_spec(shape):
    nd = len(shape)
    return pl.BlockSpec(shape, lambda *_: (0,) * nd, pipeline_mode=pl.Buffered(1))


def _norm_proj(h, norm_w, wa, wb, tm):
    t, d = h.shape
    fa, fb = wa.shape[1], wb.shape[1]
    return pl.pallas_call(
        _norm_proj_kernel,
        grid=(t // tm,),
        in_specs=[
            pl.BlockSpec((tm, d), lambda i: (i, 0)),
            _const_spec((1, d)),
            _const_spec((d, fa)),
            _const_spec((d, fb)),
        ],
        out_specs=[
            pl.BlockSpec((tm, fa), lambda i: (i, 0)),
            pl.BlockSpec((tm, fb), lambda i: (i, 0)),
        ],
        out_shape=[jax.ShapeDtypeStruct((t, fa), F32), jax.ShapeDtypeStruct((t, fb), F32)],
        compiler_params=pltpu.CompilerParams(
            dimension_semantics=("parallel",), vmem_limit_bytes=VMEM_LIMIT),
        name="norm_proj",
    )(h, norm_w.reshape(1, d), wa, wb)


def _post_kernel(final_norm, h_ref, ma_ref, mb_ref, p_ref, wo_ref, nf_ref, wg_ref, wu_ref, wd_ref,
                 npl_ref, wpg_ref, wpu_ref, nfin_ref, out_ref):
    h = h_ref[...]
    mix = (jnp.dot(ma_ref[...], wo_ref[0:GROUP_WIDTH, :], preferred_element_type=F32)
           + jnp.dot(mb_ref[...], wo_ref[GROUP_WIDTH:2 * GROUP_WIDTH, :], preferred_element_type=F32))
    h1 = h + mix
    hn = _rms(h1, nf_ref[...]).astype(BF16)
    g = jnp.dot(hn, wg_ref[...], preferred_element_type=F32)
    u = jnp.dot(hn, wu_ref[...], preferred_element_type=F32)
    act = (_silu(g) * u).astype(BF16)
    h2 = h1 + jnp.dot(act, wd_ref[...], preferred_element_type=F32)
    hp = _rms(h2, npl_ref[...]).astype(BF16)
    gate = _sigmoid(jnp.dot(hp, wpg_ref[...], preferred_element_type=F32))
    up = jnp.dot(p_ref[...].astype(BF16), wpu_ref[...], preferred_element_type=F32)
    h3 = h2 + gate * up
    if final_norm:
        h3 = _rms(h3, nfin_ref[...])
    out_ref[...] = h3


def _post(h, ma, mb, p, wo, nf, wg, wu, wd, npl, wpg, wpu, nfin, final_norm, tm):
    t, d = h.shape
    row = lambda w: pl.BlockSpec((tm, w), lambda i: (i, 0))
    return pl.pallas_call(
        functools.partial(_post_kernel, final_norm),
        grid=(t // tm,),
        in_specs=[
            row(d), row(GROUP_WIDTH), row(GROUP_WIDTH), row(PLE_DIM),
            _const_spec(wo.shape), _const_spec((1, d)),
            _const_spec(wg.shape), _const_spec(wu.shape), _const_spec(wd.shape),
            _const_spec((1, d)), _const_spec(wpg.shape), _const_spec(wpu.shape),
            _const_spec((1, d)),
        ],
        out_specs=row(d),
        out_shape=jax.ShapeDtypeStruct((t, d), F32),
        compiler_params=pltpu.CompilerParams(
            dimension_semantics=("parallel",), vmem_limit_bytes=VMEM_LIMIT),
        name="post_mixer",
    )(h, ma, mb, p, wo, nf.reshape(1, d), wg, wu, wd, npl.reshape(1, d), wpg, wpu, nfin.reshape(1, d))


def _causal_conv(xbuf_ref, x, w_ref, ts):
    xbuf_ref[HALO:HALO + ts, :] = x
    acc = None
    for j in range(CONV_TAPS):
        off = HALO - (CONV_TAPS - 1) + j
        term = w_ref[j:j + 1, :] * xbuf_ref[off:off + ts, :]
        acc = term if acc is None else acc + term
    xbuf_ref[0:HALO, :] = xbuf_ref[ts:ts + HALO, :]
    return acc


def _rglru_kernel(ts, z_ref, cw_ref, cb_ref, wri_ref, bri_ref, lam_ref, out_ref, xbuf_ref, hc_ref):
    w = GROUP_WIDTH
    tstep = pl.program_id(1)

    @pl.when(tstep == 0)
    def _():
        xbuf_ref[0:HALO, :] = jnp.zeros((HALO, w), F32)
        hc_ref[...] = jnp.zeros_like(hc_ref)

    xa = z_ref[0, :, 0:w]
    ya = z_ref[0, :, w:2 * w]
    x = _causal_conv(xbuf_ref, xa, cw_ref, ts) + cb_ref[...]
    ri = jnp.dot(x.astype(BF16), wri_ref[...], preferred_element_type=F32) + bri_ref[...]
    r = _sigmoid(ri[:, 0:w])
    i = _sigmoid(ri[:, w:2 * w])
    log_a = -RG_C * r * _softplus(-lam_ref[...])
    a = jnp.exp(log_a)
    mult = jnp.sqrt(-jnp.tanh(log_a) * (a * a + 1.0))
    row = _iota2((ts, w), 0)
    mult = jnp.where((row == 0) & (tstep == 0), 1.0, mult)
    u = mult * (i * x)
    acc_a, acc_h = a, u
    d = 1
    while d < ts:
        sh_a = jnp.where(row >= d, pltpu.roll(acc_a, d, 0), 1.0)
        sh_h = jnp.where(row >= d, pltpu.roll(acc_h, d, 0), 0.0)
        acc_h = acc_a * sh_h + acc_h
        acc_a = acc_a * sh_a
        d *= 2
    h = acc_h + acc_a * hc_ref[...]
    hc_ref[...] = h[ts - 1:ts, :]
    out_ref[0] = (h * _gelu_tanh(ya)).astype(out_ref.dtype)


def _rglru(z, cw, cb, wri, bri, lam, ts):
    b, s, _ = z.shape
    w = GROUP_WIDTH
    return pl.pallas_call(
        functools.partial(_rglru_kernel, ts),
        grid=(b, s // ts),
        in_specs=[
            pl.BlockSpec((1, ts, 2 * w), lambda bi, ti: (bi, ti, 0)),
            _const_spec((CONV_TAPS, w)), _const_spec((1, w)),
            _const_spec((w, 2 * w)), _const_spec((1, 2 * w)), _const_spec((1, w)),
        ],
        out_specs=pl.BlockSpec((1, ts, w), lambda bi, ti: (bi, ti, 0)),
        out_shape=jax.ShapeDtypeStruct((b, s, w), BF16),
        scratch_shapes=[pltpu.VMEM((HALO + ts, w), F32), pltpu.VMEM((1, w), F32)],
        compiler_params=pltpu.CompilerParams(
            dimension_semantics=("parallel", "arbitrary"), vmem_limit_bytes=VMEM_LIMIT),
        name="rglru",
    )(z, cw, cb, wri, bri, lam)


def _gdn_kernel(ts, z_ref, cw_ref, alog_ref, dtb_ref, nw_ref, out_ref, xbuf_ref, st_ref):
    w = GROUP_WIDTH
    hd = HEAD128
    nh = w // hd
    cs = GDN_CHUNK
    nchunk = ts // cs
    tstep = pl.program_id(1)

    @pl.when(tstep == 0)
    def _():
        xbuf_ref[0:HALO, :] = jnp.zeros((HALO, 3 * w), F32)
        st_ref[...] = jnp.zeros_like(st_ref)

    qkv = _silu(_causal_conv(xbuf_ref, z_ref[0, :, 0:3 * w], cw_ref, ts))
    zg = z_ref[0, :, 3 * w:4 * w]
    a_in = z_ref[0, :, 4 * w:5 * w]
    b_in = z_ref[0, :, 5 * w:6 * w]
    beta = _sigmoid(b_in)
    g = -jnp.exp(alog_ref[...]) * _softplus(a_in + dtb_ref[...])
    gc_all = _sel_mm(_block_tri(ts, cs, "incl"), g)
    gs_all = _sel_mm(_block_tri(ts, cs, "suffix"), g)
    eg_all = jnp.exp(gc_all)
    scale = hd ** -0.5
    tt = _iota2((cs, cs), 0)
    ss = _iota2((cs, cs), 1)

    items = [(h, c) for c in range(nchunk) for h in range(nh)]
    qs, ks, vbs, kbegs, decs, kdts, glast = [], [], [], [], [], [], []
    for h, c in items:
        lo = h * hd
        rows = slice(c * cs, (c + 1) * cs)
        hs = slice(lo, lo + hd)
        qh = qkv[rows, lo:lo + hd]
        kh = qkv[rows, w + lo:w + lo + hd]
        vh = qkv[rows, 2 * w + lo:2 * w + lo + hd]
        qh = qh * lax.rsqrt(jnp.sum(qh * qh, axis=-1, keepdims=True) + L2_EPS) * scale
        kh = kh * lax.rsqrt(jnp.sum(kh * kh, axis=-1, keepdims=True) + L2_EPS)
        gc = gc_all[rows, hs]
        bt = beta[rows, hs]
        eg = eg_all[rows, hs]
        kb = kh * bt
        diff = gc - gc.T
        decs.append(jnp.exp(jnp.where(ss <= tt, diff, -jnp.inf)))
        qs.append(qh * eg)
        ks.append((kh, kb, qh))
        vbs.append(vh * bt)
        kbegs.append(kb * eg)
        kdts.append((kh * jnp.exp(gs_all[rows, hs])).T)
        glast.append(jnp.exp(gc_all[(c + 1) * cs - 1:(c + 1) * cs, hs]))
    a_mats = [jnp.where(ss < tt, _mm_nt(kb, kh) * dec, 0.0) for (kh, kb, _), dec in zip(ks, decs)]
    qks = [_mm_nt(qh, kh) * dec for (kh, _, qh), dec in zip(ks, decs)]
    tinvs = _nilpotent_inverses([-a for a in a_mats], cs)
    us = [_mm_hi(t, vb) for t, vb in zip(tinvs, vbs)]
    ws = [_mm_hi(t, kbeg) for t, kbeg in zip(tinvs, kbegs)]

    states = [st_ref[h] for h in range(nh)]
    outs = [[] for _ in range(nh)]
    for c in range(nchunk):
        idx = [c * nh + h for h in range(nh)]
        proj = [_mm(jnp.concatenate([ws[i], qs[i]], axis=0), states[h]) for h, i in enumerate(idx)]
        v_new = [us[i] - pr[0:cs] for i, pr in zip(idx, proj)]
        o = [pr[cs:2 * cs] + _mm(qks[i], vn) for i, pr, vn in zip(idx, proj, v_new)]
        states = [glast[i] * states[h] + _mm(kdts[i], vn) for (h, i), vn in zip(enumerate(idx), v_new)]
        for h in range(nh):
            outs[h].append(o[h])
    for h in range(nh):
        lo = h * hd
        st_ref[h] = states[h]
        o = jnp.concatenate(outs[h], axis=0) if nchunk > 1 else outs[h][0]
        o = _rms(o, nw_ref[...]) * _silu(zg[:, lo:lo + hd])
        out_ref[0, :, lo:lo + hd] = o.astype(out_ref.dtype)


def _gdn(z, cw, alog, dtb, nw, ts):
    b, s, _ = z.shape
    w = GROUP_WIDTH
    nh = w // HEAD128
    return pl.pallas_call(
        functools.partial(_gdn_kernel, ts),
        grid=(b, s // ts),
        in_specs=[
            pl.BlockSpec((1, ts, 6 * w), lambda bi, ti: (bi, ti, 0)),
            _const_spec((CONV_TAPS, 3 * w)), _const_spec((1, w)), _const_spec((1, w)),
            _const_spec((1, HEAD128)),
        ],
        out_specs=pl.BlockSpec((1, ts, w), lambda bi, ti: (bi, ti, 0)),
        out_shape=jax.ShapeDtypeStruct((b, s, w), BF16),
        scratch_shapes=[pltpu.VMEM((HALO + ts, 3 * w), F32), pltpu.VMEM((nh, HEAD128, HEAD128), F32)],
        compiler_params=pltpu.CompilerParams(
            dimension_semantics=("parallel", "arbitrary"), vmem_limit_bytes=VMEM_LIMIT),
        name="gated_deltanet",
    )(z, cw, alog, dtb, nw)


def _hgrn2_kernel(ts, layer_j, z_ref, lbraw_ref, nw_ref, out_ref, st_ref, o_ref):
    w = GROUP_WIDTH
    hd = HEAD128
    nh = w // hd
    sub = HG_SUB
    nsub = ts // sub
    tstep = pl.program_id(1)

    @pl.when(tstep == 0)
    def _():
        st_ref[...] = jnp.zeros_like(st_ref)

    lbraw = lbraw_ref[...]
    e = jnp.exp(lbraw - jnp.max(lbraw, axis=0, keepdims=True))
    soft = e / jnp.sum(e, axis=0, keepdims=True)
    lb = jnp.sum(soft[0:layer_j + 1, :], axis=0, keepdims=True) - soft[0:1, :]
    lb = jnp.maximum(lb, 0.0)

    q = _silu(z_ref[0, :, 0:w])
    f = z_ref[0, :, w:2 * w]
    v = z_ref[0, :, 2 * w:3 * w]
    gate = z_ref[0, :, 3 * w:4 * w]
    kin = (1.0 - lb) * _sigmoid(-f)
    la = jnp.log(lb)
    lbm = jnp.log1p(-lb) - _softplus(-f)
    mx = jnp.maximum(la, lbm)
    log_f = mx + jnp.log1p(jnp.exp(-jnp.abs(la - lbm)))

    bl = _sel_mm(_block_tri(ts, sub, "incl"), log_f)
    bsuf = _sel_mm(_block_tri(ts, sub, "suffix"), log_f)
    qe = q * jnp.exp(bl)
    kh = kin * jnp.exp(bsuf)

    ones_bd = _block_ones(w, hd)
    q3 = q.reshape(nsub, sub, w)
    k3 = kin.reshape(nsub, sub, w)
    v3 = v.reshape(nsub, sub, w)
    b3 = bl.reshape(nsub, sub, w)
    tpos = _iota2((nsub, sub, w), 1)
    o3 = jnp.zeros((nsub, sub, w), F32)
    for s in range(sub):
        rel = jnp.exp(jnp.where(tpos >= s, b3 - b3[:, s:s + 1, :], -jnp.inf))
        prod = q3 * k3[:, s:s + 1, :] * rel
        sc = _mm_sel(prod.reshape(ts, w), ones_bd).reshape(nsub, sub, w)
        o3 = o3 + sc * v3[:, s:s + 1, :]
    o_intra = o3.reshape(ts, w)

    dec_sub = jnp.exp(bl + bsuf)
    blk = 128
    per_blk = blk // sub
    incs = [[] for _ in range(nh)]
    for h in range(nh):
        lo = h * hd
        for b0 in range(0, ts, blk):
            vt = v[b0:b0 + blk, lo:lo + hd].T.astype(BF16)
            for i in range(per_blk):
                r0 = b0 + i * sub
                parts = []
                if i > 0:
                    parts.append(jnp.zeros((i * sub, hd), BF16))
                parts.append(kh[r0:r0 + sub, lo:lo + hd].astype(BF16))
                if i + 1 < per_blk:
                    parts.append(jnp.zeros((blk - (i + 1) * sub, hd), BF16))
                kpad = jnp.concatenate(parts, axis=0) if len(parts) > 1 else parts[0]
                incs[h].append(jnp.dot(vt, kpad, preferred_element_type=F32))
    seen = [[] for _ in range(nh)]
    for h in range(nh):
        lo = h * hd
        st = st_ref[h]
        for i in range(nsub):
            seen[h].append(st.astype(BF16))
            st = st * dec_sub[i * sub:i * sub + 1, lo:lo + hd] + incs[h][i]
        st_ref[h] = st
    for h in range(nh):
        lo = h * hd
        o_inter = jnp.concatenate(
            [_mm_nt(qe[i * sub:(i + 1) * sub, lo:lo + hd], seen[h][i]) for i in range(nsub)], axis=0)
        o_ref[:, lo:lo + hd] = o_intra[:, lo:lo + hd] + o_inter

    out_ref[0] = (_rms(o_ref[...], nw_ref[...]) * _sigmoid(gate)).astype(out_ref.dtype)


def _hgrn2(z, lbraw, nw, layer_j, ts):
    b, s, _ = z.shape
    w = GROUP_WIDTH
    nh = w // HEAD128
    return pl.pallas_call(
        functools.partial(_hgrn2_kernel, ts, layer_j),
        grid=(b, s // ts),
        in_specs=[
            pl.BlockSpec((1, ts, 4 * w), lambda bi, ti: (bi, ti, 0)),
            _const_spec(lbraw.shape), _const_spec((1, w)),
        ],
        out_specs=pl.BlockSpec((1, ts, w), lambda bi, ti: (bi, ti, 0)),
        out_shape=jax.ShapeDtypeStruct((b, s, w), BF16),
        scratch_shapes=[pltpu.VMEM((nh, HEAD128, HEAD128), F32), pltpu.VMEM((ts, w), F32)],
        compiler_params=pltpu.CompilerParams(
            dimension_semantics=("parallel", "arbitrary"), vmem_limit_bytes=VMEM_LIMIT),
        name="hgrn2",
    )(z, lbraw, nw)


def _rwkv7_kernel(ts, z_ref, mu_ref, w0_ref, w2_ref, a0_ref, a2_ref, g2_ref, kk_ref, ka_ref, rk_ref,
                  lnw_ref, lnb_ref, out_ref, xbuf_ref, st_ref, y_ref):
    w = GROUP_WIDTH
    lp = R7_LORA_PAD
    ch = R7_CHUNK
    npair = w // (2 * R7_N)
    tstep = pl.program_id(1)

    @pl.when(tstep == 0)
    def _():
        xbuf_ref[0:HALO, :] = jnp.zeros((HALO, R7_IN_PAD), F32)
        st_ref[...] = jnp.zeros_like(st_ref)

    z = z_ref[0]
    xbuf_ref[HALO:HALO + ts, :] = z
    z_prev = xbuf_ref[HALO - 1:HALO - 1 + ts, :]
    xbuf_ref[0:HALO, :] = xbuf_ref[ts:ts + HALO, :]
    zd = z + mu_ref[...] * (z_prev - z)
    r = zd[:, 0:w]
    k = zd[:, w:2 * w]
    v = zd[:, 2 * w:3 * w]
    w_l = zd[:, 3 * w:3 * w + lp]
    a_l = zd[:, 3 * w + lp:3 * w + 2 * lp]
    g_l = zd[:, 3 * w + 2 * lp:3 * w + 3 * lp]
    log_w = -jnp.exp(-_softplus(-(w0_ref[...] + _mm(jnp.tanh(w_l), w2_ref[...]))) - 0.5)
    a = _sigmoid(a0_ref[...] + _mm(a_l, a2_ref[...]))
    gate = _mm(_sigmoid(g_l), g2_ref[...])
    ones_bd = _block_ones(w, R7_N)
    kk = k * kk_ref[...]
    kk = kk * lax.rsqrt(_mm_sel(kk * kk, ones_bd) + L2_EPS)
    k = k * (1.0 + (a - 1.0) * ka_ref[...])

    cum = _sel_mm(_block_tri(ts, ch, "incl"), log_w)
    suf = _sel_mm(_block_tri(ts, ch, "suffix"), log_w)
    e_inv = jnp.exp(-cum)
    r_t = r * jnp.exp(cum)
    a_t = -kk * jnp.exp(cum - log_w)
    kka = kk * a
    b_h = kka * e_inv
    k_h = k * e_inv
    e_suf = jnp.exp(suf)
    b_dc = kka * e_suf
    k_dc = k * e_suf
    d_end = jnp.exp(cum + suf)

    pw = 2 * R7_N
    n = R7_N
    nchunk = ts // ch
    lane = _iota2((ch, pw), 1)
    in_h0 = lane < n
    t_row = _iota2((ch, pw), 0)
    s_col = lane % n
    i2 = _iota2((pw, pw), 0)
    j2 = _iota2((pw, pw), 1)
    same_head = (i2 < n) == (j2 < n)

    def stack_heads(x):
        return jnp.concatenate([jnp.where(in_h0, x, 0.0), jnp.where(in_h0, 0.0, x)], axis=0)

    items = [(c, p) for c in range(nchunk) for p in range(npair)]
    sl = lambda c, p: (slice(c * ch, (c + 1) * ch), slice(p * pw, (p + 1) * pw))
    v_ss = [stack_heads(v[sl(c, p)]) for c, p in items]
    grams = [
        _mm_nt(jnp.concatenate([a_t[sl(c, p)], r_t[sl(c, p)]], axis=0),
               jnp.concatenate([stack_heads(b_h[sl(c, p)]), stack_heads(k_h[sl(c, p)])], axis=0))
        for c, p in items]
    a_abs = [jnp.where(s_col < t_row, g[0:ch, 0:pw], 0.0) for g in grams]
    a_aks = [jnp.where(s_col < t_row, g[0:ch, pw:2 * pw], 0.0) for g in grams]
    a_rbs = [jnp.where(s_col <= t_row, g[ch:2 * ch, 0:pw], 0.0) for g in grams]
    a_rks = [jnp.where(s_col <= t_row, g[ch:2 * ch, pw:2 * pw], 0.0) for g in grams]
    tinvs = _nilpotent_inverses([stack_heads(x) for x in a_abs], ch)
    akv = [_mm(a, vs) for a, vs in zip(a_aks, v_ss)]
    rkv = [_mm(a, vs) for a, vs in zip(a_rks, v_ss)]

    states = [st_ref[p] for p in range(npair)]
    outs = [[] for _ in range(npair)]
    for c in range(nchunk):
        idx = [c * npair + p for p in range(npair)]
        proj = [_mm_nt(jnp.concatenate([a_t[sl(c, p)], r_t[sl(c, p)]], axis=0), states[p])
                for p in range(npair)]
        u_ss = [_mm_hi(tinvs[i], stack_heads(akv[i] + pr[0:ch])) for i, pr in zip(idx, proj)]
        us = [u_s[0:ch] + u_s[ch:2 * ch] for u_s in u_ss]
        ys = [pr[ch:2 * ch] + _mm(a_rbs[i], u_s) + rkv[i] for i, pr, u_s in zip(idx, proj, u_ss)]
        upds = [_mm(jnp.concatenate([u, v[sl(c, p)]], axis=0).T,
                    jnp.concatenate([b_dc[sl(c, p)], k_dc[sl(c, p)]], axis=0))
                for p, u in enumerate(us)]
        states = [states[p] * d_end[c * ch:c * ch + 1, p * pw:(p + 1) * pw]
                  + jnp.where(same_head, upds[p], 0.0) for p in range(npair)]
        for p in range(npair):
            outs[p].append(ys[p])
    for p in range(npair):
        st_ref[p] = states[p]
        y_ref[:, p * pw:(p + 1) * pw] = jnp.concatenate(outs[p], axis=0) if nchunk > 1 else outs[p][0]

    y = y_ref[...]
    mean = _mm_sel(y, ones_bd) * (1.0 / R7_N)
    yc = y - mean
    var = _mm_sel(yc * yc, ones_bd) * (1.0 / R7_N)
    yn = yc * lax.rsqrt(var + GN_EPS) * lnw_ref[...] + lnb_ref[...]
    bonus = _mm_sel(r * k * rk_ref[...], ones_bd) * v
    out_ref[0] = ((yn + bonus) * gate).astype(out_ref.dtype)


def _rwkv7(z, mu, w0, w2, a0, a2, g2, k_k, k_a, r_k, ln_w, ln_b, ts):
    b, s, fin = z.shape
    w = GROUP_WIDTH
    npair = w // (2 * R7_N)
    vec = _const_spec((1, w))
    return pl.pallas_call(
        functools.partial(_rwkv7_kernel, ts),
        grid=(b, s // ts),
        in_specs=[
            pl.BlockSpec((1, ts, fin), lambda bi, ti: (bi, ti, 0)),
            _const_spec((1, fin)), vec, _const_spec(w2.shape), vec, _const_spec(a2.shape),
            _const_spec(g2.shape), vec, vec, vec, vec, vec,
        ],
        out_specs=pl.BlockSpec((1, ts, w), lambda bi, ti: (bi, ti, 0)),
        out_shape=jax.ShapeDtypeStruct((b, s, w), BF16),
        scratch_shapes=[pltpu.VMEM((HALO + ts, fin), F32), pltpu.VMEM((npair, 2 * R7_N, 2 * R7_N), F32),
                        pltpu.VMEM((ts, w), F32)],
        compiler_params=pltpu.CompilerParams(
            dimension_semantics=("parallel", "arbitrary"), vmem_limit_bytes=VMEM_LIMIT),
        name="rwkv7",
    )(z, mu, w0, w2, a0, a2, g2, k_k, k_a, r_k, ln_w, ln_b)


def _even_weights(w_in, rg_w_r, rg_w_i, rg_b_r, rg_b_i, gdn_a_log, gdn_dt_bias):
    w = GROUP_WIDTH
    nh = w // HEAD128
    rep = lambda cols: jnp.repeat(cols, HEAD128, axis=1)
    w_rg = w_in[:, 0:2 * w]
    w_gdn = jnp.concatenate(
        [w_in[:, 2 * w:6 * w], rep(w_in[:, 6 * w:6 * w + nh]), rep(w_in[:, 6 * w + nh:6 * w + 2 * nh])], axis=1)
    wri = jnp.concatenate([jax.scipy.linalg.block_diag(*rg_w_r), jax.scipy.linalg.block_diag(*rg_w_i)], axis=1)
    bri = jnp.concatenate([rg_b_r, rg_b_i]).reshape(1, 2 * w)
    alog = jnp.repeat(gdn_a_log, HEAD128).reshape(1, w)
    dtb = jnp.repeat(gdn_dt_bias, HEAD128).reshape(1, w)
    return w_rg.astype(BF16), w_gdn.astype(BF16), wri.astype(BF16), bri, alog, dtb


def _pad_groups(x, axis):
    w = GROUP_WIDTH
    sizes = (w, w, w, 64, 64, 128)
    parts = []
    start = 0
    for sz in sizes:
        piece = lax.slice_in_dim(x, start, start + sz, axis=axis)
        pad = (-sz) % R7_LORA_PAD
        if pad:
            cfg = [(0, 0)] * x.ndim
            cfg[axis] = (0, pad)
            piece = jnp.pad(piece, cfg)
        parts.append(piece)
        start += sz
    return jnp.concatenate(parts, axis=axis)


def _pad_rows(x, rows):
    return jnp.pad(x, ((0, rows - x.shape[0]), (0, 0)))


def _tiles(b, s):
    t = b * s
    tm_proj = 512 if t % 512 == 0 else 256
    tm_post = 256
    ts = 256 if s % 256 == 0 else 128
    return tm_proj, tm_post, ts


def kernel(x, p, norm_mix, norm_ffn, norm_ple, norm_final, e_w_in, rg_conv_w, rg_conv_b, rg_w_r, rg_b_r, rg_w_i, rg_b_i, rg_lambda, gdn_conv_w, gdn_a_log, gdn_dt_bias, gdn_norm_w, e_w_out, o_w_in, hg_lower_bounds, hg_norm_w, r7_mu, r7_w0, r7_w2, r7_a0, r7_a2, r7_g2, r7_k_k, r7_k_a, r7_r_k, r7_ln_w, r7_ln_b, o_w_out, ffn_w_gate, ffn_w_up, ffn_w_down, ple_w_up, ple_w_gate):
    b, s, d = x.shape
    depth = p.shape[0]
    t = b * s
    w = GROUP_WIDTH
    tm_proj, tm_post, ts = _tiles(b, s)
    h = x.reshape(t, d)
    row = lambda vec: vec.reshape(1, -1)
    for layer in range(depth):
        j = layer // 2
        if layer % 2 == 0:
            w_rg, w_gdn, wri, bri, alog, dtb = _even_weights(
                e_w_in[j], rg_w_r[j], rg_w_i[j], rg_b_r[j], rg_b_i[j], gdn_a_log[j], gdn_dt_bias[j])
            z_a, z_b = _norm_proj(h, norm_mix[layer], w_rg, w_gdn, tm_proj)
            out_a = _rglru(z_a.reshape(b, s, -1), rg_conv_w[j], row(rg_conv_b[j]), wri, bri,
                           row(rg_lambda[j]), ts)
            out_b = _gdn(z_b.reshape(b, s, -1), gdn_conv_w[j], alog, dtb, row(gdn_norm_w[j]), ts)
            w_out = e_w_out[j]
        else:
            hg_in = 4 * w
            w_hg = o_w_in[j][:, 0:hg_in].astype(BF16)
            w_r7 = _pad_groups(o_w_in[j][:, hg_in:], axis=1).astype(BF16)
            z_a, z_b = _norm_proj(h, norm_mix[layer], w_hg, w_r7, tm_proj)
            out_a = _hgrn2(z_a.reshape(b, s, -1), hg_lower_bounds, row(hg_norm_w[j]), j, ts)
            out_b = _rwkv7(z_b.reshape(b, s, -1), row(_pad_groups(r7_mu[j], axis=0)), row(r7_w0[j]),
                           _pad_rows(r7_w2[j], R7_LORA_PAD).astype(BF16), row(r7_a0[j]),
                           _pad_rows(r7_a2[j], R7_LORA_PAD).astype(BF16), r7_g2[j].astype(BF16),
                           row(r7_k_k[j]), row(r7_k_a[j]), row(r7_r_k[j].reshape(-1)),
                           row(r7_ln_w[j]), row(r7_ln_b[j]), ts)
            w_out = o_w_out[j]
        h = _post(h, out_a.reshape(t, w), out_b.reshape(t, w), p[layer].reshape(t, PLE_DIM),
                  w_out.astype(BF16), norm_ffn[layer], ffn_w_gate[layer].astype(BF16),
                  ffn_w_up[layer].astype(BF16), ffn_w_down[layer].astype(BF16), norm_ple[layer],
                  ple_w_gate[layer].astype(BF16), ple_w_up[layer].astype(BF16), norm_final,
                  layer == depth - 1, tm_post)
    return h.reshape(b, s, d)
```

```python
import functools
import math

import jax
import jax.numpy as jnp
from jax import lax
from jax.experimental import pallas as pl
from jax.experimental.pallas import tpu as pltpu

F32 = jnp.float32
BF16 = jnp.bfloat16

D_MODEL = 1024
PLE_DIM = 256
D_FF = 2816
GROUP_WIDTH = 512
RMS_EPS = 1e-6
L2_EPS = 1e-6
GN_EPS = 64e-5
RG_C = 8.0
RG_BLOCKS = 8
CONV_TAPS = 4
HEAD128 = 128
R7_N = 64
R7_LORA_PAD = 128
R7_IN_PAD = 3 * GROUP_WIDTH + 3 * R7_LORA_PAD

GDN_CHUNK = 128
HG_SUB = 16
R7_CHUNK = 64
R7_GROUP = 4
HALO = 8

VMEM_LIMIT = 56 * 1024 * 1024


def _sigmoid(x):
    return 1.0 / (1.0 + jnp.exp(-x))


def _softplus(x):
    return jnp.maximum(x, 0.0) + jnp.log1p(jnp.exp(-jnp.abs(x)))


def _silu(x):
    return x * _sigmoid(x)


def _gelu_tanh(x):
    c = math.sqrt(2.0 / math.pi)
    return 0.5 * x * (1.0 + jnp.tanh(c * (x + 0.044715 * (x * x * x))))


def _rms(x, w):
    ms = jnp.mean(x * x, axis=-1, keepdims=True)
    return x * lax.rsqrt(ms + RMS_EPS) * w


def _mm(a, b):
    return jnp.dot(a.astype(BF16), b.astype(BF16), preferred_element_type=F32)


def _mm_nt(a, b):
    return lax.dot_general(a.astype(BF16), b.astype(BF16), (((1,), (1,)), ((), ())),
                           preferred_element_type=F32)


def _split(x, terms):
    parts = []
    r = x
    for _ in range(terms):
        hi = r.astype(BF16)
        parts.append(hi)
        r = r - hi.astype(F32)
    return parts


def _sel_mm(sel, x, terms=3):
    acc = None
    for part in _split(x, terms):
        d = jnp.dot(sel, part, preferred_element_type=F32)
        acc = d if acc is None else acc + d
    return acc


def _mm_sel(x, sel, terms=2):
    acc = None
    for part in _split(x, terms):
        d = jnp.dot(part, sel, preferred_element_type=F32)
        acc = d if acc is None else acc + d
    return acc


def _mm_hi(a, b):
    a_hi, a_lo = _split(a, 2)
    b_hi, b_lo = _split(b, 2)
    return (jnp.dot(a_hi, b_hi, preferred_element_type=F32)
            + jnp.dot(a_lo, b_hi, preferred_element_type=F32)
            + jnp.dot(a_hi, b_lo, preferred_element_type=F32))


def _iota2(shape, axis):
    return lax.broadcasted_iota(jnp.int32, shape, axis)


def _block_tri(n, block, kind):
    t = _iota2((n, n), 0)
    s = _iota2((n, n), 1)
    same = (t // block) == (s // block)
    cond = (s <= t) if kind == "incl" else (s > t)
    return jnp.where(same & cond, 1.0, 0.0).astype(BF16)


def _block_ones(n, block):
    t = _iota2((n, n), 0)
    s = _iota2((n, n), 1)
    return jnp.where((t // block) == (s // block), 1.0, 0.0).astype(BF16)


def _nilpotent_inverses(xs, n, refine):
    shape = xs[0].shape
    eye = jnp.where(_iota2(shape, 0) == _iota2(shape, 1), 1.0, 0.0).astype(F32)
    ts = [eye + x for x in xs]
    ps = list(xs)
    k = 2
    while k < n:
        ps = [_mm(p, p) for p in ps]
        ts = [t + _mm(t, p) for t, p in zip(ts, ps)]
        k *= 2
    if refine:
        res = [(eye - t) + _mm_hi(x, t) for x, t in zip(xs, ts)]
        ts = [t + _mm(t, r) for t, r in zip(ts, res)]
    return ts


def _norm_proj_kernel(h_ref, nw_ref, wa_ref, wb_ref, za_ref, zb_ref):
    xn = _rms(h_ref[...], nw_ref[...]).astype(BF16)
    za_ref[...] = jnp.dot(xn, wa_ref[...], preferred_element_type=F32)
    zb_ref[...] = jnp.dot(xn, wb_ref[...], preferred_element_type=F32)


def _const_spec(shape):
    nd = len(shape)
    return pl.BlockSpec(shape, lambda *_: (0,) * nd, pipeline_mode=pl.Buffered(1))


def _norm_proj(h, norm_w, wa, wb, tm):
    t, d = h.shape
    fa, fb = wa.shape[1], wb.shape[1]
    return pl.pallas_call(
        _norm_proj_kernel,
        grid=(t // tm,),
        in_specs=[
            pl.BlockSpec((tm, d), lambda i: (i, 0)),
            _const_spec((1, d)),
            _const_spec((d, fa)),
            _const_spec((d, fb)),
        ],
        out_specs=[
            pl.BlockSpec((tm, fa), lambda i: (i, 0)),
            pl.BlockSpec((tm, fb), lambda i: (i, 0)),
        ],
        out_shape=[jax.ShapeDtypeStruct((t, fa), F32), jax.ShapeDtypeStruct((t, fb), F32)],
        compiler_params=pltpu.CompilerParams(
            dimension_semantics=("parallel",), vmem_limit_bytes=VMEM_LIMIT),
        name="norm_proj",
    )(h, norm_w.reshape(1, d), wa, wb)


def _post_kernel(final_norm, h_ref, ma_ref, mb_ref, p_ref, wo_ref, nf_ref, wg_ref, wu_ref, wd_ref,
                 npl_ref, wpg_ref, wpu_ref, nfin_ref, out_ref):
    h = h_ref[...]
    mix = (jnp.dot(ma_ref[...], wo_ref[0:GROUP_WIDTH, :], preferred_element_type=F32)
           + jnp.dot(mb_ref[...], wo_ref[GROUP_WIDTH:2 * GROUP_WIDTH, :], preferred_element_type=F32))
    h1 = h + mix
    hn = _rms(h1, nf_ref[...]).astype(BF16)
    g = jnp.dot(hn, wg_ref[...], preferred_element_type=F32)
    u = jnp.dot(hn, wu_ref[...], preferred_element_type=F32)
    act = (_silu(g) * u).astype(BF16)
    h2 = h1 + jnp.dot(act, wd_ref[...], preferred_element_type=F32)
    hp = _rms(h2, npl_ref[...]).astype(BF16)
    gate = _sigmoid(jnp.dot(hp, wpg_ref[...], preferred_element_type=F32))
    up = jnp.dot(p_ref[...].astype(BF16), wpu_ref[...], preferred_element_type=F32)
    h3 = h2 + gate * up
    if final_norm:
        h3 = _rms(h3, nfin_ref[...])
    out_ref[...] = h3


def _post(h, ma, mb, p, wo, nf, wg, wu, wd, npl, wpg, wpu, nfin, final_norm, tm):
    t, d = h.shape
    row = lambda w: pl.BlockSpec((tm, w), lambda i: (i, 0))
    return pl.pallas_call(
        functools.partial(_post_kernel, final_norm),
        grid=(t // tm,),
        in_specs=[
            row(d), row(GROUP_WIDTH), row(GROUP_WIDTH), row(PLE_DIM),
            _const_spec(wo.shape), _const_spec((1, d)),
            _const_spec(wg.shape), _const_spec(wu.shape), _const_spec(wd.shape),
            _const_spec((1, d)), _const_spec(wpg.shape), _const_spec(wpu.shape),
            _const_spec((1, d)),
        ],
        out_specs=row(d),
        out_shape=jax.ShapeDtypeStruct((t, d), F32),
        compiler_params=pltpu.CompilerParams(
            dimension_semantics=("parallel",), vmem_limit_bytes=VMEM_LIMIT),
        name="post_mixer",
    )(h, ma, mb, p, wo, nf.reshape(1, d), wg, wu, wd, npl.reshape(1, d), wpg, wpu, nfin.reshape(1, d))


def _causal_conv(xbuf_ref, x, w_ref, ts):
    xbuf_ref[HALO:HALO + ts, :] = x
    acc = None
    for j in range(CONV_TAPS):
        off = HALO - (CONV_TAPS - 1) + j
        term = w_ref[j:j + 1, :] * xbuf_ref[off:off + ts, :]
        acc = term if acc is None else acc + term
    xbuf_ref[0:HALO, :] = xbuf_ref[ts:ts + HALO, :]
    return acc


def _rglru_kernel(ts, z_ref, cw_ref, cb_ref, wri_ref, bri_ref, lam_ref, out_ref, xbuf_ref, hc_ref):
    w = GROUP_WIDTH
    tstep = pl.program_id(1)

    @pl.when(tstep == 0)
    def _():
        xbuf_ref[0:HALO, :] = jnp.zeros((HALO, w), F32)
        hc_ref[...] = jnp.zeros_like(hc_ref)

    xa = z_ref[0, :, 0:w]
    ya = z_ref[0, :, w:2 * w]
    x = _causal_conv(xbuf_ref, xa, cw_ref, ts) + cb_ref[...]
    ri = jnp.dot(x.astype(BF16), wri_ref[...], preferred_element_type=F32) + bri_ref[...]
    r = _sigmoid(ri[:, 0:w])
    i = _sigmoid(ri[:, w:2 * w])
    log_a = -RG_C * r * _softplus(-lam_ref[...])
    a = jnp.exp(log_a)
    mult = jnp.sqrt(-jnp.tanh(log_a) * (a * a + 1.0))
    row = _iota2((ts, w), 0)
    mult = jnp.where((row == 0) & (tstep == 0), 1.0, mult)
    u = mult * (i * x)
    acc_a, acc_h = a, u
    d = 1
    while d < ts:
        if d % HALO == 0:
            sh_a = jnp.concatenate([jnp.ones((d, w), F32), acc_a[0:ts - d, :]], axis=0)
            sh_h = jnp.concatenate([jnp.zeros((d, w), F32), acc_h[0:ts - d, :]], axis=0)
        else:
            sh_a = jnp.where(row >= d, pltpu.roll(acc_a, d, 0), 1.0)
            sh_h = jnp.where(row >= d, pltpu.roll(acc_h, d, 0), 0.0)
        acc_h = acc_a * sh_h + acc_h
        acc_a = acc_a * sh_a
        d *= 2
    h = acc_h + acc_a * hc_ref[...]
    hc_ref[...] = h[ts - 1:ts, :]
    out_ref[0] = (h * _gelu_tanh(ya)).astype(out_ref.dtype)


def _rglru(z, cw, cb, wri, bri, lam, ts):
    b, s, _ = z.shape
    w = GROUP_WIDTH
    return pl.pallas_call(
        functools.partial(_rglru_kernel, ts),
        grid=(b, s // ts),
        in_specs=[
            pl.BlockSpec((1, ts, 2 * w), lambda bi, ti: (bi, ti, 0)),
            _const_spec((CONV_TAPS, w)), _const_spec((1, w)),
            _const_spec((w, 2 * w)), _const_spec((1, 2 * w)), _const_spec((1, w)),
        ],
        out_specs=pl.BlockSpec((1, ts, w), lambda bi, ti: (bi, ti, 0)),
        out_shape=jax.ShapeDtypeStruct((b, s, w), BF16),
        scratch_shapes=[pltpu.VMEM((HALO + ts, w), F32), pltpu.VMEM((1, w), F32)],
        compiler_params=pltpu.CompilerParams(
            dimension_semantics=("parallel", "arbitrary"), vmem_limit_bytes=VMEM_LIMIT),
        name="rglru",
    )(z, cw, cb, wri, bri, lam)


def _gdn_kernel(ts, z_ref, cw_ref, alog_ref, dtb_ref, nw_ref, out_ref, xbuf_ref, st_ref):
    w = GROUP_WIDTH
    hd = HEAD128
    nh = w // hd
    cs = GDN_CHUNK
    nchunk = ts // cs
    tstep = pl.program_id(1)

    @pl.when(tstep == 0)
    def _():
        xbuf_ref[0:HALO, :] = jnp.zeros((HALO, 3 * w), F32)
        st_ref[...] = jnp.zeros_like(st_ref)

    qkv = _silu(_causal_conv(xbuf_ref, z_ref[0, :, 0:3 * w], cw_ref, ts))
    zg = z_ref[0, :, 3 * w:4 * w]
    a_in = z_ref[0, :, 4 * w:5 * w]
    b_in = z_ref[0, :, 5 * w:6 * w]
    beta = _sigmoid(b_in)
    g = -jnp.exp(alog_ref[...]) * _softplus(a_in + dtb_ref[...])
    gc_all = _sel_mm(_block_tri(ts, cs, "incl"), g)
    gs_all = _sel_mm(_block_tri(ts, cs, "suffix"), g)
    eg_all = jnp.exp(gc_all)
    scale = hd ** -0.5
    tt = _iota2((cs, cs), 0)
    ss = _iota2((cs, cs), 1)

    items = [(h, c) for c in range(nchunk) for h in range(nh)]
    qs, ks, vbs, kbegs, decs, kdts, glast = [], [], [], [], [], [], []
    for h, c in items:
        lo = h * hd
        rows = slice(c * cs, (c + 1) * cs)
        hs = slice(lo, lo + hd)
        qh = qkv[rows, lo:lo + hd]
        kh = qkv[rows, w + lo:w + lo + hd]
        vh = qkv[rows, 2 * w + lo:2 * w + lo + hd]
        qh = qh * lax.rsqrt(jnp.sum(qh * qh, axis=-1, keepdims=True) + L2_EPS) * scale
        kh = kh * lax.rsqrt(jnp.sum(kh * kh, axis=-1, keepdims=True) + L2_EPS)
        gc = gc_all[rows, hs]
        bt = beta[rows, hs]
        eg = eg_all[rows, hs]
        kb = kh * bt
        diff = gc - gc.T
        decs.append(jnp.exp(jnp.where(ss <= tt, diff, -jnp.inf)))
        qs.append(qh * eg)
        ks.append((kh, kb, qh))
        vbs.append(vh * bt)
        kbegs.append(kb * eg)
        kdts.append((kh * jnp.exp(gs_all[rows, hs])).T)
        glast.append(jnp.exp(gc_all[(c + 1) * cs - 1:(c + 1) * cs, hs]))
    a_mats = [jnp.where(ss < tt, _mm_nt(kb, kh) * dec, 0.0) for (kh, kb, _), dec in zip(ks, decs)]
    qks = [_mm_nt(qh, kh) * dec for (kh, _, qh), dec in zip(ks, decs)]
    tinvs = _nilpotent_inverses([-a for a in a_mats], cs, refine=True)
    sols = [_mm(t, jnp.concatenate([vb, kbeg], axis=1)) for t, vb, kbeg in zip(tinvs, vbs, kbegs)]
    us = [sol[:, 0:hd] for sol in sols]
    ws = [sol[:, hd:2 * hd] for sol in sols]

    states = [st_ref[h] for h in range(nh)]
    outs = [[] for _ in range(nh)]
    for c in range(nchunk):
        idx = [c * nh + h for h in range(nh)]
        proj = [_mm(jnp.concatenate([ws[i], qs[i]], axis=0), states[h]) for h, i in enumerate(idx)]
        v_new = [us[i] - pr[0:cs] for i, pr in zip(idx, proj)]
        o = [pr[cs:2 * cs] + _mm(qks[i], vn) for i, pr, vn in zip(idx, proj, v_new)]
        states = [glast[i] * states[h] + _mm(kdts[i], vn) for (h, i), vn in zip(enumerate(idx), v_new)]
        for h in range(nh):
            outs[h].append(o[h])
    for h in range(nh):
        lo = h * hd
        st_ref[h] = states[h]
        o = jnp.concatenate(outs[h], axis=0) if nchunk > 1 else outs[h][0]
        o = _rms(o, nw_ref[...]) * _silu(zg[:, lo:lo + hd])
        out_ref[0, :, lo:lo + hd] = o.astype(out_ref.dtype)


def _gdn(z, cw, alog, dtb, nw, ts):
    b, s, _ = z.shape
    w = GROUP_WIDTH
    nh = w // HEAD128
    return pl.pallas_call(
        functools.partial(_gdn_kernel, ts),
        grid=(b, s // ts),
        in_specs=[
            pl.BlockSpec((1, ts, 6 * w), lambda bi, ti: (bi, ti, 0)),
            _const_spec((CONV_TAPS, 3 * w)), _const_spec((1, w)), _const_spec((1, w)),
            _const_spec((1, HEAD128)),
        ],
        out_specs=pl.BlockSpec((1, ts, w), lambda bi, ti: (bi, ti, 0)),
        out_shape=jax.ShapeDtypeStruct((b, s, w), BF16),
        scratch_shapes=[pltpu.VMEM((HALO + ts, 3 * w), F32), pltpu.VMEM((nh, HEAD128, HEAD128), F32)],
        compiler_params=pltpu.CompilerParams(
            dimension_semantics=("parallel", "arbitrary"), vmem_limit_bytes=VMEM_LIMIT),
        name="gated_deltanet",
    )(z, cw, alog, dtb, nw)


def _hgrn2_kernel(ts, layer_j, z_ref, lbraw_ref, nw_ref, out_ref, st_ref, o_ref):
    w = GROUP_WIDTH
    hd = HEAD128
    nh = w // hd
    sub = HG_SUB
    nsub = ts // sub
    tstep = pl.program_id(1)

    @pl.when(tstep == 0)
    def _():
        st_ref[...] = jnp.zeros_like(st_ref)

    lbraw = lbraw_ref[...]
    e = jnp.exp(lbraw - jnp.max(lbraw, axis=0, keepdims=True))
    soft = e / jnp.sum(e, axis=0, keepdims=True)
    lb = jnp.sum(soft[0:layer_j + 1, :], axis=0, keepdims=True) - soft[0:1, :]
    lb = jnp.maximum(lb, 0.0)

    q = _silu(z_ref[0, :, 0:w])
    f = z_ref[0, :, w:2 * w]
    v = z_ref[0, :, 2 * w:3 * w]
    gate = z_ref[0, :, 3 * w:4 * w]
    kin = (1.0 - lb) * _sigmoid(-f)
    la = jnp.log(lb)
    lbm = jnp.log1p(-lb) - _softplus(-f)
    mx = jnp.maximum(la, lbm)
    log_f = mx + jnp.log1p(jnp.exp(-jnp.abs(la - lbm)))

    bl = _sel_mm(_block_tri(ts, sub, "incl"), log_f)
    bsuf = _sel_mm(_block_tri(ts, sub, "suffix"), log_f)
    qe = q * jnp.exp(bl)
    kh = kin * jnp.exp(bsuf)

    ones_bd = _block_ones(w, hd)
    q3 = q.reshape(nsub, sub, w)
    k3 = kin.reshape(nsub, sub, w)
    v3 = v.reshape(nsub, sub, w)
    b3 = (bl * math.log2(math.e)).reshape(nsub, sub, w)
    half = sub // 2
    tpos = _iota2((nsub, half, w), 1)
    q_lo, q_hi = q3[:, 0:half, :], q3[:, half:sub, :]
    b_lo, b_hi = b3[:, 0:half, :], b3[:, half:sub, :]
    o_lo = jnp.zeros((nsub, half, w), F32)
    o_hi = jnp.zeros((nsub, half, w), F32)
    for s in range(sub):
        bs = b3[:, s:s + 1, :]
        ksrc = k3[:, s:s + 1, :]
        vsrc = v3[:, s:s + 1, :]
        if s < half:
            rel_lo = jnp.exp2(jnp.where(tpos >= s, b_lo - bs, -jnp.inf))
            prod = jnp.concatenate([q_lo * ksrc * rel_lo, q_hi * ksrc * jnp.exp2(b_hi - bs)], axis=1)
            sc = _mm_sel(prod.reshape(ts, w), ones_bd, terms=1).reshape(nsub, sub, w)
            o_lo = o_lo + sc[:, 0:half, :] * vsrc
            o_hi = o_hi + sc[:, half:sub, :] * vsrc
        else:
            rel_hi = jnp.exp2(jnp.where(tpos >= s - half, b_hi - bs, -jnp.inf))
            prod = q_hi * ksrc * rel_hi
            sc = _mm_sel(prod.reshape(nsub * half, w), ones_bd, terms=1).reshape(nsub, half, w)
            o_hi = o_hi + sc * vsrc
    o_intra = jnp.concatenate([o_lo, o_hi], axis=1).reshape(ts, w)

    dec_sub = jnp.exp(bl + bsuf)
    blk = 128
    per_blk = blk // sub
    incs = [[] for _ in range(nh)]
    for h in range(nh):
        lo = h * hd
        for b0 in range(0, ts, blk):
            vt = v[b0:b0 + blk, lo:lo + hd].T.astype(BF16)
            for i in range(per_blk):
                r0 = b0 + i * sub
                parts = []
                if i > 0:
                    parts.append(jnp.zeros((i * sub, hd), BF16))
                parts.append(kh[r0:r0 + sub, lo:lo + hd].astype(BF16))
                if i + 1 < per_blk:
                    parts.append(jnp.zeros((blk - (i + 1) * sub, hd), BF16))
                kpad = jnp.concatenate(parts, axis=0) if len(parts) > 1 else parts[0]
                incs[h].append(jnp.dot(vt, kpad, preferred_element_type=F32))
    seen = [[] for _ in range(nh)]
    for h in range(nh):
        lo = h * hd
        st = st_ref[h]
        for i in range(nsub):
            seen[h].append(st.astype(BF16))
            st = st * dec_sub[i * sub:i * sub + 1, lo:lo + hd] + incs[h][i]
        st_ref[h] = st
    for h in range(nh):
        lo = h * hd
        o_inter = jnp.concatenate(
            [_mm_nt(qe[i * sub:(i + 1) * sub, lo:lo + hd], seen[h][i]) for i in range(nsub)], axis=0)
        o_ref[:, lo:lo + hd] = o_intra[:, lo:lo + hd] + o_inter

    out_ref[0] = (_rms(o_ref[...], nw_ref[...]) * _sigmoid(gate)).astype(out_ref.dtype)


def _hgrn2(z, lbraw, nw, layer_j, ts):
    b, s, _ = z.shape
    w = GROUP_WIDTH
    nh = w // HEAD128
    return pl.pallas_call(
        functools.partial(_hgrn2_kernel, ts, layer_j),
        grid=(b, s // ts),
        in_specs=[
            pl.BlockSpec((1, ts, 4 * w), lambda bi, ti: (bi, ti, 0)),
            _const_spec(lbraw.shape), _const_spec((1, w)),
        ],
        out_specs=pl.BlockSpec((1, ts, w), lambda bi, ti: (bi, ti, 0)),
        out_shape=jax.ShapeDtypeStruct((b, s, w), BF16),
        scratch_shapes=[pltpu.VMEM((nh, HEAD128, HEAD128), F32), pltpu.VMEM((ts, w), F32)],
        compiler_params=pltpu.CompilerParams(
            dimension_semantics=("parallel", "arbitrary"), vmem_limit_bytes=VMEM_LIMIT),
        name="hgrn2",
    )(z, lbraw, nw)


def _rwkv7_kernel(ts, z_ref, mu_ref, w0_ref, w2_ref, a0_ref, a2_ref, g2_ref, kk_ref, ka_ref, rk_ref,
                  lnw_ref, lnb_ref, out_ref, xbuf_ref, st_ref):
    w = GROUP_WIDTH
    lp = R7_LORA_PAD
    ch = R7_CHUNK
    npair = w // (2 * R7_N)
    tstep = pl.program_id(1)

    @pl.when(tstep == 0)
    def _():
        xbuf_ref[0:HALO, :] = jnp.zeros((HALO, R7_IN_PAD), F32)
        st_ref[...] = jnp.zeros_like(st_ref)

    pw = 2 * R7_N
    n = R7_N
    gch = min(R7_GROUP, ts // ch)
    grows = gch * ch
    lane = _iota2((ch, pw), 1)
    in_h0 = lane < n
    t_row = _iota2((ch, pw), 0)
    s_col = lane % n
    ones_bd = _block_ones(w, R7_N)
    tri_incl = _block_tri(grows, ch, "incl")
    tri_suf = _block_tri(grows, ch, "suffix")
    items = [(c, p) for c in range(gch) for p in range(npair)]
    sl = lambda c, p: (slice(c * ch, (c + 1) * ch), slice(p * pw, (p + 1) * pw))

    def stack_heads(x):
        return jnp.concatenate([jnp.where(in_h0, x, 0.0), jnp.where(in_h0, 0.0, x)], axis=0)

    xbuf_ref[HALO:HALO + ts, :] = z_ref[0]
    states = [st_ref[p] for p in range(npair)]
    for g0 in range(0, ts, grows):
        z = xbuf_ref[HALO + g0:HALO + g0 + grows, :]
        z_prev = xbuf_ref[HALO - 1 + g0:HALO - 1 + g0 + grows, :]
        zd = z + mu_ref[...] * (z_prev - z)
        r = zd[:, 0:w]
        k = zd[:, w:2 * w]
        v = zd[:, 2 * w:3 * w]
        w_l = zd[:, 3 * w:3 * w + lp]
        a_l = zd[:, 3 * w + lp:3 * w + 2 * lp]
        g_l = zd[:, 3 * w + 2 * lp:3 * w + 3 * lp]
        log_w = -jnp.exp(-_softplus(-(w0_ref[...] + _mm(jnp.tanh(w_l), w2_ref[...]))) - 0.5)
        a = _sigmoid(a0_ref[...] + _mm(a_l, a2_ref[...]))
        gate = _mm(_sigmoid(g_l), g2_ref[...])
        kk = k * kk_ref[...]
        kk = kk * lax.rsqrt(_mm_sel(kk * kk, ones_bd) + L2_EPS)
        k = k * (1.0 + (a - 1.0) * ka_ref[...])
        cum = _sel_mm(tri_incl, log_w)
        suf = _sel_mm(tri_suf, log_w)
        e_inv = jnp.exp(-cum)
        r_t = r * jnp.exp(cum)
        a_t = -kk * jnp.exp(cum - log_w)
        kka = kk * a
        b_h = kka * e_inv
        k_h = k * e_inv
        e_suf = jnp.exp(suf)
        b_dc = kka * e_suf
        k_dc = k * e_suf
        d_end = jnp.exp(cum + suf)

        v_ss = [stack_heads(v[sl(c, p)]) for c, p in items]
        grams = [
            _mm_nt(jnp.concatenate([a_t[sl(c, p)], r_t[sl(c, p)]], axis=0),
                   jnp.concatenate([stack_heads(b_h[sl(c, p)]), stack_heads(k_h[sl(c, p)])], axis=0))
            for c, p in items]
        a_abs = [jnp.where(s_col < t_row, g[0:ch, 0:pw], 0.0) for g in grams]
        a_rbs = [jnp.where(s_col <= t_row, g[ch:2 * ch, 0:pw], 0.0) for g in grams]
        a_k = [jnp.concatenate([jnp.where(s_col < t_row, g[0:ch, pw:2 * pw], 0.0),
                                jnp.where(s_col <= t_row, g[ch:2 * ch, pw:2 * pw], 0.0)], axis=0)
               for g in grams]
        tinvs = _nilpotent_inverses([stack_heads(x) for x in a_abs], ch, refine=False)
        kv = [_mm(x, vs) for x, vs in zip(a_k, v_ss)]
        w_ss = [_mm(t, jnp.concatenate([stack_heads(a_t[sl(c, p)]), stack_heads(x[0:ch])], axis=1))
                for t, x, (c, p) in zip(tinvs, kv, items)]
        ry = [_mm(x, w_s) for x, w_s in zip(a_rbs, w_ss)]
        r_eff = [r_t[sl(c, p)] + x[:, 0:pw] for x, (c, p) in zip(ry, items)]
        y0 = [x[:, pw:2 * pw] + y[ch:2 * ch] for x, y in zip(ry, kv)]
        bdc_ss = [stack_heads(b_dc[sl(c, p)]) for c, p in items]
        phi_t = [_mm(w_s[:, 0:pw].T, bd).astype(BF16) for w_s, bd in zip(w_ss, bdc_ss)]
        gamma_t = [_mm(jnp.concatenate([w_s[:, pw:2 * pw].T, vs.T], axis=1),
                       jnp.concatenate([bd, stack_heads(k_dc[sl(c, p)])], axis=0))
                   for w_s, vs, bd, (c, p) in zip(w_ss, v_ss, bdc_ss, items)]

        ys = [[] for _ in range(npair)]
        for c in range(gch):
            st_bs = [st.astype(BF16) for st in states]
            states = [states[p] * d_end[c * ch:c * ch + 1, p * pw:(p + 1) * pw]
                      + jnp.dot(st_bs[p], phi_t[c * npair + p], preferred_element_type=F32)
                      + gamma_t[c * npair + p] for p in range(npair)]
            for p in range(npair):
                i = c * npair + p
                ys[p].append(_mm_nt(r_eff[i], st_bs[p]) + y0[i])
        y = jnp.concatenate([jnp.concatenate(yp, axis=0) if gch > 1 else yp[0] for yp in ys], axis=1)

        mean = _mm_sel(y, ones_bd) * (1.0 / R7_N)
        yc = y - mean
        var = _mm_sel(yc * yc, ones_bd) * (1.0 / R7_N)
        yn = yc * lax.rsqrt(var + GN_EPS) * lnw_ref[...] + lnb_ref[...]
        bonus = _mm_sel(r * k * rk_ref[...], ones_bd) * v
        out_ref[0, g0:g0 + grows, :] = ((yn + bonus) * gate).astype(out_ref.dtype)
    for p in range(npair):
        st_ref[p] = states[p]
    xbuf_ref[0:HALO, :] = xbuf_ref[ts:ts + HALO, :]


def _rwkv7(z, mu, w0, w2, a0, a2, g2, k_k, k_a, r_k, ln_w, ln_b, ts):
    b, s, fin = z.shape
    w = GROUP_WIDTH
    npair = w // (2 * R7_N)
    vec = _const_spec((1, w))
    return pl.pallas_call(
        functools.partial(_rwkv7_kernel, ts),
        grid=(b, s // ts),
        in_specs=[
            pl.BlockSpec((1, ts, fin), lambda bi, ti: (bi, ti, 0)),
            _const_spec((1, fin)), vec, _const_spec(w2.shape), vec, _const_spec(a2.shape),
            _const_spec(g2.shape), vec, vec, vec, vec, vec,
        ],
        out_specs=pl.BlockSpec((1, ts, w), lambda bi, ti: (bi, ti, 0)),
        out_shape=jax.ShapeDtypeStruct((b, s, w), BF16),
        scratch_shapes=[pltpu.VMEM((HALO + ts, fin), F32), pltpu.VMEM((npair, 2 * R7_N, 2 * R7_N), F32)],
        compiler_params=pltpu.CompilerParams(
            dimension_semantics=("parallel", "arbitrary"), vmem_limit_bytes=VMEM_LIMIT),
        name="rwkv7",
    )(z, mu, w0, w2, a0, a2, g2, k_k, k_a, r_k, ln_w, ln_b)


def _even_weights(w_in, rg_w_r, rg_w_i, rg_b_r, rg_b_i, gdn_a_log, gdn_dt_bias):
    w = GROUP_WIDTH
    nh = w // HEAD128
    rep = lambda cols: jnp.repeat(cols, HEAD128, axis=1)
    w_rg = w_in[:, 0:2 * w]
    w_gdn = jnp.concatenate(
        [w_in[:, 2 * w:6 * w], rep(w_in[:, 6 * w:6 * w + nh]), rep(w_in[:, 6 * w + nh:6 * w + 2 * nh])], axis=1)
    wri = jnp.concatenate([jax.scipy.linalg.block_diag(*rg_w_r), jax.scipy.linalg.block_diag(*rg_w_i)], axis=1)
    bri = jnp.concatenate([rg_b_r, rg_b_i]).reshape(1, 2 * w)
    alog = jnp.repeat(gdn_a_log, HEAD128).reshape(1, w)
    dtb = jnp.repeat(gdn_dt_bias, HEAD128).reshape(1, w)
    return w_rg.astype(BF16), w_gdn.astype(BF16), wri.astype(BF16), bri, alog, dtb


def _pad_groups(x, axis):
    w = GROUP_WIDTH
    sizes = (w, w, w, 64, 64, 128)
    parts = []
    start = 0
    for sz in sizes:
        piece = lax.slice_in_dim(x, start, start + sz, axis=axis)
        pad = (-sz) % R7_LORA_PAD
        if pad:
            cfg = [(0, 0)] * x.ndim
            cfg[axis] = (0, pad)
            piece = jnp.pad(piece, cfg)
        parts.append(piece)
        start += sz
    return jnp.concatenate(parts, axis=axis)


def _pad_rows(x, rows):
    return jnp.pad(x, ((0, rows - x.shape[0]), (0, 0)))


def _tiles(b, s):
    t = b * s
    tm_proj = 512 if t % 512 == 0 else 256
    tm_post = 256
    ts = 256 if s % 256 == 0 else 128
    return tm_proj, tm_post, ts


def kernel(x, p, norm_mix, norm_ffn, norm_ple, norm_final, e_w_in, rg_conv_w, rg_conv_b, rg_w_r, rg_b_r, rg_w_i, rg_b_i, rg_lambda, gdn_conv_w, gdn_a_log, gdn_dt_bias, gdn_norm_w, e_w_out, o_w_in, hg_lower_bounds, hg_norm_w, r7_mu, r7_w0, r7_w2, r7_a0, r7_a2, r7_g2, r7_k_k, r7_k_a, r7_r_k, r7_ln_w, r7_ln_b, o_w_out, ffn_w_gate, ffn_w_up, ffn_w_down, ple_w_up, ple_w_gate):
    b, s, d = x.shape
    depth = p.shape[0]
    t = b * s
    w = GROUP_WIDTH
    tm_proj, tm_post, ts = _tiles(b, s)
    h = x.reshape(t, d)
    row = lambda vec: vec.reshape(1, -1)
    for layer in range(depth):
        j = layer // 2
        if layer % 2 == 0:
            w_rg, w_gdn, wri, bri, alog, dtb = _even_weights(
                e_w_in[j], rg_w_r[j], rg_w_i[j], rg_b_r[j], rg_b_i[j], gdn_a_log[j], gdn_dt_bias[j])
            z_a, z_b = _norm_proj(h, norm_mix[layer], w_rg, w_gdn, tm_proj)
            out_a = _rglru(z_a.reshape(b, s, -1), rg_conv_w[j], row(rg_conv_b[j]), wri, bri,
                           row(rg_lambda[j]), ts)
            out_b = _gdn(z_b.reshape(b, s, -1), gdn_conv_w[j], alog, dtb, row(gdn_norm_w[j]),
                         2 * ts if s % (2 * ts) == 0 else ts)
            w_out = e_w_out[j]
        else:
            hg_in = 4 * w
            w_hg = o_w_in[j][:, 0:hg_in].astype(BF16)
            w_r7 = _pad_groups(o_w_in[j][:, hg_in:], axis=1).astype(BF16)
            z_a, z_b = _norm_proj(h, norm_mix[layer], w_hg, w_r7, tm_proj)
            out_a = _hgrn2(z_a.reshape(b, s, -1), hg_lower_bounds, row(hg_norm_w[j]), j, ts)
            out_b = _rwkv7(z_b.reshape(b, s, -1), row(_pad_groups(r7_mu[j], axis=0)), row(r7_w0[j]),
                           _pad_rows(r7_w2[j], R7_LORA_PAD).astype(BF16), row(r7_a0[j]),
                           _pad_rows(r7_a2[j], R7_LORA_PAD).astype(BF16), r7_g2[j].astype(BF16),
                           row(r7_k_k[j]), row(r7_k_a[j]), row(r7_r_k[j].reshape(-1)),
                           row(r7_ln_w[j]), row(r7_ln_b[j]), ts)
            w_out = o_w_out[j]
        h = _post(h, out_a.reshape(t, w), out_b.reshape(t, w), p[layer].reshape(t, PLE_DIM),
                  w_out.astype(BF16), norm_ffn[layer], ffn_w_gate[layer].astype(BF16),
                  ffn_w_up[layer].astype(BF16), ffn_w_down[layer].astype(BF16), norm_ple[layer],
                  ple_w_gate[layer].astype(BF16), ple_w_up[layer].astype(BF16), norm_final,
                  layer == depth - 1, tm_post)
    return h.reshape(b, s, d)
```

```python
import functools
import math

import jax
import jax.numpy as jnp
from jax import lax
from jax.experimental import pallas as pl
from jax.experimental.pallas import tpu as pltpu

F32 = jnp.float32
BF16 = jnp.bfloat16

D_MODEL = 1024
PLE_DIM = 256
D_FF = 2816
GROUP_WIDTH = 512
RMS_EPS = 1e-6
L2_EPS = 1e-6
GN_EPS = 64e-5
RG_C = 8.0
RG_BLOCKS = 8
RG_ROWS = 64
CONV_TAPS = 4
HEAD128 = 128
R7_N = 64
R7_LORA_PAD = 128
R7_IN_PAD = 3 * GROUP_WIDTH + 3 * R7_LORA_PAD

GDN_CHUNK = 128
HG_SUB = 16
R7_CHUNK = 64
R7_GROUP = 4
HALO = 8

VMEM_LIMIT = 56 * 1024 * 1024


def _sigmoid(x):
    return 1.0 / (1.0 + jnp.exp(-x))


def _softplus(x):
    return jnp.maximum(x, 0.0) + jnp.log1p(jnp.exp(-jnp.abs(x)))


def _silu(x):
    return x * _sigmoid(x)


def _gelu_tanh(x):
    c = math.sqrt(2.0 / math.pi)
    return 0.5 * x * (1.0 + jnp.tanh(c * (x + 0.044715 * (x * x * x))))


def _rms(x, w):
    ms = jnp.mean(x * x, axis=-1, keepdims=True)
    return x * lax.rsqrt(ms + RMS_EPS) * w


def _mm(a, b):
    return jnp.dot(a.astype(BF16), b.astype(BF16), preferred_element_type=F32)


def _mm_nt(a, b):
    return lax.dot_general(a.astype(BF16), b.astype(BF16), (((1,), (1,)), ((), ())),
                           preferred_element_type=F32)


def _split(x, terms):
    parts = []
    r = x
    for _ in range(terms):
        hi = r.astype(BF16)
        parts.append(hi)
        r = r - hi.astype(F32)
    return parts


def _sel_mm(sel, x, terms=3):
    acc = None
    for part in _split(x, terms):
        d = jnp.dot(sel, part, preferred_element_type=F32)
        acc = d if acc is None else acc + d
    return acc


def _mm_sel(x, sel, terms=2):
    acc = None
    for part in _split(x, terms):
        d = jnp.dot(part, sel, preferred_element_type=F32)
        acc = d if acc is None else acc + d
    return acc


def _mm_hi(a, b):
    a_hi, a_lo = _split(a, 2)
    b_hi, b_lo = _split(b, 2)
    return (jnp.dot(a_hi, b_hi, preferred_element_type=F32)
            + jnp.dot(a_lo, b_hi, preferred_element_type=F32)
            + jnp.dot(a_hi, b_lo, preferred_element_type=F32))


def _iota2(shape, axis):
    return lax.broadcasted_iota(jnp.int32, shape, axis)


def _block_tri(n, block, kind):
    t = _iota2((n, n), 0)
    s = _iota2((n, n), 1)
    same = (t // block) == (s // block)
    cond = (s <= t) if kind == "incl" else (s > t)
    return jnp.where(same & cond, 1.0, 0.0).astype(BF16)


def _chunk_last(x, chunk):
    cols = x.shape[1]
    return jnp.concatenate(
        [jnp.broadcast_to(x[r + chunk - 1:r + chunk, :], (chunk, cols)) for r in range(0, x.shape[0], chunk)],
        axis=0)


def _block_ones(n, block):
    t = _iota2((n, n), 0)
    s = _iota2((n, n), 1)
    return jnp.where((t // block) == (s // block), 1.0, 0.0).astype(BF16)


def _nilpotent_inverses(xs, n, refine):
    shape = xs[0].shape
    eye = jnp.where(_iota2(shape, 0) == _iota2(shape, 1), 1.0, 0.0).astype(F32)
    ts = [eye + x for x in xs]
    ps = list(xs)
    k = 2
    while k < n:
        ps = [_mm(p, p) for p in ps]
        ts = [t + _mm(t, p) for t, p in zip(ts, ps)]
        k *= 2
    if refine:
        res = [(eye - t) + _mm_hi(x, t) for x, t in zip(xs, ts)]
        ts = [t + _mm(t, r) for t, r in zip(ts, res)]
    return ts


def _norm_proj_kernel(h_ref, nw_ref, wa_ref, wb_ref, za_ref, zb_ref):
    xn = _rms(h_ref[...], nw_ref[...]).astype(BF16)
    za_ref[...] = jnp.dot(xn, wa_ref[...], preferred_element_type=F32)
    zb_ref[...] = jnp.dot(xn, wb_ref[...], preferred_element_type=F32)


def _const_spec(shape):
    nd = len(shape)
    return pl.BlockSpec(shape, lambda *_: (0,) * nd, pipeline_mode=pl.Buffered(1))


def _norm_proj(h, norm_w, wa, wb, tm):
    t, d = h.shape
    fa, fb = wa.shape[1], wb.shape[1]
    return pl.pallas_call(
        _norm_proj_kernel,
        grid=(t // tm,),
        in_specs=[
            pl.BlockSpec((tm, d), lambda i: (i, 0)),
            _const_spec((1, d)),
            _const_spec((d, fa)),
            _const_spec((d, fb)),
        ],
        out_specs=[
            pl.BlockSpec((tm, fa), lambda i: (i, 0)),
            pl.BlockSpec((tm, fb), lambda i: (i, 0)),
        ],
        out_shape=[jax.ShapeDtypeStruct((t, fa), F32), jax.ShapeDtypeStruct((t, fb), F32)],
        compiler_params=pltpu.CompilerParams(
            dimension_semantics=("parallel",), vmem_limit_bytes=VMEM_LIMIT),
        name="norm_proj",
    )(h, norm_w.reshape(1, d), wa, wb)


def _post_kernel(final_norm, h_ref, ma_ref, mb_ref, p_ref, wo_ref, nf_ref, wg_ref, wu_ref, wd_ref,
                 npl_ref, wpg_ref, wpu_ref, nfin_ref, out_ref):
    h = h_ref[...]
    mix = (jnp.dot(ma_ref[...], wo_ref[0:GROUP_WIDTH, :], preferred_element_type=F32)
           + jnp.dot(mb_ref[...], wo_ref[GROUP_WIDTH:2 * GROUP_WIDTH, :], preferred_element_type=F32))
    h1 = h + mix
    hn = _rms(h1, nf_ref[...]).astype(BF16)
    g = jnp.dot(hn, wg_ref[...], preferred_element_type=F32)
    u = jnp.dot(hn, wu_ref[...], preferred_element_type=F32)
    act = (_silu(g) * u).astype(BF16)
    h2 = h1 + jnp.dot(act, wd_ref[...], preferred_element_type=F32)
    hp = _rms(h2, npl_ref[...]).astype(BF16)
    gate = _sigmoid(jnp.dot(hp, wpg_ref[...], preferred_element_type=F32))
    up = jnp.dot(p_ref[...].astype(BF16), wpu_ref[...], preferred_element_type=F32)
    h3 = h2 + gate * up
    if final_norm:
        h3 = _rms(h3, nfin_ref[...])
    out_ref[...] = h3


def _post(h, ma, mb, p, wo, nf, wg, wu, wd, npl, wpg, wpu, nfin, final_norm, tm):
    t, d = h.shape
    row = lambda w: pl.BlockSpec((tm, w), lambda i: (i, 0))
    return pl.pallas_call(
        functools.partial(_post_kernel, final_norm),
        grid=(t // tm,),
        in_specs=[
            row(d), row(GROUP_WIDTH), row(GROUP_WIDTH), row(PLE_DIM),
            _const_spec(wo.shape), _const_spec((1, d)),
            _const_spec(wg.shape), _const_spec(wu.shape), _const_spec(wd.shape),
            _const_spec((1, d)), _const_spec(wpg.shape), _const_spec(wpu.shape),
            _const_spec((1, d)),
        ],
        out_specs=row(d),
        out_shape=jax.ShapeDtypeStruct((t, d), F32),
        compiler_params=pltpu.CompilerParams(
            dimension_semantics=("parallel",), vmem_limit_bytes=VMEM_LIMIT),
        name="post_mixer",
    )(h, ma, mb, p, wo, nf.reshape(1, d), wg, wu, wd, npl.reshape(1, d), wpg, wpu, nfin.reshape(1, d))


def _causal_conv(xbuf_ref, x, w_ref, ts):
    xbuf_ref[HALO:HALO + ts, :] = x
    acc = None
    for j in range(CONV_TAPS):
        off = HALO - (CONV_TAPS - 1) + j
        term = w_ref[j:j + 1, :] * xbuf_ref[off:off + ts, :]
        acc = term if acc is None else acc + term
    xbuf_ref[0:HALO, :] = xbuf_ref[ts:ts + HALO, :]
    return acc


def _proj_rglru_kernel(tm, tiles_per_seq, h_ref, nw_ref, wa_ref, wb_ref, cw_ref, cb_ref, wri_ref, bri_ref,
                       lam_ref, out_ref, zb_ref, xbuf_ref, hc_ref):
    w = GROUP_WIDTH
    first = (pl.program_id(0) % tiles_per_seq) == 0

    @pl.when(first)
    def _():
        xbuf_ref[0:HALO, :] = jnp.zeros((HALO, w), F32)
        hc_ref[...] = jnp.zeros_like(hc_ref)

    xn = _rms(h_ref[...], nw_ref[...]).astype(BF16)
    z_rg = jnp.dot(xn, wa_ref[...], preferred_element_type=F32)
    xa = z_rg[:, 0:w]
    ya = z_rg[:, w:2 * w]
    x = _causal_conv(xbuf_ref, xa, cw_ref, tm) + cb_ref[...]
    ri = jnp.dot(x.astype(BF16), wri_ref[...], preferred_element_type=F32) + bri_ref[...]
    zb_ref[...] = jnp.dot(xn, wb_ref[...], preferred_element_type=F32)
    sp_lam = _softplus(-lam_ref[...])
    lanes = HEAD128
    row = _iota2((RG_ROWS, lanes), 0)
    for l0 in range(0, w, lanes):
        ls = slice(l0, l0 + lanes)
        carry = hc_ref[:, ls]
        for r0 in range(0, tm, RG_ROWS):
            rs = slice(r0, r0 + RG_ROWS)
            r = _sigmoid(ri[rs, l0:l0 + lanes])
            i = _sigmoid(ri[rs, w + l0:w + l0 + lanes])
            log_a = -RG_C * r * sp_lam[:, ls]
            a = jnp.exp(log_a)
            mult = jnp.sqrt(-jnp.tanh(log_a) * (a * a + 1.0))
            if r0 == 0:
                mult = jnp.where((row == 0) & first, 1.0, mult)
            acc_a, acc_h = a, mult * (i * x[rs, ls])
            d = 1
            while d < RG_ROWS:
                if d % HALO == 0:
                    sh_a = jnp.concatenate([jnp.ones((d, lanes), F32), acc_a[0:RG_ROWS - d, :]], axis=0)
                    sh_h = jnp.concatenate([jnp.zeros((d, lanes), F32), acc_h[0:RG_ROWS - d, :]], axis=0)
                else:
                    sh_a = jnp.where(row >= d, pltpu.roll(acc_a, d, 0), 1.0)
                    sh_h = jnp.where(row >= d, pltpu.roll(acc_h, d, 0), 0.0)
                acc_h = acc_a * sh_h + acc_h
                acc_a = acc_a * sh_a
                d *= 2
            h = acc_h + acc_a * carry
            carry = h[RG_ROWS - 1:RG_ROWS, :]
            out_ref[rs, ls] = (h * _gelu_tanh(ya[rs, ls])).astype(out_ref.dtype)
        hc_ref[:, ls] = carry


def _proj_rglru(h, norm_w, wa, wb, cw, cb, wri, bri, lam, tm, seq_len):
    t, d = h.shape
    w = GROUP_WIDTH
    fb = wb.shape[1]
    return pl.pallas_call(
        functools.partial(_proj_rglru_kernel, tm, seq_len // tm),
        grid=(t // tm,),
        in_specs=[
            pl.BlockSpec((tm, d), lambda i: (i, 0)),
            _const_spec((1, d)), _const_spec(wa.shape), _const_spec(wb.shape),
            _const_spec((CONV_TAPS, w)), _const_spec((1, w)),
            _const_spec((w, 2 * w)), _const_spec((1, 2 * w)), _const_spec((1, w)),
        ],
        out_specs=[
            pl.BlockSpec((tm, w), lambda i: (i, 0)),
            pl.BlockSpec((tm, fb), lambda i: (i, 0)),
        ],
        out_shape=[jax.ShapeDtypeStruct((t, w), BF16), jax.ShapeDtypeStruct((t, fb), F32)],
        scratch_shapes=[pltpu.VMEM((HALO + tm, w), F32), pltpu.VMEM((1, w), F32)],
        compiler_params=pltpu.CompilerParams(
            dimension_semantics=("arbitrary",), vmem_limit_bytes=VMEM_LIMIT),
        name="proj_rglru",
    )(h, norm_w.reshape(1, d), wa, wb, cw, cb, wri, bri, lam)


def _gdn_kernel(ts, z_ref, cw_ref, alog_ref, dtb_ref, nw_ref, out_ref, xbuf_ref, st_ref):
    w = GROUP_WIDTH
    hd = HEAD128
    nh = w // hd
    cs = GDN_CHUNK
    nchunk = ts // cs
    tstep = pl.program_id(1)

    @pl.when(tstep == 0)
    def _():
        xbuf_ref[0:HALO, :] = jnp.zeros((HALO, 3 * w), F32)
        st_ref[...] = jnp.zeros_like(st_ref)

    qkv = _silu(_causal_conv(xbuf_ref, z_ref[0, :, 0:3 * w], cw_ref, ts))
    zg = z_ref[0, :, 3 * w:4 * w]
    ab = z_ref[0, :, 4 * w:4 * w + hd]
    g_n = -jnp.exp(alog_ref[...]) * _softplus(ab + dtb_ref[...])
    gc_n = _sel_mm(_block_tri(ts, cs, "incl"), g_n)
    src = _iota2((hd, w), 0)
    dst_head = _iota2((hd, w), 1) // hd
    gc_all = _mm_sel(gc_n, jnp.where(src == dst_head, 1.0, 0.0).astype(BF16), terms=3)
    beta = _mm_sel(_sigmoid(ab), jnp.where(src == dst_head + nh, 1.0, 0.0).astype(BF16), terms=3)
    gs_all = _chunk_last(gc_all, cs) - gc_all
    eg_all = jnp.exp(gc_all)
    scale = hd ** -0.5
    tt = _iota2((cs, cs), 0)
    ss = _iota2((cs, cs), 1)

    items = [(h, c) for c in range(nchunk) for h in range(nh)]
    qs, ks, vbs, kbegs, decs, kdts, glast = [], [], [], [], [], [], []
    for h, c in items:
        lo = h * hd
        rows = slice(c * cs, (c + 1) * cs)
        hs = slice(lo, lo + hd)
        qh = qkv[rows, lo:lo + hd]
        kh = qkv[rows, w + lo:w + lo + hd]
        vh = qkv[rows, 2 * w + lo:2 * w + lo + hd]
        qh = qh * lax.rsqrt(jnp.sum(qh * qh, axis=-1, keepdims=True) + L2_EPS) * scale
        kh = kh * lax.rsqrt(jnp.sum(kh * kh, axis=-1, keepdims=True) + L2_EPS)
        gc = gc_all[rows, hs]
        bt = beta[rows, hs]
        eg = eg_all[rows, hs]
        kb = kh * bt
        diff = gc - gc.T
        decs.append(jnp.exp(jnp.where(ss <= tt, diff, -jnp.inf)))
        qs.append(qh * eg)
        ks.append((kh, kb, qh))
        vbs.append(vh * bt)
        kbegs.append(kb * eg)
        kdts.append((kh * jnp.exp(gs_all[rows, hs])).T)
        glast.append(jnp.exp(gc_all[(c + 1) * cs - 1:(c + 1) * cs, hs]))
    a_mats = [jnp.where(ss < tt, _mm_nt(kb, kh) * dec, 0.0) for (kh, kb, _), dec in zip(ks, decs)]
    qks = [_mm_nt(qh, kh) * dec for (kh, _, qh), dec in zip(ks, decs)]
    tinvs = _nilpotent_inverses([-a for a in a_mats], cs, refine=True)
    sols = [_mm(t, jnp.concatenate([vb, kbeg], axis=1)) for t, vb, kbeg in zip(tinvs, vbs, kbegs)]
    us = [sol[:, 0:hd] for sol in sols]
    ws = [sol[:, hd:2 * hd] for sol in sols]

    states = [st_ref[h] for h in range(nh)]
    outs = [[] for _ in range(nh)]
    for c in range(nchunk):
        idx = [c * nh + h for h in range(nh)]
        proj = [_mm(jnp.concatenate([ws[i], qs[i]], axis=0), states[h]) for h, i in enumerate(idx)]
        v_new = [us[i] - pr[0:cs] for i, pr in zip(idx, proj)]
        o = [pr[cs:2 * cs] + _mm(qks[i], vn) for i, pr, vn in zip(idx, proj, v_new)]
        states = [glast[i] * states[h] + _mm(kdts[i], vn) for (h, i), vn in zip(enumerate(idx), v_new)]
        for h in range(nh):
            outs[h].append(o[h])
    for h in range(nh):
        lo = h * hd
        st_ref[h] = states[h]
        o = jnp.concatenate(outs[h], axis=0) if nchunk > 1 else outs[h][0]
        o = _rms(o, nw_ref[...]) * _silu(zg[:, lo:lo + hd])
        out_ref[0, :, lo:lo + hd] = o.astype(out_ref.dtype)


def _gdn(z, cw, alog, dtb, nw, ts):
    b, s, _ = z.shape
    w = GROUP_WIDTH
    nh = w // HEAD128
    return pl.pallas_call(
        functools.partial(_gdn_kernel, ts),
        grid=(b, s // ts),
        in_specs=[
            pl.BlockSpec((1, ts, 4 * w + HEAD128), lambda bi, ti: (bi, ti, 0)),
            _const_spec((CONV_TAPS, 3 * w)), _const_spec((1, HEAD128)), _const_spec((1, HEAD128)),
            _const_spec((1, HEAD128)),
        ],
        out_specs=pl.BlockSpec((1, ts, w), lambda bi, ti: (bi, ti, 0)),
        out_shape=jax.ShapeDtypeStruct((b, s, w), BF16),
        scratch_shapes=[pltpu.VMEM((HALO + ts, 3 * w), F32), pltpu.VMEM((nh, HEAD128, HEAD128), F32)],
        compiler_params=pltpu.CompilerParams(
            dimension_semantics=("parallel", "arbitrary"), vmem_limit_bytes=VMEM_LIMIT),
        name="gated_deltanet",
    )(z, cw, alog, dtb, nw)


def _hgrn2_kernel(ts, layer_j, z_ref, lbraw_ref, nw_ref, out_ref, st_ref, o_ref):
    w = GROUP_WIDTH
    hd = HEAD128
    nh = w // hd
    sub = HG_SUB
    nsub = ts // sub
    tstep = pl.program_id(1)

    @pl.when(tstep == 0)
    def _():
        st_ref[...] = jnp.zeros_like(st_ref)

    lbraw = lbraw_ref[...]
    e = jnp.exp(lbraw - jnp.max(lbraw, axis=0, keepdims=True))
    soft = e / jnp.sum(e, axis=0, keepdims=True)
    lb = jnp.sum(soft[0:layer_j + 1, :], axis=0, keepdims=True) - soft[0:1, :]
    lb = jnp.maximum(lb, 0.0)

    q = _silu(z_ref[0, :, 0:w])
    f = z_ref[0, :, w:2 * w]
    v = z_ref[0, :, 2 * w:3 * w]
    gate = z_ref[0, :, 3 * w:4 * w]
    kin = (1.0 - lb) * _sigmoid(-f)
    la = jnp.log(lb)
    lbm = jnp.log1p(-lb) - _softplus(-f)
    mx = jnp.maximum(la, lbm)
    log_f = mx + jnp.log1p(jnp.exp(-jnp.abs(la - lbm)))

    bl = _sel_mm(_block_tri(ts, sub, "incl"), log_f)
    btot = _chunk_last(bl, sub)
    qe = q * jnp.exp(bl)
    kh = kin * jnp.exp(btot - bl)

    ones_bd = _block_ones(w, hd)
    q3 = q.reshape(nsub, sub, w)
    k3 = kin.reshape(nsub, sub, w)
    v3 = v.reshape(nsub, sub, w)
    b3 = (bl * math.log2(math.e)).reshape(nsub, sub, w)
    half = sub // 2
    tpos = _iota2((nsub, half, w), 1)
    q_lo, q_hi = q3[:, 0:half, :], q3[:, half:sub, :]
    b_lo, b_hi = b3[:, 0:half, :], b3[:, half:sub, :]
    o_lo = jnp.zeros((nsub, half, w), F32)
    o_hi = jnp.zeros((nsub, half, w), F32)
    for s in range(sub):
        bs = b3[:, s:s + 1, :]
        ksrc = k3[:, s:s + 1, :]
        vsrc = v3[:, s:s + 1, :]
        if s < half:
            rel_lo = jnp.exp2(jnp.where(tpos >= s, b_lo - bs, -jnp.inf))
            prod = jnp.concatenate([q_lo * ksrc * rel_lo, q_hi * ksrc * jnp.exp2(b_hi - bs)], axis=1)
            sc = _mm_sel(prod.reshape(ts, w), ones_bd, terms=1).reshape(nsub, sub, w)
            o_lo = o_lo + sc[:, 0:half, :] * vsrc
            o_hi = o_hi + sc[:, half:sub, :] * vsrc
        else:
            rel_hi = jnp.exp2(jnp.where(tpos >= s - half, b_hi - bs, -jnp.inf))
            prod = q_hi * ksrc * rel_hi
            sc = _mm_sel(prod.reshape(nsub * half, w), ones_bd, terms=1).reshape(nsub, half, w)
            o_hi = o_hi + sc * vsrc
    o_intra = jnp.concatenate([o_lo, o_hi], axis=1).reshape(ts, w)

    dec_sub = jnp.exp(btot)
    blk = 128
    per_blk = blk // sub
    incs = [[] for _ in range(nh)]
    for h in range(nh):
        lo = h * hd
        for b0 in range(0, ts, blk):
            vt = v[b0:b0 + blk, lo:lo + hd].T.astype(BF16)
            for i in range(per_blk):
                r0 = b0 + i * sub
                parts = []
                if i > 0:
                    parts.append(jnp.zeros((i * sub, hd), BF16))
                parts.append(kh[r0:r0 + sub, lo:lo + hd].astype(BF16))
                if i + 1 < per_blk:
                    parts.append(jnp.zeros((blk - (i + 1) * sub, hd), BF16))
                kpad = jnp.concatenate(parts, axis=0) if len(parts) > 1 else parts[0]
                incs[h].append(jnp.dot(vt, kpad, preferred_element_type=F32))
    seen = [[] for _ in range(nh)]
    for h in range(nh):
        lo = h * hd
        st = st_ref[h]
        for i in range(nsub):
            seen[h].append(st.astype(BF16))
            st = st * dec_sub[i * sub:i * sub + 1, lo:lo + hd] + incs[h][i]
        st_ref[h] = st
    for h in range(nh):
        lo = h * hd
        o_inter = jnp.concatenate(
            [_mm_nt(qe[i * sub:(i + 1) * sub, lo:lo + hd], seen[h][i]) for i in range(nsub)], axis=0)
        o_ref[:, lo:lo + hd] = o_intra[:, lo:lo + hd] + o_inter

    out_ref[0] = (_rms(o_ref[...], nw_ref[...]) * _sigmoid(gate)).astype(out_ref.dtype)


def _hgrn2(z, lbraw, nw, layer_j, ts):
    b, s, _ = z.shape
    w = GROUP_WIDTH
    nh = w // HEAD128
    return pl.pallas_call(
        functools.partial(_hgrn2_kernel, ts, layer_j),
        grid=(b, s // ts),
        in_specs=[
            pl.BlockSpec((1, ts, 4 * w), lambda bi, ti: (bi, ti, 0)),
            _const_spec(lbraw.shape), _const_spec((1, w)),
        ],
        out_specs=pl.BlockSpec((1, ts, w), lambda bi, ti: (bi, ti, 0)),
        out_shape=jax.ShapeDtypeStruct((b, s, w), BF16),
        scratch_shapes=[pltpu.VMEM((nh, HEAD128, HEAD128), F32), pltpu.VMEM((ts, w), F32)],
        compiler_params=pltpu.CompilerParams(
            dimension_semantics=("parallel", "arbitrary"), vmem_limit_bytes=VMEM_LIMIT),
        name="hgrn2",
    )(z, lbraw, nw)


def _rwkv7_kernel(ts, z_ref, mu_ref, w0_ref, w2_ref, a0_ref, a2_ref, g2_ref, kk_ref, ka_ref, rk_ref,
                  lnw_ref, lnb_ref, out_ref, xbuf_ref, st_ref):
    w = GROUP_WIDTH
    lp = R7_LORA_PAD
    ch = R7_CHUNK
    npair = w // (2 * R7_N)
    tstep = pl.program_id(1)

    @pl.when(tstep == 0)
    def _():
        xbuf_ref[0:HALO, :] = jnp.zeros((HALO, R7_IN_PAD), F32)
        st_ref[...] = jnp.zeros_like(st_ref)

    pw = 2 * R7_N
    n = R7_N
    gch = min(R7_GROUP, ts // ch)
    grows = gch * ch
    lane = _iota2((ch, pw), 1)
    in_h0 = lane < n
    t_row = _iota2((ch, pw), 0)
    s_col = lane % n
    ones_bd = _block_ones(w, R7_N)
    tri_incl = _block_tri(grows, ch, "incl")
    items = [(c, p) for c in range(gch) for p in range(npair)]
    sl = lambda c, p: (slice(c * ch, (c + 1) * ch), slice(p * pw, (p + 1) * pw))

    def stack_heads(x):
        return jnp.concatenate([jnp.where(in_h0, x, 0.0), jnp.where(in_h0, 0.0, x)], axis=0)

    xbuf_ref[HALO:HALO + ts, :] = z_ref[0]
    states = [st_ref[p] for p in range(npair)]
    for g0 in range(0, ts, grows):
        z = xbuf_ref[HALO + g0:HALO + g0 + grows, :]
        z_prev = xbuf_ref[HALO - 1 + g0:HALO - 1 + g0 + grows, :]
        zd = z + mu_ref[...] * (z_prev - z)
        r = zd[:, 0:w]
        k = zd[:, w:2 * w]
        v = zd[:, 2 * w:3 * w]
        w_l = zd[:, 3 * w:3 * w + lp]
        a_l = zd[:, 3 * w + lp:3 * w + 2 * lp]
        g_l = zd[:, 3 * w + 2 * lp:3 * w + 3 * lp]
        log_w = -jnp.exp(-_softplus(-(w0_ref[...] + _mm(jnp.tanh(w_l), w2_ref[...]))) - 0.5)
        a = _sigmoid(a0_ref[...] + _mm(a_l, a2_ref[...]))
        gate = _mm(_sigmoid(g_l), g2_ref[...])
        kk = k * kk_ref[...]
        kk = kk * lax.rsqrt(_mm_sel(kk * kk, ones_bd) + L2_EPS)
        k = k * (1.0 + (a - 1.0) * ka_ref[...])
        cum = _sel_mm(tri_incl, log_w)
        total = _chunk_last(cum, ch)
        e_inv = jnp.exp(-cum)
        r_t = r * jnp.exp(cum)
        a_t = -kk * jnp.exp(cum - log_w)
        kka = kk * a
        b_h = kka * e_inv
        k_h = k * e_inv
        e_suf = jnp.exp(total - cum)
        b_dc = kka * e_suf
        k_dc = k * e_suf
        d_end = jnp.exp(total)

        v_ss = [stack_heads(v[sl(c, p)]) for c, p in items]
        grams = [
            _mm_nt(jnp.concatenate([a_t[sl(c, p)], r_t[sl(c, p)]], axis=0),
                   jnp.concatenate([stack_heads(b_h[sl(c, p)]), stack_heads(k_h[sl(c, p)])], axis=0))
            for c, p in items]
        a_abs = [jnp.where(s_col < t_row, g[0:ch, 0:pw], 0.0) for g in grams]
        a_rbs = [jnp.where(s_col <= t_row, g[ch:2 * ch, 0:pw], 0.0) for g in grams]
        a_k = [jnp.concatenate([jnp.where(s_col < t_row, g[0:ch, pw:2 * pw], 0.0),
                                jnp.where(s_col <= t_row, g[ch:2 * ch, pw:2 * pw], 0.0)], axis=0)
               for g in grams]
        tinvs = _nilpotent_inverses([stack_heads(x) for x in a_abs], ch, refine=False)
        kv = [_mm(x, vs) for x, vs in zip(a_k, v_ss)]
        w_ss = [_mm(t, jnp.concatenate([stack_heads(a_t[sl(c, p)]), stack_heads(x[0:ch])], axis=1))
                for t, x, (c, p) in zip(tinvs, kv, items)]
        ry = [_mm(x, w_s) for x, w_s in zip(a_rbs, w_ss)]
        r_eff = [r_t[sl(c, p)] + x[:, 0:pw] for x, (c, p) in zip(ry, items)]
        y0 = [x[:, pw:2 * pw] + y[ch:2 * ch] for x, y in zip(ry, kv)]
        bdc_ss = [stack_heads(b_dc[sl(c, p)]) for c, p in items]
        phi_t = [_mm(w_s[:, 0:pw].T, bd).astype(BF16) for w_s, bd in zip(w_ss, bdc_ss)]
        gamma_t = [_mm(jnp.concatenate([w_s[:, pw:2 * pw].T, vs.T], axis=1),
                       jnp.concatenate([bd, stack_heads(k_dc[sl(c, p)])], axis=0))
                   for w_s, vs, bd, (c, p) in zip(w_ss, v_ss, bdc_ss, items)]

        ys = [[] for _ in range(npair)]
        for c in range(gch):
            st_bs = [st.astype(BF16) for st in states]
            states = [states[p] * d_end[c * ch:c * ch + 1, p * pw:(p + 1) * pw]
                      + jnp.dot(st_bs[p], phi_t[c * npair + p], preferred_element_type=F32)
                      + gamma_t[c * npair + p] for p in range(npair)]
            for p in range(npair):
                i = c * npair + p
                ys[p].append(_mm_nt(r_eff[i], st_bs[p]) + y0[i])
        y = jnp.concatenate([jnp.concatenate(yp, axis=0) if gch > 1 else yp[0] for yp in ys], axis=1)

        mean = _mm_sel(y, ones_bd) * (1.0 / R7_N)
        yc = y - mean
        var = _mm_sel(yc * yc, ones_bd) * (1.0 / R7_N)
        yn = yc * lax.rsqrt(var + GN_EPS) * lnw_ref[...] + lnb_ref[...]
        bonus = _mm_sel(r * k * rk_ref[...], ones_bd) * v
        out_ref[0, g0:g0 + grows, :] = ((yn + bonus) * gate).astype(out_ref.dtype)
    for p in range(npair):
        st_ref[p] = states[p]
    xbuf_ref[0:HALO, :] = xbuf_ref[ts:ts + HALO, :]


def _rwkv7(z, mu, w0, w2, a0, a2, g2, k_k, k_a, r_k, ln_w, ln_b, ts):
    b, s, fin = z.shape
    w = GROUP_WIDTH
    npair = w // (2 * R7_N)
    vec = _const_spec((1, w))
    return pl.pallas_call(
        functools.partial(_rwkv7_kernel, ts),
        grid=(b, s // ts),
        in_specs=[
            pl.BlockSpec((1, ts, fin), lambda bi, ti: (bi, ti, 0)),
            _const_spec((1, fin)), vec, _const_spec(w2.shape), vec, _const_spec(a2.shape),
            _const_spec(g2.shape), vec, vec, vec, vec, vec,
        ],
        out_specs=pl.BlockSpec((1, ts, w), lambda bi, ti: (bi, ti, 0)),
        out_shape=jax.ShapeDtypeStruct((b, s, w), BF16),
        scratch_shapes=[pltpu.VMEM((HALO + ts, fin), F32), pltpu.VMEM((npair, 2 * R7_N, 2 * R7_N), F32)],
        compiler_params=pltpu.CompilerParams(
            dimension_semantics=("parallel", "arbitrary"), vmem_limit_bytes=VMEM_LIMIT),
        name="rwkv7",
    )(z, mu, w0, w2, a0, a2, g2, k_k, k_a, r_k, ln_w, ln_b)


def _even_weights(w_in, rg_w_r, rg_w_i, rg_b_r, rg_b_i, gdn_a_log, gdn_dt_bias):
    w = GROUP_WIDTH
    nh = w // HEAD128
    w_rg = w_in[:, 0:2 * w]
    w_gdn = jnp.pad(w_in[:, 2 * w:6 * w + 2 * nh], ((0, 0), (0, HEAD128 - 2 * nh)))
    wri = jnp.concatenate([jax.scipy.linalg.block_diag(*rg_w_r), jax.scipy.linalg.block_diag(*rg_w_i)], axis=1)
    bri = jnp.concatenate([rg_b_r, rg_b_i]).reshape(1, 2 * w)
    alog = jnp.pad(gdn_a_log, (0, HEAD128 - nh)).reshape(1, HEAD128)
    dtb = jnp.pad(gdn_dt_bias, (0, HEAD128 - nh)).reshape(1, HEAD128)
    return w_rg.astype(BF16), w_gdn.astype(BF16), wri.astype(BF16), bri, alog, dtb


def _pad_groups(x, axis):
    w = GROUP_WIDTH
    sizes = (w, w, w, 64, 64, 128)
    parts = []
    start = 0
    for sz in sizes:
        piece = lax.slice_in_dim(x, start, start + sz, axis=axis)
        pad = (-sz) % R7_LORA_PAD
        if pad:
            cfg = [(0, 0)] * x.ndim
            cfg[axis] = (0, pad)
            piece = jnp.pad(piece, cfg)
        parts.append(piece)
        start += sz
    return jnp.concatenate(parts, axis=axis)


def _pad_rows(x, rows):
    return jnp.pad(x, ((0, rows - x.shape[0]), (0, 0)))


def _tiles(b, s):
    t = b * s
    tm_proj = 512 if s % 512 == 0 else 256
    tm_post = 256
    ts = 256 if s % 256 == 0 else 128
    return tm_proj, tm_post, ts


def kernel(x, p, norm_mix, norm_ffn, norm_ple, norm_final, e_w_in, rg_conv_w, rg_conv_b, rg_w_r, rg_b_r, rg_w_i, rg_b_i, rg_lambda, gdn_conv_w, gdn_a_log, gdn_dt_bias, gdn_norm_w, e_w_out, o_w_in, hg_lower_bounds, hg_norm_w, r7_mu, r7_w0, r7_w2, r7_a0, r7_a2, r7_g2, r7_k_k, r7_k_a, r7_r_k, r7_ln_w, r7_ln_b, o_w_out, ffn_w_gate, ffn_w_up, ffn_w_down, ple_w_up, ple_w_gate):
    b, s, d = x.shape
    depth = p.shape[0]
    t = b * s
    w = GROUP_WIDTH
    tm_proj, tm_post, ts = _tiles(b, s)
    h = x.reshape(t, d)
    row = lambda vec: vec.reshape(1, -1)
    for layer in range(depth):
        j = layer // 2
        if layer % 2 == 0:
            w_rg, w_gdn, wri, bri, alog, dtb = _even_weights(
                e_w_in[j], rg_w_r[j], rg_w_i[j], rg_b_r[j], rg_b_i[j], gdn_a_log[j], gdn_dt_bias[j])
            out_a, z_b = _proj_rglru(h, norm_mix[layer], w_rg, w_gdn, rg_conv_w[j], row(rg_conv_b[j]),
                                     wri, bri, row(rg_lambda[j]), tm_proj, s)
            out_b = _gdn(z_b.reshape(b, s, -1), gdn_conv_w[j], alog, dtb, row(gdn_norm_w[j]),
                         2 * ts if s % (2 * ts) == 0 else ts)
            w_out = e_w_out[j]
        else:
            hg_in = 4 * w
            w_hg = o_w_in[j][:, 0:hg_in].astype(BF16)
            w_r7 = _pad_groups(o_w_in[j][:, hg_in:], axis=1).astype(BF16)
            z_a, z_b = _norm_proj(h, norm_mix[layer], w_hg, w_r7, tm_proj)
            out_a = _hgrn2(z_a.reshape(b, s, -1), hg_lower_bounds, row(hg_norm_w[j]), j, ts)
            out_b = _rwkv7(z_b.reshape(b, s, -1), row(_pad_groups(r7_mu[j], axis=0)), row(r7_w0[j]),
                           _pad_rows(r7_w2[j], R7_LORA_PAD).astype(BF16), row(r7_a0[j]),
                           _pad_rows(r7_a2[j], R7_LORA_PAD).astype(BF16), r7_g2[j].astype(BF16),
                           row(r7_k_k[j]), row(r7_k_a[j]), row(r7_r_k[j].reshape(-1)),
                           row(r7_ln_w[j]), row(r7_ln_b[j]), ts)
            w_out = o_w_out[j]
        h = _post(h, out_a.reshape(t, w), out_b.reshape(t, w), p[layer].reshape(t, PLE_DIM),
                  w_out.astype(BF16), norm_ffn[layer], ffn_w_gate[layer].astype(BF16),
                  ffn_w_up[layer].astype(BF16), ffn_w_down[layer].astype(BF16), norm_ple[layer],
                  ple_w_gate[layer].astype(BF16), ple_w_up[layer].astype(BF16), norm_final,
                  layer == depth - 1, tm_post)
    return h.reshape(b, s, d)
```

```python
import functools
import math

import jax
import jax.numpy as jnp
from jax import lax
from jax.experimental import pallas as pl
from jax.experimental.pallas import tpu as pltpu

F32 = jnp.float32
BF16 = jnp.bfloat16

D_MODEL = 1024
PLE_DIM = 256
D_FF = 2816
GROUP_WIDTH = 512
RMS_EPS = 1e-6
L2_EPS = 1e-6
GN_EPS = 64e-5
RG_C = 8.0
RG_BLOCKS = 8
RG_ROWS = 64
CONV_TAPS = 4
HEAD128 = 128
R7_N = 64
R7_LORA_PAD = 128
R7_IN_PAD = 3 * GROUP_WIDTH + 3 * R7_LORA_PAD

GDN_CHUNK = 128
HG_CHUNK = 128
R7_CHUNK = 64
R7_GROUP = 4
HALO = 8

VMEM_LIMIT = 56 * 1024 * 1024


def _sigmoid(x):
    return 1.0 / (1.0 + jnp.exp(-x))


def _softplus(x):
    return jnp.maximum(x, 0.0) + jnp.log1p(jnp.exp(-jnp.abs(x)))


def _silu(x):
    return x * _sigmoid(x)


def _gelu_tanh(x):
    c = math.sqrt(2.0 / math.pi)
    return 0.5 * x * (1.0 + jnp.tanh(c * (x + 0.044715 * (x * x * x))))


def _rms(x, w):
    ms = jnp.mean(x * x, axis=-1, keepdims=True)
    return x * lax.rsqrt(ms + RMS_EPS) * w


def _mm(a, b):
    return jnp.dot(a.astype(BF16), b.astype(BF16), preferred_element_type=F32)


def _mm_nt(a, b):
    return lax.dot_general(a.astype(BF16), b.astype(BF16), (((1,), (1,)), ((), ())),
                           preferred_element_type=F32)


def _split(x, terms):
    parts = []
    r = x
    for _ in range(terms):
        hi = r.astype(BF16)
        parts.append(hi)
        r = r - hi.astype(F32)
    return parts


def _sel_mm(sel, x, terms=3):
    acc = None
    for part in _split(x, terms):
        d = jnp.dot(sel, part, preferred_element_type=F32)
        acc = d if acc is None else acc + d
    return acc


def _mm_sel(x, sel, terms=2):
    acc = None
    for part in _split(x, terms):
        d = jnp.dot(part, sel, preferred_element_type=F32)
        acc = d if acc is None else acc + d
    return acc


def _mm_hi(a, b):
    a_hi, a_lo = _split(a, 2)
    b_hi, b_lo = _split(b, 2)
    return (jnp.dot(a_hi, b_hi, preferred_element_type=F32)
            + jnp.dot(a_lo, b_hi, preferred_element_type=F32)
            + jnp.dot(a_hi, b_lo, preferred_element_type=F32))


def _iota2(shape, axis):
    return lax.broadcasted_iota(jnp.int32, shape, axis)


def _block_tri(n, block, kind):
    t = _iota2((n, n), 0)
    s = _iota2((n, n), 1)
    same = (t // block) == (s // block)
    cond = (s <= t) if kind == "incl" else (s > t)
    return jnp.where(same & cond, 1.0, 0.0).astype(BF16)


def _chunk_last(x, chunk):
    cols = x.shape[1]
    return jnp.concatenate(
        [jnp.broadcast_to(x[r + chunk - 1:r + chunk, :], (chunk, cols)) for r in range(0, x.shape[0], chunk)],
        axis=0)


def _block_ones(n, block):
    t = _iota2((n, n), 0)
    s = _iota2((n, n), 1)
    return jnp.where((t // block) == (s // block), 1.0, 0.0).astype(BF16)


def _nilpotent_inverses(xs, n, refine):
    shape = xs[0].shape
    eye = jnp.where(_iota2(shape, 0) == _iota2(shape, 1), 1.0, 0.0).astype(F32)
    ts = [eye + x for x in xs]
    ps = list(xs)
    k = 2
    while k < n:
        ps = [_mm(p, p) for p in ps]
        ts = [t + _mm(t, p) for t, p in zip(ts, ps)]
        k *= 2
    if refine:
        res = [(eye - t) + _mm_hi(x, t) for x, t in zip(xs, ts)]
        ts = [t + _mm(t, r) for t, r in zip(ts, res)]
    return ts


def _norm_proj_kernel(h_ref, nw_ref, wa_ref, wb_ref, za_ref, zb_ref):
    xn = _rms(h_ref[...], nw_ref[...]).astype(BF16)
    za_ref[...] = jnp.dot(xn, wa_ref[...], preferred_element_type=F32)
    zb_ref[...] = jnp.dot(xn, wb_ref[...], preferred_element_type=F32)


def _const_spec(shape):
    nd = len(shape)
    return pl.BlockSpec(shape, lambda *_: (0,) * nd, pipeline_mode=pl.Buffered(1))


def _norm_proj(h, norm_w, wa, wb, tm):
    t, d = h.shape
    fa, fb = wa.shape[1], wb.shape[1]
    return pl.pallas_call(
        _norm_proj_kernel,
        grid=(t // tm,),
        in_specs=[
            pl.BlockSpec((tm, d), lambda i: (i, 0)),
            _const_spec((1, d)),
            _const_spec((d, fa)),
            _const_spec((d, fb)),
        ],
        out_specs=[
            pl.BlockSpec((tm, fa), lambda i: (i, 0)),
            pl.BlockSpec((tm, fb), lambda i: (i, 0)),
        ],
        out_shape=[jax.ShapeDtypeStruct((t, fa), F32), jax.ShapeDtypeStruct((t, fb), F32)],
        compiler_params=pltpu.CompilerParams(
            dimension_semantics=("parallel",), vmem_limit_bytes=VMEM_LIMIT),
        name="norm_proj",
    )(h, norm_w.reshape(1, d), wa, wb)


def _post_kernel(final_norm, h_ref, ma_ref, mb_ref, p_ref, wo_ref, nf_ref, wg_ref, wu_ref, wd_ref,
                 npl_ref, wpg_ref, wpu_ref, nfin_ref, out_ref):
    h = h_ref[...]
    mix = (jnp.dot(ma_ref[...], wo_ref[0:GROUP_WIDTH, :], preferred_element_type=F32)
           + jnp.dot(mb_ref[...], wo_ref[GROUP_WIDTH:2 * GROUP_WIDTH, :], preferred_element_type=F32))
    h1 = h + mix
    hn = _rms(h1, nf_ref[...]).astype(BF16)
    g = jnp.dot(hn, wg_ref[...], preferred_element_type=F32)
    u = jnp.dot(hn, wu_ref[...], preferred_element_type=F32)
    act = (_silu(g) * u).astype(BF16)
    h2 = h1 + jnp.dot(act, wd_ref[...], preferred_element_type=F32)
    hp = _rms(h2, npl_ref[...]).astype(BF16)
    gate = _sigmoid(jnp.dot(hp, wpg_ref[...], preferred_element_type=F32))
    up = jnp.dot(p_ref[...].astype(BF16), wpu_ref[...], preferred_element_type=F32)
    h3 = h2 + gate * up
    if final_norm:
        h3 = _rms(h3, nfin_ref[...])
    out_ref[...] = h3


def _post(h, ma, mb, p_all, layer, wo, nf, wg, wu, wd, npl, wpg, wpu, nfin, final_norm, tm):
    t, d = h.shape
    row = lambda w: pl.BlockSpec((tm, w), lambda i: (i, 0))
    return pl.pallas_call(
        functools.partial(_post_kernel, final_norm),
        grid=(t // tm,),
        in_specs=[
            row(d), row(GROUP_WIDTH), row(GROUP_WIDTH),
            pl.BlockSpec((None, tm, PLE_DIM), lambda i: (layer, i, 0)),
            _const_spec(wo.shape), _const_spec((1, d)),
            _const_spec(wg.shape), _const_spec(wu.shape), _const_spec(wd.shape),
            _const_spec((1, d)), _const_spec(wpg.shape), _const_spec(wpu.shape),
            _const_spec((1, d)),
        ],
        out_specs=row(d),
        out_shape=jax.ShapeDtypeStruct((t, d), F32),
        compiler_params=pltpu.CompilerParams(
            dimension_semantics=("parallel",), vmem_limit_bytes=VMEM_LIMIT),
        name="post_mixer",
    )(h, ma, mb, p_all, wo, nf.reshape(1, d), wg, wu, wd, npl.reshape(1, d), wpg, wpu, nfin.reshape(1, d))


def _causal_conv(xbuf_ref, x, w_ref, ts):
    xbuf_ref[HALO:HALO + ts, :] = x
    acc = None
    for j in range(CONV_TAPS):
        off = HALO - (CONV_TAPS - 1) + j
        term = w_ref[j:j + 1, :] * xbuf_ref[off:off + ts, :]
        acc = term if acc is None else acc + term
    xbuf_ref[0:HALO, :] = xbuf_ref[ts:ts + HALO, :]
    return acc


def _proj_rglru_kernel(tm, tiles_per_seq, h_ref, nw_ref, wa_ref, wb_ref, cw_ref, cb_ref, wri_ref, bri_ref,
                       lam_ref, out_ref, zb_ref, xbuf_ref, hc_ref):
    w = GROUP_WIDTH
    first = (pl.program_id(0) % tiles_per_seq) == 0

    @pl.when(first)
    def _():
        xbuf_ref[0:HALO, :] = jnp.zeros((HALO, w), F32)
        hc_ref[...] = jnp.zeros_like(hc_ref)

    xn = _rms(h_ref[...], nw_ref[...]).astype(BF16)
    z_rg = jnp.dot(xn, wa_ref[...], preferred_element_type=F32)
    xa = z_rg[:, 0:w]
    ya = z_rg[:, w:2 * w]
    x = _causal_conv(xbuf_ref, xa, cw_ref, tm) + cb_ref[...]
    ri = jnp.dot(x.astype(BF16), wri_ref[...], preferred_element_type=F32) + bri_ref[...]
    zb_ref[...] = jnp.dot(xn, wb_ref[...], preferred_element_type=F32)
    sp_lam = _softplus(-lam_ref[...])
    lanes = HEAD128
    row = _iota2((RG_ROWS, lanes), 0)
    for l0 in range(0, w, lanes):
        ls = slice(l0, l0 + lanes)
        carry = hc_ref[:, ls]
        for r0 in range(0, tm, RG_ROWS):
            rs = slice(r0, r0 + RG_ROWS)
            r = _sigmoid(ri[rs, l0:l0 + lanes])
            i = _sigmoid(ri[rs, w + l0:w + l0 + lanes])
            log_a = -RG_C * r * sp_lam[:, ls]
            a = jnp.exp(log_a)
            mult = jnp.sqrt(-jnp.tanh(log_a) * (a * a + 1.0))
            if r0 == 0:
                mult = jnp.where((row == 0) & first, 1.0, mult)
            acc_a, acc_h = a, mult * (i * x[rs, ls])
            d = 1
            while d < RG_ROWS:
                if d % HALO == 0:
                    sh_a = jnp.concatenate([jnp.ones((d, lanes), F32), acc_a[0:RG_ROWS - d, :]], axis=0)
                    sh_h = jnp.concatenate([jnp.zeros((d, lanes), F32), acc_h[0:RG_ROWS - d, :]], axis=0)
                else:
                    sh_a = jnp.where(row >= d, pltpu.roll(acc_a, d, 0), 1.0)
                    sh_h = jnp.where(row >= d, pltpu.roll(acc_h, d, 0), 0.0)
                acc_h = acc_a * sh_h + acc_h
                acc_a = acc_a * sh_a
                d *= 2
            h = acc_h + acc_a * carry
            carry = h[RG_ROWS - 1:RG_ROWS, :]
            out_ref[rs, ls] = (h * _gelu_tanh(ya[rs, ls])).astype(out_ref.dtype)
        hc_ref[:, ls] = carry


def _proj_rglru(h, norm_w, wa, wb, cw, cb, wri, bri, lam, tm, seq_len):
    t, d = h.shape
    w = GROUP_WIDTH
    fb = wb.shape[1]
    return pl.pallas_call(
        functools.partial(_proj_rglru_kernel, tm, seq_len // tm),
        grid=(t // tm,),
        in_specs=[
            pl.BlockSpec((tm, d), lambda i: (i, 0)),
            _const_spec((1, d)), _const_spec(wa.shape), _const_spec(wb.shape),
            _const_spec((CONV_TAPS, w)), _const_spec((1, w)),
            _const_spec((w, 2 * w)), _const_spec((1, 2 * w)), _const_spec((1, w)),
        ],
        out_specs=[
            pl.BlockSpec((tm, w), lambda i: (i, 0)),
            pl.BlockSpec((tm, fb), lambda i: (i, 0)),
        ],
        out_shape=[jax.ShapeDtypeStruct((t, w), BF16), jax.ShapeDtypeStruct((t, fb), F32)],
        scratch_shapes=[pltpu.VMEM((HALO + tm, w), F32), pltpu.VMEM((1, w), F32)],
        compiler_params=pltpu.CompilerParams(
            dimension_semantics=("arbitrary",), vmem_limit_bytes=VMEM_LIMIT),
        name="proj_rglru",
    )(h, norm_w.reshape(1, d), wa, wb, cw, cb, wri, bri, lam)


def _gdn_kernel(ts, z_ref, cw_ref, alog_ref, dtb_ref, nw_ref, out_ref, xbuf_ref, st_ref):
    w = GROUP_WIDTH
    hd = HEAD128
    nh = w // hd
    cs = GDN_CHUNK
    nchunk = ts // cs
    tstep = pl.program_id(1)

    @pl.when(tstep == 0)
    def _():
        xbuf_ref[0:HALO, :] = jnp.zeros((HALO, 3 * w), F32)
        st_ref[...] = jnp.zeros_like(st_ref)

    qkv = _silu(_causal_conv(xbuf_ref, z_ref[0, :, 0:3 * w], cw_ref, ts))
    zg = z_ref[0, :, 3 * w:4 * w]
    ab = z_ref[0, :, 4 * w:4 * w + hd]
    g_n = -jnp.exp(alog_ref[...]) * _softplus(ab + dtb_ref[...])
    gc_n = _sel_mm(_block_tri(ts, cs, "incl"), g_n)
    src = _iota2((hd, w), 0)
    dst_head = _iota2((hd, w), 1) // hd
    gc_all = _mm_sel(gc_n, jnp.where(src == dst_head, 1.0, 0.0).astype(BF16), terms=3)
    beta = _mm_sel(_sigmoid(ab), jnp.where(src == dst_head + nh, 1.0, 0.0).astype(BF16), terms=3)
    gs_all = _chunk_last(gc_all, cs) - gc_all
    eg_all = jnp.exp(gc_all)
    scale = hd ** -0.5
    tt = _iota2((cs, cs), 0)
    ss = _iota2((cs, cs), 1)

    items = [(h, c) for c in range(nchunk) for h in range(nh)]
    qs, ks, vbs, kbegs, decs, kdts, glast = [], [], [], [], [], [], []
    for h, c in items:
        lo = h * hd
        rows = slice(c * cs, (c + 1) * cs)
        hs = slice(lo, lo + hd)
        qh = qkv[rows, lo:lo + hd]
        kh = qkv[rows, w + lo:w + lo + hd]
        vh = qkv[rows, 2 * w + lo:2 * w + lo + hd]
        qh = qh * lax.rsqrt(jnp.sum(qh * qh, axis=-1, keepdims=True) + L2_EPS) * scale
        kh = kh * lax.rsqrt(jnp.sum(kh * kh, axis=-1, keepdims=True) + L2_EPS)
        gc = gc_all[rows, hs]
        bt = beta[rows, hs]
        eg = eg_all[rows, hs]
        kb = kh * bt
        diff = gc - gc.T
        decs.append(jnp.exp(jnp.where(ss <= tt, diff, -jnp.inf)))
        qs.append(qh * eg)
        ks.append((kh, kb, qh))
        vbs.append(vh * bt)
        kbegs.append(kb * eg)
        kdts.append((kh * jnp.exp(gs_all[rows, hs])).T)
        glast.append(jnp.exp(gc_all[(c + 1) * cs - 1:(c + 1) * cs, hs]))
    a_mats = [jnp.where(ss < tt, _mm_nt(kb, kh) * dec, 0.0) for (kh, kb, _), dec in zip(ks, decs)]
    qks = [_mm_nt(qh, kh) * dec for (kh, _, qh), dec in zip(ks, decs)]
    tinvs = _nilpotent_inverses([-a for a in a_mats], cs, refine=True)
    sols = [_mm(t, jnp.concatenate([vb, kbeg], axis=1)) for t, vb, kbeg in zip(tinvs, vbs, kbegs)]
    us = [sol[:, 0:hd] for sol in sols]
    ws = [sol[:, hd:2 * hd] for sol in sols]

    states = [st_ref[h] for h in range(nh)]
    outs = [[] for _ in range(nh)]
    for c in range(nchunk):
        idx = [c * nh + h for h in range(nh)]
        proj = [_mm(jnp.concatenate([ws[i], qs[i]], axis=0), states[h]) for h, i in enumerate(idx)]
        v_new = [us[i] - pr[0:cs] for i, pr in zip(idx, proj)]
        o = [pr[cs:2 * cs] + _mm(qks[i], vn) for i, pr, vn in zip(idx, proj, v_new)]
        states = [glast[i] * states[h] + _mm(kdts[i], vn) for (h, i), vn in zip(enumerate(idx), v_new)]
        for h in range(nh):
            outs[h].append(o[h])
    for h in range(nh):
        lo = h * hd
        st_ref[h] = states[h]
        o = jnp.concatenate(outs[h], axis=0) if nchunk > 1 else outs[h][0]
        o = _rms(o, nw_ref[...]) * _silu(zg[:, lo:lo + hd])
        out_ref[0, :, lo:lo + hd] = o.astype(out_ref.dtype)


def _gdn(z, cw, alog, dtb, nw, ts):
    b, s, _ = z.shape
    w = GROUP_WIDTH
    nh = w // HEAD128
    return pl.pallas_call(
        functools.partial(_gdn_kernel, ts),
        grid=(b, s // ts),
        in_specs=[
            pl.BlockSpec((1, ts, 4 * w + HEAD128), lambda bi, ti: (bi, ti, 0)),
            _const_spec((CONV_TAPS, 3 * w)), _const_spec((1, HEAD128)), _const_spec((1, HEAD128)),
            _const_spec((1, HEAD128)),
        ],
        out_specs=pl.BlockSpec((1, ts, w), lambda bi, ti: (bi, ti, 0)),
        out_shape=jax.ShapeDtypeStruct((b, s, w), BF16),
        scratch_shapes=[pltpu.VMEM((HALO + ts, 3 * w), F32), pltpu.VMEM((nh, HEAD128, HEAD128), F32)],
        compiler_params=pltpu.CompilerParams(
            dimension_semantics=("parallel", "arbitrary"), vmem_limit_bytes=VMEM_LIMIT),
        name="gated_deltanet",
    )(z, cw, alog, dtb, nw)


def _hgrn2_kernel(ts, layer_j, z_ref, lbraw_ref, nw_ref, out_ref, st_ref, o_ref):
    w = GROUP_WIDTH
    hd = HEAD128
    nh = w // hd
    tstep = pl.program_id(1)

    @pl.when(tstep == 0)
    def _():
        st_ref[...] = jnp.zeros_like(st_ref)

    lbraw = lbraw_ref[...]
    e = jnp.exp(lbraw - jnp.max(lbraw, axis=0, keepdims=True))
    soft = e / jnp.sum(e, axis=0, keepdims=True)
    lb = jnp.sum(soft[0:layer_j + 1, :], axis=0, keepdims=True) - soft[0:1, :]
    lb = jnp.maximum(lb, 0.0)

    q = _silu(z_ref[0, :, 0:w])
    f = z_ref[0, :, w:2 * w]
    v = z_ref[0, :, 2 * w:3 * w]
    gate = z_ref[0, :, 3 * w:4 * w]
    kin = (1.0 - lb) * _sigmoid(-f)
    la = jnp.log(lb)
    lbm = jnp.log1p(-lb) - _softplus(-f)
    mx = jnp.maximum(la, lbm)
    log_f = mx + jnp.log1p(jnp.exp(-jnp.abs(la - lbm)))

    cs = HG_CHUNK
    nchunk = ts // cs
    b = _sel_mm(_block_tri(ts, cs, "incl"), log_f)
    b2 = b * math.log2(math.e)
    row = _iota2((ts, w), 0)
    tt = _iota2((cs, cs), 0)
    ss = _iota2((cs, cs), 1)
    items = [(c, h) for c in range(nchunk) for h in range(nh)]
    sl = lambda c, h: (slice(c * cs, (c + 1) * cs), slice(h * hd, (h + 1) * hd))

    scores = [jnp.where(tt == ss, _mm_nt(q[sl(c, h)], kin[sl(c, h)]), 0.0) for c, h in items]
    block_end = b2
    m = 1
    while m < cs:
        prev_end = pltpu.roll(block_end, m, 0)
        q_m = q * jnp.exp2(b2 - prev_end)
        k_m = kin * jnp.exp2(block_end - b2)
        pair = ((tt // m) % 2 == 1) & ((ss // m) == (tt // m) - 1)
        scores = [jnp.where(pair, _mm_nt(q_m[sl(c, h)], k_m[sl(c, h)]), sc)
                  for sc, (c, h) in zip(scores, items)]
        block_end = jnp.where((row // m) % 2 == 0, pltpu.roll(block_end, ts - m, 0), block_end)
        m *= 2
    o_intra = [_mm(sc, v[sl(c, h)]) for sc, (c, h) in zip(scores, items)]

    btot = _chunk_last(b, cs)
    qe = q * jnp.exp(b)
    k_dec = kin * jnp.exp(btot - b)
    dec = jnp.exp(btot)
    incs = [_mm(v[sl(c, h)].T, k_dec[sl(c, h)]) for c, h in items]
    states = [st_ref[h] for h in range(nh)]
    for c in range(nchunk):
        for h in range(nh):
            i = c * nh + h
            o_ref[c * cs:(c + 1) * cs, h * hd:(h + 1) * hd] = o_intra[i] + _mm_nt(qe[sl(c, h)], states[h])
        states = [states[h] * dec[c * cs:c * cs + 1, h * hd:(h + 1) * hd] + incs[c * nh + h]
                  for h in range(nh)]
    for h in range(nh):
        st_ref[h] = states[h]

    out_ref[0] = (_rms(o_ref[...], nw_ref[...]) * _sigmoid(gate)).astype(out_ref.dtype)


def _hgrn2(z, lbraw, nw, layer_j, ts):
    b, s, _ = z.shape
    w = GROUP_WIDTH
    nh = w // HEAD128
    return pl.pallas_call(
        functools.partial(_hgrn2_kernel, ts, layer_j),
        grid=(b, s // ts),
        in_specs=[
            pl.BlockSpec((1, ts, 4 * w), lambda bi, ti: (bi, ti, 0)),
            _const_spec(lbraw.shape), _const_spec((1, w)),
        ],
        out_specs=pl.BlockSpec((1, ts, w), lambda bi, ti: (bi, ti, 0)),
        out_shape=jax.ShapeDtypeStruct((b, s, w), BF16),
        scratch_shapes=[pltpu.VMEM((nh, HEAD128, HEAD128), F32), pltpu.VMEM((ts, w), F32)],
        compiler_params=pltpu.CompilerParams(
            dimension_semantics=("parallel", "arbitrary"), vmem_limit_bytes=VMEM_LIMIT),
        name="hgrn2",
    )(z, lbraw, nw)


def _rwkv7_kernel(ts, z_ref, mu_ref, w0_ref, w2_ref, a0_ref, a2_ref, g2_ref, kk_ref, ka_ref, rk_ref,
                  lnw_ref, lnb_ref, out_ref, xbuf_ref, st_ref):
    w = GROUP_WIDTH
    lp = R7_LORA_PAD
    ch = R7_CHUNK
    npair = w // (2 * R7_N)
    tstep = pl.program_id(1)

    @pl.when(tstep == 0)
    def _():
        xbuf_ref[0:HALO, :] = jnp.zeros((HALO, R7_IN_PAD), F32)
        st_ref[...] = jnp.zeros_like(st_ref)

    pw = 2 * R7_N
    n = R7_N
    gch = min(R7_GROUP, ts // ch)
    grows = gch * ch
    lane = _iota2((ch, pw), 1)
    in_h0 = lane < n
    t_row = _iota2((ch, pw), 0)
    s_col = lane % n
    ones_bd = _block_ones(w, R7_N)
    tri_incl = _block_tri(grows, ch, "incl")
    items = [(c, p) for c in range(gch) for p in range(npair)]
    sl = lambda c, p: (slice(c * ch, (c + 1) * ch), slice(p * pw, (p + 1) * pw))

    def stack_heads(x):
        return jnp.concatenate([jnp.where(in_h0, x, 0.0), jnp.where(in_h0, 0.0, x)], axis=0)

    xbuf_ref[HALO:HALO + ts, :] = z_ref[0]
    states = [st_ref[p] for p in range(npair)]
    for g0 in range(0, ts, grows):
        z = xbuf_ref[HALO + g0:HALO + g0 + grows, :]
        z_prev = xbuf_ref[HALO - 1 + g0:HALO - 1 + g0 + grows, :]
        zd = z + mu_ref[...] * (z_prev - z)
        r = zd[:, 0:w]
        k = zd[:, w:2 * w]
        v = zd[:, 2 * w:3 * w]
        w_l = zd[:, 3 * w:3 * w + lp]
        a_l = zd[:, 3 * w + lp:3 * w + 2 * lp]
        g_l = zd[:, 3 * w + 2 * lp:3 * w + 3 * lp]
        log_w = -jnp.exp(-_softplus(-(w0_ref[...] + _mm(jnp.tanh(w_l), w2_ref[...]))) - 0.5)
        a = _sigmoid(a0_ref[...] + _mm(a_l, a2_ref[...]))
        gate = _mm(_sigmoid(g_l), g2_ref[...])
        kk = k * kk_ref[...]
        kk = kk * lax.rsqrt(_mm_sel(kk * kk, ones_bd) + L2_EPS)
        k = k * (1.0 + (a - 1.0) * ka_ref[...])
        cum = _sel_mm(tri_incl, log_w)
        total = _chunk_last(cum, ch)
        e_inv = jnp.exp(-cum)
        r_t = r * jnp.exp(cum)
        a_t = -kk * jnp.exp(cum - log_w)
        kka = kk * a
        b_h = kka * e_inv
        k_h = k * e_inv
        e_suf = jnp.exp(total - cum)
        b_dc = kka * e_suf
        k_dc = k * e_suf
        d_end = jnp.exp(total)

        v_ss = [stack_heads(v[sl(c, p)]) for c, p in items]
        grams = [
            _mm_nt(jnp.concatenate([a_t[sl(c, p)], r_t[sl(c, p)]], axis=0),
                   jnp.concatenate([stack_heads(b_h[sl(c, p)]), stack_heads(k_h[sl(c, p)])], axis=0))
            for c, p in items]
        a_abs = [jnp.where(s_col < t_row, g[0:ch, 0:pw], 0.0) for g in grams]
        a_rbs = [jnp.where(s_col <= t_row, g[ch:2 * ch, 0:pw], 0.0) for g in grams]
        a_k = [jnp.concatenate([jnp.where(s_col < t_row, g[0:ch, pw:2 * pw], 0.0),
                                jnp.where(s_col <= t_row, g[ch:2 * ch, pw:2 * pw], 0.0)], axis=0)
               for g in grams]
        tinvs = _nilpotent_inverses([stack_heads(x) for x in a_abs], ch, refine=False)
        kv = [_mm(x, vs) for x, vs in zip(a_k, v_ss)]
        w_ss = [_mm(t, jnp.concatenate([stack_heads(a_t[sl(c, p)]), stack_heads(x[0:ch])], axis=1))
                for t, x, (c, p) in zip(tinvs, kv, items)]
        ry = [_mm(x, w_s) for x, w_s in zip(a_rbs, w_ss)]
        r_eff = [r_t[sl(c, p)] + x[:, 0:pw] for x, (c, p) in zip(ry, items)]
        y0 = [x[:, pw:2 * pw] + y[ch:2 * ch] for x, y in zip(ry, kv)]
        bdc_ss = [stack_heads(b_dc[sl(c, p)]) for c, p in items]
        phi_t = [_mm(w_s[:, 0:pw].T, bd).astype(BF16) for w_s, bd in zip(w_ss, bdc_ss)]
        gamma_t = [_mm(jnp.concatenate([w_s[:, pw:2 * pw].T, vs.T], axis=1),
                       jnp.concatenate([bd, stack_heads(k_dc[sl(c, p)])], axis=0))
                   for w_s, vs, bd, (c, p) in zip(w_ss, v_ss, bdc_ss, items)]

        ys = [[] for _ in range(npair)]
        for c in range(gch):
            st_bs = [st.astype(BF16) for st in states]
            states = [states[p] * d_end[c * ch:c * ch + 1, p * pw:(p + 1) * pw]
                      + jnp.dot(st_bs[p], phi_t[c * npair + p], preferred_element_type=F32)
                      + gamma_t[c * npair + p] for p in range(npair)]
            for p in range(npair):
                i = c * npair + p
                ys[p].append(_mm_nt(r_eff[i], st_bs[p]) + y0[i])
        y = jnp.concatenate([jnp.concatenate(yp, axis=0) if gch > 1 else yp[0] for yp in ys], axis=1)

        mean = _mm_sel(y, ones_bd) * (1.0 / R7_N)
        yc = y - mean
        var = _mm_sel(yc * yc, ones_bd) * (1.0 / R7_N)
        yn = yc * lax.rsqrt(var + GN_EPS) * lnw_ref[...] + lnb_ref[...]
        bonus = _mm_sel(r * k * rk_ref[...], ones_bd) * v
        out_ref[0, g0:g0 + grows, :] = ((yn + bonus) * gate).astype(out_ref.dtype)
    for p in range(npair):
        st_ref[p] = states[p]
    xbuf_ref[0:HALO, :] = xbuf_ref[ts:ts + HALO, :]


def _rwkv7(z, mu, w0, w2, a0, a2, g2, k_k, k_a, r_k, ln_w, ln_b, ts):
    b, s, fin = z.shape
    w = GROUP_WIDTH
    npair = w // (2 * R7_N)
    vec = _const_spec((1, w))
    return pl.pallas_call(
        functools.partial(_rwkv7_kernel, ts),
        grid=(b, s // ts),
        in_specs=[
            pl.BlockSpec((1, ts, fin), lambda bi, ti: (bi, ti, 0)),
            _const_spec((1, fin)), vec, _const_spec(w2.shape), vec, _const_spec(a2.shape),
            _const_spec(g2.shape), vec, vec, vec, vec, vec,
        ],
        out_specs=pl.BlockSpec((1, ts, w), lambda bi, ti: (bi, ti, 0)),
        out_shape=jax.ShapeDtypeStruct((b, s, w), BF16),
        scratch_shapes=[pltpu.VMEM((HALO + ts, fin), F32), pltpu.VMEM((npair, 2 * R7_N, 2 * R7_N), F32)],
        compiler_params=pltpu.CompilerParams(
            dimension_semantics=("parallel", "arbitrary"), vmem_limit_bytes=VMEM_LIMIT),
        name="rwkv7",
    )(z, mu, w0, w2, a0, a2, g2, k_k, k_a, r_k, ln_w, ln_b)


def _even_weights(w_in, rg_w_r, rg_w_i, rg_b_r, rg_b_i, gdn_a_log, gdn_dt_bias):
    w = GROUP_WIDTH
    nh = w // HEAD128
    w_rg = w_in[:, 0:2 * w]
    w_gdn = jnp.pad(w_in[:, 2 * w:6 * w + 2 * nh], ((0, 0), (0, HEAD128 - 2 * nh)))
    wri = jnp.concatenate([jax.scipy.linalg.block_diag(*rg_w_r), jax.scipy.linalg.block_diag(*rg_w_i)], axis=1)
    bri = jnp.concatenate([rg_b_r, rg_b_i]).reshape(1, 2 * w)
    alog = jnp.pad(gdn_a_log, (0, HEAD128 - nh)).reshape(1, HEAD128)
    dtb = jnp.pad(gdn_dt_bias, (0, HEAD128 - nh)).reshape(1, HEAD128)
    return w_rg.astype(BF16), w_gdn.astype(BF16), wri.astype(BF16), bri, alog, dtb


def _pad_groups(x, axis):
    w = GROUP_WIDTH
    sizes = (w, w, w, 64, 64, 128)
    parts = []
    start = 0
    for sz in sizes:
        piece = lax.slice_in_dim(x, start, start + sz, axis=axis)
        pad = (-sz) % R7_LORA_PAD
        if pad:
            cfg = [(0, 0)] * x.ndim
            cfg[axis] = (0, pad)
            piece = jnp.pad(piece, cfg)
        parts.append(piece)
        start += sz
    return jnp.concatenate(parts, axis=axis)


def _pad_rows(x, rows):
    return jnp.pad(x, ((0, rows - x.shape[0]), (0, 0)))


def _tiles(b, s):
    t = b * s
    tm_proj = 512 if s % 512 == 0 else 256
    tm_post = 256
    ts = 256 if s % 256 == 0 else 128
    return tm_proj, tm_post, ts


def kernel(x, p, norm_mix, norm_ffn, norm_ple, norm_final, e_w_in, rg_conv_w, rg_conv_b, rg_w_r, rg_b_r, rg_w_i, rg_b_i, rg_lambda, gdn_conv_w, gdn_a_log, gdn_dt_bias, gdn_norm_w, e_w_out, o_w_in, hg_lower_bounds, hg_norm_w, r7_mu, r7_w0, r7_w2, r7_a0, r7_a2, r7_g2, r7_k_k, r7_k_a, r7_r_k, r7_ln_w, r7_ln_b, o_w_out, ffn_w_gate, ffn_w_up, ffn_w_down, ple_w_up, ple_w_gate):
    b, s, d = x.shape
    depth = p.shape[0]
    t = b * s
    w = GROUP_WIDTH
    tm_proj, tm_post, ts = _tiles(b, s)
    h = x.reshape(t, d)
    row = lambda vec: vec.reshape(1, -1)
    for layer in range(depth):
        j = layer // 2
        if layer % 2 == 0:
            w_rg, w_gdn, wri, bri, alog, dtb = _even_weights(
                e_w_in[j], rg_w_r[j], rg_w_i[j], rg_b_r[j], rg_b_i[j], gdn_a_log[j], gdn_dt_bias[j])
            out_a, z_b = _proj_rglru(h, norm_mix[layer], w_rg, w_gdn, rg_conv_w[j], row(rg_conv_b[j]),
                                     wri, bri, row(rg_lambda[j]), tm_proj, s)
            out_b = _gdn(z_b.reshape(b, s, -1), gdn_conv_w[j], alog, dtb, row(gdn_norm_w[j]),
                         2 * ts if s % (2 * ts) == 0 else ts)
            w_out = e_w_out[j]
        else:
            hg_in = 4 * w
            w_hg = o_w_in[j][:, 0:hg_in].astype(BF16)
            w_r7 = _pad_groups(o_w_in[j][:, hg_in:], axis=1).astype(BF16)
            z_a, z_b = _norm_proj(h, norm_mix[layer], w_hg, w_r7, tm_proj)
            out_a = _hgrn2(z_a.reshape(b, s, -1), hg_lower_bounds, row(hg_norm_w[j]), j, ts)
            out_b = _rwkv7(z_b.reshape(b, s, -1), row(_pad_groups(r7_mu[j], axis=0)), row(r7_w0[j]),
                           _pad_rows(r7_w2[j], R7_LORA_PAD).astype(BF16), row(r7_a0[j]),
                           _pad_rows(r7_a2[j], R7_LORA_PAD).astype(BF16), r7_g2[j].astype(BF16),
                           row(r7_k_k[j]), row(r7_k_a[j]), row(r7_r_k[j].reshape(-1)),
                           row(r7_ln_w[j]), row(r7_ln_b[j]), ts)
            w_out = o_w_out[j]
        h = _post(h, out_a.reshape(t, w), out_b.reshape(t, w), p.reshape(depth, t, PLE_DIM), layer,
                  w_out.astype(BF16), norm_ffn[layer], ffn_w_gate[layer].astype(BF16),
                  ffn_w_up[layer].astype(BF16), ffn_w_down[layer].astype(BF16), norm_ple[layer],
                  ple_w_gate[layer].astype(BF16), ple_w_up[layer].astype(BF16), norm_final,
                  layer == depth - 1, tm_post)
    return h.reshape(b, s, d)
```

```python
import functools
import math

import jax
import jax.numpy as jnp
from jax import lax
from jax.experimental import pallas as pl
from jax.experimental.pallas import tpu as pltpu

F32 = jnp.float32
BF16 = jnp.bfloat16

D_MODEL = 1024
PLE_DIM = 256
D_FF = 2816
GROUP_WIDTH = 512
RMS_EPS = 1e-6
L2_EPS = 1e-6
GN_EPS = 64e-5
RG_C = 8.0
RG_BLOCKS = 8
RG_ROWS = 64
CONV_TAPS = 4
HEAD128 = 128
R7_N = 64
R7_LORA_PAD = 128
R7_IN_PAD = 3 * GROUP_WIDTH + 3 * R7_LORA_PAD

GDN_CHUNK = 128
HG_CHUNK = 128
R7_CHUNK = 64
R7_GROUP = 4
HALO = 8

VMEM_LIMIT = 56 * 1024 * 1024


def _sigmoid(x):
    return 1.0 / (1.0 + jnp.exp(-x))


def _softplus(x):
    return jnp.maximum(x, 0.0) + jnp.log(1.0 + jnp.exp(-jnp.abs(x)))


def _silu(x):
    return x * _sigmoid(x)


def _gelu_tanh(x):
    c = math.sqrt(2.0 / math.pi)
    return 0.5 * x * (1.0 + jnp.tanh(c * (x + 0.044715 * (x * x * x))))


def _rms(x, w):
    ms = jnp.mean(x * x, axis=-1, keepdims=True)
    return x * lax.rsqrt(ms + RMS_EPS) * w


def _mm(a, b):
    return jnp.dot(a.astype(BF16), b.astype(BF16), preferred_element_type=F32)


def _mm_nt(a, b):
    return lax.dot_general(a.astype(BF16), b.astype(BF16), (((1,), (1,)), ((), ())),
                           preferred_element_type=F32)


def _split(x, terms):
    parts = []
    r = x
    for _ in range(terms):
        hi = r.astype(BF16)
        parts.append(hi)
        r = r - hi.astype(F32)
    return parts


def _sel_mm(sel, x, terms=3):
    acc = None
    for part in _split(x, terms):
        d = jnp.dot(sel, part, preferred_element_type=F32)
        acc = d if acc is None else acc + d
    return acc


def _mm_sel(x, sel, terms=2):
    acc = None
    for part in _split(x, terms):
        d = jnp.dot(part, sel, preferred_element_type=F32)
        acc = d if acc is None else acc + d
    return acc


def _mm_hi(a, b):
    a_hi, a_lo = _split(a, 2)
    b_hi, b_lo = _split(b, 2)
    return (jnp.dot(a_hi, b_hi, preferred_element_type=F32)
            + jnp.dot(a_lo, b_hi, preferred_element_type=F32)
            + jnp.dot(a_hi, b_lo, preferred_element_type=F32))


def _iota2(shape, axis):
    return lax.broadcasted_iota(jnp.int32, shape, axis)


def _block_tri(n, block, kind):
    t = _iota2((n, n), 0)
    s = _iota2((n, n), 1)
    same = (t // block) == (s // block)
    cond = (s <= t) if kind == "incl" else (s > t)
    return jnp.where(same & cond, 1.0, 0.0).astype(BF16)


def _chunk_last(x, chunk):
    cols = x.shape[1]
    return jnp.concatenate(
        [jnp.broadcast_to(x[r + chunk - 1:r + chunk, :], (chunk, cols)) for r in range(0, x.shape[0], chunk)],
        axis=0)


def _block_ones(n, block):
    t = _iota2((n, n), 0)
    s = _iota2((n, n), 1)
    return jnp.where((t // block) == (s // block), 1.0, 0.0).astype(BF16)


def _nilpotent_inverses(xs, n, refine):
    shape = xs[0].shape
    eye = jnp.where(_iota2(shape, 0) == _iota2(shape, 1), 1.0, 0.0).astype(F32)
    ts = [eye + x for x in xs]
    ps = list(xs)
    k = 2
    while k < n:
        ps = [_mm(p, p) for p in ps]
        ts = [t + _mm(t, p) for t, p in zip(ts, ps)]
        k *= 2
    if refine:
        res = [(eye - t) + _mm_hi(x, t) for x, t in zip(xs, ts)]
        ts = [t + _mm(t, r) for t, r in zip(ts, res)]
    return ts


def _norm_proj_kernel(h_ref, nw_ref, wa_ref, wb_ref, za_ref, zb_ref):
    xn = _rms(h_ref[...], nw_ref[...]).astype(BF16)
    za_ref[...] = jnp.dot(xn, wa_ref[...], preferred_element_type=F32)
    zb_ref[...] = jnp.dot(xn, wb_ref[...], preferred_element_type=F32)


def _const_spec(shape):
    nd = len(shape)
    return pl.BlockSpec(shape, lambda *_: (0,) * nd, pipeline_mode=pl.Buffered(1))


def _norm_proj(h, norm_w, wa, wb, tm):
    t, d = h.shape
    fa, fb = wa.shape[1], wb.shape[1]
    return pl.pallas_call(
        _norm_proj_kernel,
        grid=(t // tm,),
        in_specs=[
            pl.BlockSpec((tm, d), lambda i: (i, 0)),
            _const_spec((1, d)),
            _const_spec((d, fa)),
            _const_spec((d, fb)),
        ],
        out_specs=[
            pl.BlockSpec((tm, fa), lambda i: (i, 0)),
            pl.BlockSpec((tm, fb), lambda i: (i, 0)),
        ],
        out_shape=[jax.ShapeDtypeStruct((t, fa), F32), jax.ShapeDtypeStruct((t, fb), F32)],
        compiler_params=pltpu.CompilerParams(
            dimension_semantics=("parallel",), vmem_limit_bytes=VMEM_LIMIT),
        name="norm_proj",
    )(h, norm_w.reshape(1, d), wa, wb)


def _post_kernel(final_norm, h_ref, ma_ref, mb_ref, p_ref, wo_ref, nf_ref, wg_ref, wu_ref, wd_ref,
                 npl_ref, wpg_ref, wpu_ref, nfin_ref, out_ref):
    h = h_ref[...]
    mix = (jnp.dot(ma_ref[...], wo_ref[0:GROUP_WIDTH, :], preferred_element_type=F32)
           + jnp.dot(mb_ref[...], wo_ref[GROUP_WIDTH:2 * GROUP_WIDTH, :], preferred_element_type=F32))
    h1 = h + mix
    hn = _rms(h1, nf_ref[...]).astype(BF16)
    g = jnp.dot(hn, wg_ref[...], preferred_element_type=F32)
    u = jnp.dot(hn, wu_ref[...], preferred_element_type=F32)
    act = (_silu(g) * u).astype(BF16)
    h2 = h1 + jnp.dot(act, wd_ref[...], preferred_element_type=F32)
    hp = _rms(h2, npl_ref[...]).astype(BF16)
    gate = _sigmoid(jnp.dot(hp, wpg_ref[...], preferred_element_type=F32))
    up = jnp.dot(p_ref[...].astype(BF16), wpu_ref[...], preferred_element_type=F32)
    h3 = h2 + gate * up
    if final_norm:
        h3 = _rms(h3, nfin_ref[...])
    out_ref[...] = h3


def _post(h, ma, mb, p_all, layer, wo, nf, wg, wu, wd, npl, wpg, wpu, nfin, final_norm, tm):
    t, d = h.shape
    row = lambda w: pl.BlockSpec((tm, w), lambda i: (i, 0))
    return pl.pallas_call(
        functools.partial(_post_kernel, final_norm),
        grid=(t // tm,),
        in_specs=[
            row(d), row(GROUP_WIDTH), row(GROUP_WIDTH),
            pl.BlockSpec((None, tm, PLE_DIM), lambda i: (layer, i, 0)),
            _const_spec(wo.shape), _const_spec((1, d)),
            _const_spec(wg.shape), _const_spec(wu.shape), _const_spec(wd.shape),
            _const_spec((1, d)), _const_spec(wpg.shape), _const_spec(wpu.shape),
            _const_spec((1, d)),
        ],
        out_specs=row(d),
        out_shape=jax.ShapeDtypeStruct((t, d), F32),
        compiler_params=pltpu.CompilerParams(
            dimension_semantics=("parallel",), vmem_limit_bytes=VMEM_LIMIT),
        name="post_mixer",
    )(h, ma, mb, p_all, wo, nf.reshape(1, d), wg, wu, wd, npl.reshape(1, d), wpg, wpu, nfin.reshape(1, d))


def _causal_conv(xbuf_ref, x, w_ref, ts):
    xcat = jnp.concatenate([xbuf_ref[0:HALO, :], x], axis=0)
    acc = w_ref[CONV_TAPS - 1:CONV_TAPS, :] * x
    for back in range(1, CONV_TAPS):
        shifted = pltpu.roll(xcat, back, 0)[HALO:HALO + ts, :]
        acc = acc + w_ref[CONV_TAPS - 1 - back:CONV_TAPS - back, :] * shifted
    xbuf_ref[0:HALO, :] = x[ts - HALO:ts, :]
    return acc


def _proj_rglru_kernel(tm, tiles_per_seq, h_ref, nw_ref, wa_ref, wb_ref, cw_ref, cb_ref, wri_ref, bri_ref,
                       lam_ref, out_ref, zb_ref, xbuf_ref, hc_ref):
    w = GROUP_WIDTH
    first = (pl.program_id(0) % tiles_per_seq) == 0

    @pl.when(first)
    def _():
        xbuf_ref[0:HALO, :] = jnp.zeros((HALO, w), F32)
        hc_ref[...] = jnp.zeros_like(hc_ref)

    xn = _rms(h_ref[...], nw_ref[...]).astype(BF16)
    z_rg = jnp.dot(xn, wa_ref[...], preferred_element_type=F32)
    xa = z_rg[:, 0:w]
    ya = z_rg[:, w:2 * w]
    x = _causal_conv(xbuf_ref, xa, cw_ref, tm) + cb_ref[...]
    ri = jnp.dot(x.astype(BF16), wri_ref[...], preferred_element_type=F32) + bri_ref[...]
    zb_ref[...] = jnp.dot(xn, wb_ref[...], preferred_element_type=F32)
    sp_lam = _softplus(-lam_ref[...])
    lanes = HEAD128
    row = _iota2((RG_ROWS, lanes), 0)
    for l0 in range(0, w, lanes):
        ls = slice(l0, l0 + lanes)
        carry = hc_ref[:, ls]
        for r0 in range(0, tm, RG_ROWS):
            rs = slice(r0, r0 + RG_ROWS)
            r = _sigmoid(ri[rs, l0:l0 + lanes])
            i = _sigmoid(ri[rs, w + l0:w + l0 + lanes])
            log_a = -RG_C * r * sp_lam[:, ls]
            a = jnp.exp(log_a)
            mult = jnp.sqrt(-jnp.tanh(log_a) * (a * a + 1.0))
            if r0 == 0:
                mult = jnp.where((row == 0) & first, 1.0, mult)
            acc_a, acc_h = a, mult * (i * x[rs, ls])
            d = 1
            while d < RG_ROWS:
                if d % HALO == 0:
                    sh_a = jnp.concatenate([jnp.ones((d, lanes), F32), acc_a[0:RG_ROWS - d, :]], axis=0)
                    sh_h = jnp.concatenate([jnp.zeros((d, lanes), F32), acc_h[0:RG_ROWS - d, :]], axis=0)
                else:
                    sh_a = jnp.where(row >= d, pltpu.roll(acc_a, d, 0), 1.0)
                    sh_h = jnp.where(row >= d, pltpu.roll(acc_h, d, 0), 0.0)
                acc_h = acc_a * sh_h + acc_h
                acc_a = acc_a * sh_a
                d *= 2
            h = acc_h + acc_a * carry
            carry = h[RG_ROWS - 1:RG_ROWS, :]
            out_ref[rs, ls] = (h * _gelu_tanh(ya[rs, ls])).astype(out_ref.dtype)
        hc_ref[:, ls] = carry


def _proj_rglru(h, norm_w, wa, wb, cw, cb, wri, bri, lam, tm, seq_len):
    t, d = h.shape
    w = GROUP_WIDTH
    fb = wb.shape[1]
    return pl.pallas_call(
        functools.partial(_proj_rglru_kernel, tm, seq_len // tm),
        grid=(t // tm,),
        in_specs=[
            pl.BlockSpec((tm, d), lambda i: (i, 0)),
            _const_spec((1, d)), _const_spec(wa.shape), _const_spec(wb.shape),
            _const_spec((CONV_TAPS, w)), _const_spec((1, w)),
            _const_spec((w, 2 * w)), _const_spec((1, 2 * w)), _const_spec((1, w)),
        ],
        out_specs=[
            pl.BlockSpec((tm, w), lambda i: (i, 0)),
            pl.BlockSpec((tm, fb), lambda i: (i, 0)),
        ],
        out_shape=[jax.ShapeDtypeStruct((t, w), BF16), jax.ShapeDtypeStruct((t, fb), F32)],
        scratch_shapes=[pltpu.VMEM((HALO + tm, w), F32), pltpu.VMEM((1, w), F32)],
        compiler_params=pltpu.CompilerParams(
            dimension_semantics=("arbitrary",), vmem_limit_bytes=VMEM_LIMIT),
        name="proj_rglru",
    )(h, norm_w.reshape(1, d), wa, wb, cw, cb, wri, bri, lam)


def _gdn_kernel(ts, z_ref, cw_ref, alog_ref, dtb_ref, nw_ref, out_ref, xbuf_ref, st_ref):
    w = GROUP_WIDTH
    hd = HEAD128
    nh = w // hd
    cs = GDN_CHUNK
    nchunk = ts // cs
    tstep = pl.program_id(1)

    @pl.when(tstep == 0)
    def _():
        xbuf_ref[0:HALO, :] = jnp.zeros((HALO, 3 * w), F32)
        st_ref[...] = jnp.zeros_like(st_ref)

    qkv = _silu(_causal_conv(xbuf_ref, z_ref[0, :, 0:3 * w], cw_ref, ts))
    zg = z_ref[0, :, 3 * w:4 * w]
    ab = z_ref[0, :, 4 * w:4 * w + hd]
    g_n = -jnp.exp(alog_ref[...]) * _softplus(ab + dtb_ref[...])
    gc_n = _sel_mm(_block_tri(ts, cs, "incl"), g_n)
    src = _iota2((hd, w), 0)
    dst_head = _iota2((hd, w), 1) // hd
    gc_all = _mm_sel(gc_n, jnp.where(src == dst_head, 1.0, 0.0).astype(BF16), terms=3)
    beta = _mm_sel(_sigmoid(ab), jnp.where(src == dst_head + nh, 1.0, 0.0).astype(BF16), terms=3)
    gs_all = _chunk_last(gc_all, cs) - gc_all
    eg_all = jnp.exp(gc_all)
    scale = hd ** -0.5
    tt = _iota2((cs, cs), 0)
    ss = _iota2((cs, cs), 1)

    items = [(h, c) for c in range(nchunk) for h in range(nh)]
    qs, ks, vbs, kbegs, decs, kdts, glast = [], [], [], [], [], [], []
    for h, c in items:
        lo = h * hd
        rows = slice(c * cs, (c + 1) * cs)
        hs = slice(lo, lo + hd)
        qh = qkv[rows, lo:lo + hd]
        kh = qkv[rows, w + lo:w + lo + hd]
        vh = qkv[rows, 2 * w + lo:2 * w + lo + hd]
        qh = qh * lax.rsqrt(jnp.sum(qh * qh, axis=-1, keepdims=True) + L2_EPS) * scale
        kh = kh * lax.rsqrt(jnp.sum(kh * kh, axis=-1, keepdims=True) + L2_EPS)
        gc = gc_all[rows, hs]
        bt = beta[rows, hs]
        eg = eg_all[rows, hs]
        kb = kh * bt
        diff = gc - gc.T
        decs.append(jnp.exp(jnp.where(ss <= tt, diff, -jnp.inf)))
        qs.append(qh * eg)
        ks.append((kh, kb, qh))
        vbs.append(vh * bt)
        kbegs.append(kb * eg)
        kdts.append((kh * jnp.exp(gs_all[rows, hs])).T)
        glast.append(jnp.exp(gc_all[(c + 1) * cs - 1:(c + 1) * cs, hs]))
    a_mats = [jnp.where(ss < tt, _mm_nt(kb, kh) * dec, 0.0) for (kh, kb, _), dec in zip(ks, decs)]
    qks = [_mm_nt(qh, kh) * dec for (kh, _, qh), dec in zip(ks, decs)]
    tinvs = _nilpotent_inverses([-a for a in a_mats], cs, refine=True)
    sols = [_mm(t, jnp.concatenate([vb, kbeg], axis=1)) for t, vb, kbeg in zip(tinvs, vbs, kbegs)]
    us = [sol[:, 0:hd] for sol in sols]
    ws = [sol[:, hd:2 * hd] for sol in sols]

    states = [st_ref[h] for h in range(nh)]
    outs = [[] for _ in range(nh)]
    for c in range(nchunk):
        idx = [c * nh + h for h in range(nh)]
        proj = [_mm(jnp.concatenate([ws[i], qs[i]], axis=0), states[h]) for h, i in enumerate(idx)]
        v_new = [us[i] - pr[0:cs] for i, pr in zip(idx, proj)]
        o = [pr[cs:2 * cs] + _mm(qks[i], vn) for i, pr, vn in zip(idx, proj, v_new)]
        states = [glast[i] * states[h] + _mm(kdts[i], vn) for (h, i), vn in zip(enumerate(idx), v_new)]
        for h in range(nh):
            outs[h].append(o[h])
    for h in range(nh):
        lo = h * hd
        st_ref[h] = states[h]
        o = jnp.concatenate(outs[h], axis=0) if nchunk > 1 else outs[h][0]
        o = _rms(o, nw_ref[...]) * _silu(zg[:, lo:lo + hd])
        out_ref[0, :, lo:lo + hd] = o.astype(out_ref.dtype)


def _gdn(z, cw, alog, dtb, nw, ts):
    b, s, _ = z.shape
    w = GROUP_WIDTH
    nh = w // HEAD128
    return pl.pallas_call(
        functools.partial(_gdn_kernel, ts),
        grid=(b, s // ts),
        in_specs=[
            pl.BlockSpec((1, ts, 4 * w + HEAD128), lambda bi, ti: (bi, ti, 0)),
            _const_spec((CONV_TAPS, 3 * w)), _const_spec((1, HEAD128)), _const_spec((1, HEAD128)),
            _const_spec((1, HEAD128)),
        ],
        out_specs=pl.BlockSpec((1, ts, w), lambda bi, ti: (bi, ti, 0)),
        out_shape=jax.ShapeDtypeStruct((b, s, w), BF16),
        scratch_shapes=[pltpu.VMEM((HALO + ts, 3 * w), F32), pltpu.VMEM((nh, HEAD128, HEAD128), F32)],
        compiler_params=pltpu.CompilerParams(
            dimension_semantics=("parallel", "arbitrary"), vmem_limit_bytes=VMEM_LIMIT),
        name="gated_deltanet",
    )(z, cw, alog, dtb, nw)


def _hgrn2_kernel(ts, layer_j, z_ref, lbraw_ref, nw_ref, out_ref, st_ref, o_ref):
    w = GROUP_WIDTH
    hd = HEAD128
    nh = w // hd
    tstep = pl.program_id(1)

    @pl.when(tstep == 0)
    def _():
        st_ref[...] = jnp.zeros_like(st_ref)

    lbraw = lbraw_ref[...]
    e = jnp.exp(lbraw - jnp.max(lbraw, axis=0, keepdims=True))
    soft = e / jnp.sum(e, axis=0, keepdims=True)
    lb = jnp.sum(soft[0:layer_j + 1, :], axis=0, keepdims=True) - soft[0:1, :]
    lb = jnp.maximum(lb, 0.0)

    q = _silu(z_ref[0, :, 0:w])
    f = z_ref[0, :, w:2 * w]
    v = z_ref[0, :, 2 * w:3 * w]
    gate = z_ref[0, :, 3 * w:4 * w]
    kin = (1.0 - lb) * _sigmoid(-f)
    la = jnp.log(lb)
    lbm = jnp.log1p(-lb) - _softplus(-f)
    mx = jnp.maximum(la, lbm)
    log_f = mx + jnp.log(1.0 + jnp.exp(-jnp.abs(la - lbm)))

    cs = HG_CHUNK
    nchunk = ts // cs
    b = _sel_mm(_block_tri(ts, cs, "incl"), log_f)
    b2 = b * math.log2(math.e)
    row = _iota2((ts, w), 0)
    tt = _iota2((cs, cs), 0)
    ss = _iota2((cs, cs), 1)
    items = [(c, h) for c in range(nchunk) for h in range(nh)]
    sl = lambda c, h: (slice(c * cs, (c + 1) * cs), slice(h * hd, (h + 1) * hd))

    scores = [jnp.where(tt == ss, _mm_nt(q[sl(c, h)], kin[sl(c, h)]), 0.0) for c, h in items]
    block_end = b2
    m = 1
    while m < cs:
        prev_end = pltpu.roll(block_end, m, 0)
        q_m = q * jnp.exp2(b2 - prev_end)
        k_m = kin * jnp.exp2(block_end - b2)
        pair = ((tt // m) % 2 == 1) & ((ss // m) == (tt // m) - 1)
        scores = [jnp.where(pair, _mm_nt(q_m[sl(c, h)], k_m[sl(c, h)]), sc)
                  for sc, (c, h) in zip(scores, items)]
        block_end = jnp.where((row // m) % 2 == 0, pltpu.roll(block_end, ts - m, 0), block_end)
        m *= 2
    o_intra = [_mm(sc, v[sl(c, h)]) for sc, (c, h) in zip(scores, items)]

    btot = _chunk_last(b, cs)
    qe = q * jnp.exp(b)
    k_dec = kin * jnp.exp(btot - b)
    dec = jnp.exp(btot)
    incs = [_mm(v[sl(c, h)].T, k_dec[sl(c, h)]) for c, h in items]
    states = [st_ref[h] for h in range(nh)]
    for c in range(nchunk):
        for h in range(nh):
            i = c * nh + h
            o_ref[c * cs:(c + 1) * cs, h * hd:(h + 1) * hd] = o_intra[i] + _mm_nt(qe[sl(c, h)], states[h])
        states = [states[h] * dec[c * cs:c * cs + 1, h * hd:(h + 1) * hd] + incs[c * nh + h]
                  for h in range(nh)]
    for h in range(nh):
        st_ref[h] = states[h]

    out_ref[0] = (_rms(o_ref[...], nw_ref[...]) * _sigmoid(gate)).astype(out_ref.dtype)


def _hgrn2(z, lbraw, nw, layer_j, ts):
    b, s, _ = z.shape
    w = GROUP_WIDTH
    nh = w // HEAD128
    return pl.pallas_call(
        functools.partial(_hgrn2_kernel, ts, layer_j),
        grid=(b, s // ts),
        in_specs=[
            pl.BlockSpec((1, ts, 4 * w), lambda bi, ti: (bi, ti, 0)),
            _const_spec(lbraw.shape), _const_spec((1, w)),
        ],
        out_specs=pl.BlockSpec((1, ts, w), lambda bi, ti: (bi, ti, 0)),
        out_shape=jax.ShapeDtypeStruct((b, s, w), BF16),
        scratch_shapes=[pltpu.VMEM((nh, HEAD128, HEAD128), F32), pltpu.VMEM((ts, w), F32)],
        compiler_params=pltpu.CompilerParams(
            dimension_semantics=("parallel", "arbitrary"), vmem_limit_bytes=VMEM_LIMIT),
        name="hgrn2",
    )(z, lbraw, nw)


def _rwkv7_kernel(ts, z_ref, mu_ref, w0_ref, w2_ref, a0_ref, a2_ref, g2_ref, kk_ref, ka_ref, rk_ref,
                  lnw_ref, lnb_ref, out_ref, xbuf_ref, st_ref):
    w = GROUP_WIDTH
    lp = R7_LORA_PAD
    ch = R7_CHUNK
    npair = w // (2 * R7_N)
    tstep = pl.program_id(1)

    @pl.when(tstep == 0)
    def _():
        xbuf_ref[0:HALO, :] = jnp.zeros((HALO, R7_IN_PAD), F32)
        st_ref[...] = jnp.zeros_like(st_ref)

    pw = 2 * R7_N
    n = R7_N
    gch = min(R7_GROUP, ts // ch)
    grows = gch * ch
    lane = _iota2((ch, pw), 1)
    in_h0 = lane < n
    t_row = _iota2((ch, pw), 0)
    s_col = lane % n
    ones_bd = _block_ones(w, R7_N)
    tri_incl = _block_tri(grows, ch, "incl")
    items = [(c, p) for c in range(gch) for p in range(npair)]
    sl = lambda c, p: (slice(c * ch, (c + 1) * ch), slice(p * pw, (p + 1) * pw))

    def stack_heads(x):
        return jnp.concatenate([jnp.where(in_h0, x, 0.0), jnp.where(in_h0, 0.0, x)], axis=0)

    xbuf_ref[HALO:HALO + ts, :] = z_ref[0]
    states = [st_ref[p] for p in range(npair)]
    for g0 in range(0, ts, grows):
        z = xbuf_ref[HALO + g0:HALO + g0 + grows, :]
        z_prev = xbuf_ref[HALO - 1 + g0:HALO - 1 + g0 + grows, :]
        zd = z + mu_ref[...] * (z_prev - z)
        r = zd[:, 0:w]
        k = zd[:, w:2 * w]
        v = zd[:, 2 * w:3 * w]
        w_l = zd[:, 3 * w:3 * w + lp]
        a_l = zd[:, 3 * w + lp:3 * w + 2 * lp]
        g_l = zd[:, 3 * w + 2 * lp:3 * w + 3 * lp]
        log_w = -jnp.exp(-_softplus(-(w0_ref[...] + _mm(jnp.tanh(w_l), w2_ref[...]))) - 0.5)
        a = _sigmoid(a0_ref[...] + _mm(a_l, a2_ref[...]))
        gate = _mm(_sigmoid(g_l), g2_ref[...])
        kk = k * kk_ref[...]
        kk = kk * lax.rsqrt(_mm_sel(kk * kk, ones_bd) + L2_EPS)
        k = k * (1.0 + (a - 1.0) * ka_ref[...])
        cum = _sel_mm(tri_incl, log_w)
        total = _chunk_last(cum, ch)
        e_inv = jnp.exp(-cum)
        r_t = r * jnp.exp(cum)
        a_t = -kk * jnp.exp(cum - log_w)
        kka = kk * a
        b_h = kka * e_inv
        k_h = k * e_inv
        e_suf = jnp.exp(total - cum)
        b_dc = kka * e_suf
        k_dc = k * e_suf
        d_end = jnp.exp(total)

        v_ss = [stack_heads(v[sl(c, p)]) for c, p in items]
        grams = [
            _mm_nt(jnp.concatenate([a_t[sl(c, p)], r_t[sl(c, p)]], axis=0),
                   jnp.concatenate([stack_heads(b_h[sl(c, p)]), stack_heads(k_h[sl(c, p)])], axis=0))
            for c, p in items]
        a_abs = [jnp.where(s_col < t_row, g[0:ch, 0:pw], 0.0) for g in grams]
        a_rbs = [jnp.where(s_col <= t_row, g[ch:2 * ch, 0:pw], 0.0) for g in grams]
        a_k = [jnp.concatenate([jnp.where(s_col < t_row, g[0:ch, pw:2 * pw], 0.0),
                                jnp.where(s_col <= t_row, g[ch:2 * ch, pw:2 * pw], 0.0)], axis=0)
               for g in grams]
        eye_cat = jnp.where(s_col == t_row, 1.0, 0.0).astype(F32)
        t_cats = [eye_cat for _ in a_abs]
        p_cats = list(a_abs)
        k_pow = 1
        while k_pow < ch:
            both = [jnp.dot(jnp.concatenate([p, t], axis=0).astype(BF16), stack_heads(p).astype(BF16),
                            preferred_element_type=F32) for p, t in zip(p_cats, t_cats)]
            p_cats = [x[0:ch] for x in both]
            t_cats = [t + x[ch:2 * ch] for t, x in zip(t_cats, both)]
            k_pow *= 2
        tinvs = [stack_heads(t) for t in t_cats]
        kv = [_mm(x, vs) for x, vs in zip(a_k, v_ss)]
        w_ss = [_mm(t, jnp.concatenate([stack_heads(a_t[sl(c, p)]), stack_heads(x[0:ch])], axis=1))
                for t, x, (c, p) in zip(tinvs, kv, items)]
        ry = [_mm(x, w_s) for x, w_s in zip(a_rbs, w_ss)]
        r_eff = [r_t[sl(c, p)] + x[:, 0:pw] for x, (c, p) in zip(ry, items)]
        y0 = [x[:, pw:2 * pw] + y[ch:2 * ch] for x, y in zip(ry, kv)]
        bdc_ss = [stack_heads(b_dc[sl(c, p)]) for c, p in items]
        phi_t = [_mm(w_s[:, 0:pw].T, bd).astype(BF16) for w_s, bd in zip(w_ss, bdc_ss)]
        gamma_t = [_mm(jnp.concatenate([w_s[:, pw:2 * pw].T, vs.T], axis=1),
                       jnp.concatenate([bd, stack_heads(k_dc[sl(c, p)])], axis=0))
                   for w_s, vs, bd, (c, p) in zip(w_ss, v_ss, bdc_ss, items)]

        ys = [[] for _ in range(npair)]
        for c in range(gch):
            st_bs = [st.astype(BF16) for st in states]
            states = [states[p] * d_end[c * ch:c * ch + 1, p * pw:(p + 1) * pw]
                      + jnp.dot(st_bs[p], phi_t[c * npair + p], preferred_element_type=F32)
                      + gamma_t[c * npair + p] for p in range(npair)]
            for p in range(npair):
                i = c * npair + p
                ys[p].append(_mm_nt(r_eff[i], st_bs[p]) + y0[i])
        y = jnp.concatenate([jnp.concatenate(yp, axis=0) if gch > 1 else yp[0] for yp in ys], axis=1)

        mean = _mm_sel(y, ones_bd) * (1.0 / R7_N)
        yc = y - mean
        var = _mm_sel(yc * yc, ones_bd) * (1.0 / R7_N)
        yn = yc * lax.rsqrt(var + GN_EPS) * lnw_ref[...] + lnb_ref[...]
        bonus = _mm_sel(r * k * rk_ref[...], ones_bd) * v
        out_ref[0, g0:g0 + grows, :] = ((yn + bonus) * gate).astype(out_ref.dtype)
    for p in range(npair):
        st_ref[p] = states[p]
    xbuf_ref[0:HALO, :] = xbuf_ref[ts:ts + HALO, :]


def _rwkv7(z, mu, w0, w2, a0, a2, g2, k_k, k_a, r_k, ln_w, ln_b, ts):
    b, s, fin = z.shape
    w = GROUP_WIDTH
    npair = w // (2 * R7_N)
    vec = _const_spec((1, w))
    return pl.pallas_call(
        functools.partial(_rwkv7_kernel, ts),
        grid=(b, s // ts),
        in_specs=[
            pl.BlockSpec((1, ts, fin), lambda bi, ti: (bi, ti, 0)),
            _const_spec((1, fin)), vec, _const_spec(w2.shape), vec, _const_spec(a2.shape),
            _const_spec(g2.shape), vec, vec, vec, vec, vec,
        ],
        out_specs=pl.BlockSpec((1, ts, w), lambda bi, ti: (bi, ti, 0)),
        out_shape=jax.ShapeDtypeStruct((b, s, w), BF16),
        scratch_shapes=[pltpu.VMEM((HALO + ts, fin), F32), pltpu.VMEM((npair, 2 * R7_N, 2 * R7_N), F32)],
        compiler_params=pltpu.CompilerParams(
            dimension_semantics=("parallel", "arbitrary"), vmem_limit_bytes=VMEM_LIMIT),
        name="rwkv7",
    )(z, mu, w0, w2, a0, a2, g2, k_k, k_a, r_k, ln_w, ln_b)


def _even_weights(w_in, rg_w_r, rg_w_i, rg_b_r, rg_b_i, gdn_a_log, gdn_dt_bias):
    w = GROUP_WIDTH
    nh = w // HEAD128
    w_rg = w_in[:, 0:2 * w]
    w_gdn = jnp.pad(w_in[:, 2 * w:6 * w + 2 * nh], ((0, 0), (0, HEAD128 - 2 * nh)))
    wri = jnp.concatenate([jax.scipy.linalg.block_diag(*rg_w_r), jax.scipy.linalg.block_diag(*rg_w_i)], axis=1)
    bri = jnp.concatenate([rg_b_r, rg_b_i]).reshape(1, 2 * w)
    alog = jnp.pad(gdn_a_log, (0, HEAD128 - nh)).reshape(1, HEAD128)
    dtb = jnp.pad(gdn_dt_bias, (0, HEAD128 - nh)).reshape(1, HEAD128)
    return w_rg.astype(BF16), w_gdn.astype(BF16), wri.astype(BF16), bri, alog, dtb


def _pad_groups(x, axis):
    w = GROUP_WIDTH
    sizes = (w, w, w, 64, 64, 128)
    parts = []
    start = 0
    for sz in sizes:
        piece = lax.slice_in_dim(x, start, start + sz, axis=axis)
        pad = (-sz) % R7_LORA_PAD
        if pad:
            cfg = [(0, 0)] * x.ndim
            cfg[axis] = (0, pad)
            piece = jnp.pad(piece, cfg)
        parts.append(piece)
        start += sz
    return jnp.concatenate(parts, axis=axis)


def _pad_rows(x, rows):
    return jnp.pad(x, ((0, rows - x.shape[0]), (0, 0)))


def _tiles(b, s):
    t = b * s
    tm_proj = 512 if s % 512 == 0 else 256
    tm_post = 256
    ts = 256 if s % 256 == 0 else 128
    return tm_proj, tm_post, ts


def kernel(x, p, norm_mix, norm_ffn, norm_ple, norm_final, e_w_in, rg_conv_w, rg_conv_b, rg_w_r, rg_b_r, rg_w_i, rg_b_i, rg_lambda, gdn_conv_w, gdn_a_log, gdn_dt_bias, gdn_norm_w, e_w_out, o_w_in, hg_lower_bounds, hg_norm_w, r7_mu, r7_w0, r7_w2, r7_a0, r7_a2, r7_g2, r7_k_k, r7_k_a, r7_r_k, r7_ln_w, r7_ln_b, o_w_out, ffn_w_gate, ffn_w_up, ffn_w_down, ple_w_up, ple_w_gate):
    b, s, d = x.shape
    depth = p.shape[0]
    t = b * s
    w = GROUP_WIDTH
    tm_proj, tm_post, ts = _tiles(b, s)
    h = x.reshape(t, d)
    row = lambda vec: vec.reshape(1, -1)
    for layer in range(depth):
        j = layer // 2
        if layer % 2 == 0:
            w_rg, w_gdn, wri, bri, alog, dtb = _even_weights(
                e_w_in[j], rg_w_r[j], rg_w_i[j], rg_b_r[j], rg_b_i[j], gdn_a_log[j], gdn_dt_bias[j])
            out_a, z_b = _proj_rglru(h, norm_mix[layer], w_rg, w_gdn, rg_conv_w[j], row(rg_conv_b[j]),
                                     wri, bri, row(rg_lambda[j]), tm_proj, s)
            out_b = _gdn(z_b.reshape(b, s, -1), gdn_conv_w[j], alog, dtb, row(gdn_norm_w[j]),
                         2 * ts if s % (2 * ts) == 0 else ts)
            w_out = e_w_out[j]
        else:
            hg_in = 4 * w
            w_hg = o_w_in[j][:, 0:hg_in].astype(BF16)
            w_r7 = _pad_groups(o_w_in[j][:, hg_in:], axis=1).astype(BF16)
            z_a, z_b = _norm_proj(h, norm_mix[layer], w_hg, w_r7, tm_proj)
            out_a = _hgrn2(z_a.reshape(b, s, -1), hg_lower_bounds, row(hg_norm_w[j]), j, ts)
            out_b = _rwkv7(z_b.reshape(b, s, -1), row(_pad_groups(r7_mu[j], axis=0)), row(r7_w0[j]),
                           _pad_rows(r7_w2[j], R7_LORA_PAD).astype(BF16), row(r7_a0[j]),
                           _pad_rows(r7_a2[j], R7_LORA_PAD).astype(BF16), r7_g2[j].astype(BF16),
                           row(r7_k_k[j]), row(r7_k_a[j]), row(r7_r_k[j].reshape(-1)),
                           row(r7_ln_w[j]), row(r7_ln_b[j]), ts)
            w_out = o_w_out[j]
        h = _post(h, out_a.reshape(t, w), out_b.reshape(t, w), p.reshape(depth, t, PLE_DIM), layer,
                  w_out.astype(BF16), norm_ffn[layer], ffn_w_gate[layer].astype(BF16),
                  ffn_w_up[layer].astype(BF16), ffn_w_down[layer].astype(BF16), norm_ple[layer],
                  ple_w_gate[layer].astype(BF16), ple_w_up[layer].astype(BF16), norm_final,
                  layer == depth - 1, tm_post)
    return h.reshape(b, s, d)
```

```python
import functools
import math

import jax
import jax.numpy as jnp
from jax import lax
from jax.experimental import pallas as pl
from jax.experimental.pallas import tpu as pltpu

F32 = jnp.float32
BF16 = jnp.bfloat16

D_MODEL = 1024
PLE_DIM = 256
D_FF = 2816
GROUP_WIDTH = 512
RMS_EPS = 1e-6
L2_EPS = 1e-6
GN_EPS = 64e-5
RG_C = 8.0
RG_BLOCKS = 8
RG_ROWS = 64
CONV_TAPS = 4
HEAD128 = 128
R7_N = 64
R7_LORA_PAD = 128
R7_IN_PAD = 3 * GROUP_WIDTH + 3 * R7_LORA_PAD

GDN_CHUNK = 128
HG_CHUNK = 128
R7_CHUNK = 64
R7_GROUP = 4
HALO = 8

VMEM_LIMIT = 56 * 1024 * 1024


def _sigmoid(x):
    return 1.0 / (1.0 + jnp.exp(-x))


def _softplus(x):
    return jnp.maximum(x, 0.0) + jnp.log(1.0 + jnp.exp(-jnp.abs(x)))


def _silu(x):
    return x * _sigmoid(x)


def _gelu_tanh(x):
    c = math.sqrt(2.0 / math.pi)
    return 0.5 * x * (1.0 + jnp.tanh(c * (x + 0.044715 * (x * x * x))))


def _rms(x, w):
    ms = jnp.mean(x * x, axis=-1, keepdims=True)
    return x * lax.rsqrt(ms + RMS_EPS) * w


def _mm(a, b):
    return jnp.dot(a.astype(BF16), b.astype(BF16), preferred_element_type=F32)


def _mm_nt(a, b):
    return lax.dot_general(a.astype(BF16), b.astype(BF16), (((1,), (1,)), ((), ())),
                           preferred_element_type=F32)


def _split(x, terms):
    parts = []
    r = x
    for _ in range(terms):
        hi = r.astype(BF16)
        parts.append(hi)
        r = r - hi.astype(F32)
    return parts


def _sel_mm(sel, x, terms=3):
    acc = None
    for part in _split(x, terms):
        d = jnp.dot(sel, part, preferred_element_type=F32)
        acc = d if acc is None else acc + d
    return acc


def _mm_sel(x, sel, terms=2):
    acc = None
    for part in _split(x, terms):
        d = jnp.dot(part, sel, preferred_element_type=F32)
        acc = d if acc is None else acc + d
    return acc


def _mm_hi(a, b):
    a_hi, a_lo = _split(a, 2)
    b_hi, b_lo = _split(b, 2)
    return (jnp.dot(a_hi, b_hi, preferred_element_type=F32)
            + jnp.dot(a_lo, b_hi, preferred_element_type=F32)
            + jnp.dot(a_hi, b_lo, preferred_element_type=F32))


def _iota2(shape, axis):
    return lax.broadcasted_iota(jnp.int32, shape, axis)


def _block_tri(n, block, kind):
    t = _iota2((n, n), 0)
    s = _iota2((n, n), 1)
    same = (t // block) == (s // block)
    cond = (s <= t) if kind == "incl" else (s > t)
    return jnp.where(same & cond, 1.0, 0.0).astype(BF16)


def _chunk_last(x, chunk):
    cols = x.shape[1]
    return jnp.concatenate(
        [jnp.broadcast_to(x[r + chunk - 1:r + chunk, :], (chunk, cols)) for r in range(0, x.shape[0], chunk)],
        axis=0)


def _block_ones(n, block):
    t = _iota2((n, n), 0)
    s = _iota2((n, n), 1)
    return jnp.where((t // block) == (s // block), 1.0, 0.0).astype(BF16)


def _nilpotent_inverses(xs, n, refine):
    shape = xs[0].shape
    eye = jnp.where(_iota2(shape, 0) == _iota2(shape, 1), 1.0, 0.0).astype(F32)
    ts = [eye + x for x in xs]
    ps = list(xs)
    k = 2
    while k < n:
        ps = [_mm(p, p) for p in ps]
        ts = [t + _mm(t, p) for t, p in zip(ts, ps)]
        k *= 2
    if refine:
        res = [(eye - t) + _mm_hi(x, t) for x, t in zip(xs, ts)]
        ts = [t + _mm(t, r) for t, r in zip(ts, res)]
    return ts


def _norm_proj_kernel(h_ref, nw_ref, wa_ref, wb_ref, za_ref, zb_ref):
    xn = _rms(h_ref[...], nw_ref[...]).astype(BF16)
    za_ref[...] = jnp.dot(xn, wa_ref[...], preferred_element_type=F32)
    zb_ref[...] = jnp.dot(xn, wb_ref[...], preferred_element_type=F32)


def _const_spec(shape):
    nd = len(shape)
    return pl.BlockSpec(shape, lambda *_: (0,) * nd, pipeline_mode=pl.Buffered(1))


def _norm_proj(h, norm_w, wa, wb, tm):
    t, d = h.shape
    fa, fb = wa.shape[1], wb.shape[1]
    return pl.pallas_call(
        _norm_proj_kernel,
        grid=(t // tm,),
        in_specs=[
            pl.BlockSpec((tm, d), lambda i: (i, 0)),
            _const_spec((1, d)),
            _const_spec((d, fa)),
            _const_spec((d, fb)),
        ],
        out_specs=[
            pl.BlockSpec((tm, fa), lambda i: (i, 0)),
            pl.BlockSpec((tm, fb), lambda i: (i, 0)),
        ],
        out_shape=[jax.ShapeDtypeStruct((t, fa), F32), jax.ShapeDtypeStruct((t, fb), F32)],
        compiler_params=pltpu.CompilerParams(
            dimension_semantics=("parallel",), vmem_limit_bytes=VMEM_LIMIT),
        name="norm_proj",
    )(h, norm_w.reshape(1, d), wa, wb)


def _post_kernel(final_norm, h_ref, ma_ref, mb_ref, p_ref, wo_ref, nf_ref, wg_ref, wu_ref, wd_ref,
                 npl_ref, wpg_ref, wpu_ref, nfin_ref, out_ref):
    h = h_ref[...]
    mix = (jnp.dot(ma_ref[...], wo_ref[0:GROUP_WIDTH, :], preferred_element_type=F32)
           + jnp.dot(mb_ref[...], wo_ref[GROUP_WIDTH:2 * GROUP_WIDTH, :], preferred_element_type=F32))
    h1 = h + mix
    hn = _rms(h1, nf_ref[...]).astype(BF16)
    g = jnp.dot(hn, wg_ref[...], preferred_element_type=F32)
    u = jnp.dot(hn, wu_ref[...], preferred_element_type=F32)
    act = (_silu(g) * u).astype(BF16)
    h2 = h1 + jnp.dot(act, wd_ref[...], preferred_element_type=F32)
    hp = _rms(h2, npl_ref[...]).astype(BF16)
    gate = _sigmoid(jnp.dot(hp, wpg_ref[...], preferred_element_type=F32))
    up = jnp.dot(p_ref[...].astype(BF16), wpu_ref[...], preferred_element_type=F32)
    h3 = h2 + gate * up
    if final_norm:
        h3 = _rms(h3, nfin_ref[...])
    out_ref[...] = h3


def _post(h, ma, mb, p_all, layer, wo, nf, wg, wu, wd, npl, wpg, wpu, nfin, final_norm, tm):
    t, d = h.shape
    row = lambda w: pl.BlockSpec((tm, w), lambda i: (i, 0))
    return pl.pallas_call(
        functools.partial(_post_kernel, final_norm),
        grid=(t // tm,),
        in_specs=[
            row(d), row(GROUP_WIDTH), row(GROUP_WIDTH),
            pl.BlockSpec((None, tm, PLE_DIM), lambda i: (layer, i, 0)),
            _const_spec(wo.shape), _const_spec((1, d)),
            _const_spec(wg.shape), _const_spec(wu.shape), _const_spec(wd.shape),
            _const_spec((1, d)), _const_spec(wpg.shape), _const_spec(wpu.shape),
            _const_spec((1, d)),
        ],
        out_specs=row(d),
        out_shape=jax.ShapeDtypeStruct((t, d), F32),
        compiler_params=pltpu.CompilerParams(
            dimension_semantics=("parallel",), vmem_limit_bytes=VMEM_LIMIT),
        name="post_mixer",
    )(h, ma, mb, p_all, wo, nf.reshape(1, d), wg, wu, wd, npl.reshape(1, d), wpg, wpu, nfin.reshape(1, d))


def _causal_conv(xbuf_ref, x, w_ref, ts):
    xcat = jnp.concatenate([xbuf_ref[0:HALO, :], x], axis=0)
    acc = w_ref[CONV_TAPS - 1:CONV_TAPS, :] * x
    for back in range(1, CONV_TAPS):
        shifted = pltpu.roll(xcat, back, 0)[HALO:HALO + ts, :]
        acc = acc + w_ref[CONV_TAPS - 1 - back:CONV_TAPS - back, :] * shifted
    xbuf_ref[0:HALO, :] = x[ts - HALO:ts, :]
    return acc


def _proj_rglru_kernel(tm, tiles_per_seq, h_ref, nw_ref, wa_ref, wb_ref, cw_ref, cb_ref, wri_ref, bri_ref,
                       lam_ref, out_ref, zb_ref, xbuf_ref, hc_ref):
    w = GROUP_WIDTH
    first = (pl.program_id(0) % tiles_per_seq) == 0

    @pl.when(first)
    def _():
        xbuf_ref[0:HALO, :] = jnp.zeros((HALO, w), F32)
        hc_ref[...] = jnp.zeros_like(hc_ref)

    xn = _rms(h_ref[...], nw_ref[...]).astype(BF16)
    z_rg = jnp.dot(xn, wa_ref[...], preferred_element_type=F32)
    xa = z_rg[:, 0:w]
    ya = z_rg[:, w:2 * w]
    x = _causal_conv(xbuf_ref, xa, cw_ref, tm) + cb_ref[...]
    ri = jnp.dot(x.astype(BF16), wri_ref[...], preferred_element_type=F32) + bri_ref[...]
    zb_ref[...] = jnp.dot(xn, wb_ref[...], preferred_element_type=F32)
    sp_lam = _softplus(-lam_ref[...])
    lanes = HEAD128
    row = _iota2((RG_ROWS, lanes), 0)
    for l0 in range(0, w, lanes):
        ls = slice(l0, l0 + lanes)
        carry = hc_ref[:, ls]
        for r0 in range(0, tm, RG_ROWS):
            rs = slice(r0, r0 + RG_ROWS)
            r = _sigmoid(ri[rs, l0:l0 + lanes])
            i = _sigmoid(ri[rs, w + l0:w + l0 + lanes])
            log_a = -RG_C * r * sp_lam[:, ls]
            a = jnp.exp(log_a)
            mult = jnp.sqrt(-jnp.tanh(log_a) * (a * a + 1.0))
            if r0 == 0:
                mult = jnp.where((row == 0) & first, 1.0, mult)
            acc_a, acc_h = a, mult * (i * x[rs, ls])
            d = 1
            while d < RG_ROWS:
                if d % HALO == 0:
                    sh_a = jnp.concatenate([jnp.ones((d, lanes), F32), acc_a[0:RG_ROWS - d, :]], axis=0)
                    sh_h = jnp.concatenate([jnp.zeros((d, lanes), F32), acc_h[0:RG_ROWS - d, :]], axis=0)
                else:
                    sh_a = jnp.where(row >= d, pltpu.roll(acc_a, d, 0), 1.0)
                    sh_h = jnp.where(row >= d, pltpu.roll(acc_h, d, 0), 0.0)
                acc_h = acc_a * sh_h + acc_h
                acc_a = acc_a * sh_a
                d *= 2
            h = acc_h + acc_a * carry
            carry = h[RG_ROWS - 1:RG_ROWS, :]
            out_ref[rs, ls] = (h * _gelu_tanh(ya[rs, ls])).astype(out_ref.dtype)
        hc_ref[:, ls] = carry


def _proj_rglru(h, norm_w, wa, wb, cw, cb, wri, bri, lam, tm, seq_len):
    t, d = h.shape
    w = GROUP_WIDTH
    fb = wb.shape[1]
    return pl.pallas_call(
        functools.partial(_proj_rglru_kernel, tm, seq_len // tm),
        grid=(t // tm,),
        in_specs=[
            pl.BlockSpec((tm, d), lambda i: (i, 0)),
            _const_spec((1, d)), _const_spec(wa.shape), _const_spec(wb.shape),
            _const_spec((CONV_TAPS, w)), _const_spec((1, w)),
            _const_spec((w, 2 * w)), _const_spec((1, 2 * w)), _const_spec((1, w)),
        ],
        out_specs=[
            pl.BlockSpec((tm, w), lambda i: (i, 0)),
            pl.BlockSpec((tm, fb), lambda i: (i, 0)),
        ],
        out_shape=[jax.ShapeDtypeStruct((t, w), BF16), jax.ShapeDtypeStruct((t, fb), F32)],
        scratch_shapes=[pltpu.VMEM((HALO + tm, w), F32), pltpu.VMEM((1, w), F32)],
        compiler_params=pltpu.CompilerParams(
            dimension_semantics=("arbitrary",), vmem_limit_bytes=VMEM_LIMIT),
        name="proj_rglru",
    )(h, norm_w.reshape(1, d), wa, wb, cw, cb, wri, bri, lam)


def _gdn_kernel(ts, z_ref, cw_ref, alog_ref, dtb_ref, nw_ref, out_ref, xbuf_ref, st_ref):
    w = GROUP_WIDTH
    hd = HEAD128
    nh = w // hd
    cs = GDN_CHUNK
    nchunk = ts // cs
    tstep = pl.program_id(1)

    @pl.when(tstep == 0)
    def _():
        xbuf_ref[0:HALO, :] = jnp.zeros((HALO, 3 * w), F32)
        st_ref[...] = jnp.zeros_like(st_ref)

    qkv = _silu(_causal_conv(xbuf_ref, z_ref[0, :, 0:3 * w], cw_ref, ts))
    seg_w = 2 * hd
    ones_bd = _block_ones(seg_w, hd)

    def head_sum(x):
        xb = x.astype(BF16)
        return jnp.concatenate(
            [jnp.dot(xb[:, c0:c0 + seg_w], ones_bd, preferred_element_type=F32)
             for c0 in range(0, x.shape[1], seg_w)], axis=1)

    qk = qkv[:, 0:2 * w]
    qk = qk * lax.rsqrt(head_sum(qk * qk) + L2_EPS)
    zg = z_ref[0, :, 3 * w:4 * w]
    ab = z_ref[0, :, 4 * w:4 * w + hd]
    g_n = -jnp.exp(alog_ref[...]) * _softplus(ab + dtb_ref[...])
    gc_n = _sel_mm(_block_tri(ts, cs, "incl"), g_n)
    src = _iota2((hd, w), 0)
    dst_head = _iota2((hd, w), 1) // hd
    gc_all = _mm_sel(gc_n, jnp.where(src == dst_head, 1.0, 0.0).astype(BF16), terms=3)
    beta = _mm_sel(_sigmoid(ab), jnp.where(src == dst_head + nh, 1.0, 0.0).astype(BF16), terms=3)
    gs_all = _chunk_last(gc_all, cs) - gc_all
    eg_all = jnp.exp(gc_all)
    scale = hd ** -0.5
    tt = _iota2((cs, cs), 0)
    ss = _iota2((cs, cs), 1)

    items = [(h, c) for c in range(nchunk) for h in range(nh)]
    qs, ks, vbs, kbegs, decs, kdts, glast = [], [], [], [], [], [], []
    for h, c in items:
        lo = h * hd
        rows = slice(c * cs, (c + 1) * cs)
        hs = slice(lo, lo + hd)
        qh = qk[rows, lo:lo + hd] * scale
        kh = qk[rows, w + lo:w + lo + hd]
        vh = qkv[rows, 2 * w + lo:2 * w + lo + hd]
        gc = gc_all[rows, hs]
        bt = beta[rows, hs]
        eg = eg_all[rows, hs]
        kb = kh * bt
        diff = gc - gc.T
        decs.append(jnp.exp(jnp.where(ss <= tt, diff, -jnp.inf)))
        qs.append(qh * eg)
        ks.append((kh, kb, qh))
        vbs.append(vh * bt)
        kbegs.append(kb * eg)
        kdts.append((kh * jnp.exp(gs_all[rows, hs])).T)
        glast.append(jnp.exp(gc_all[(c + 1) * cs - 1:(c + 1) * cs, hs]))
    a_mats = [jnp.where(ss < tt, _mm_nt(kb, kh) * dec, 0.0) for (kh, kb, _), dec in zip(ks, decs)]
    qks = [_mm_nt(qh, kh) * dec for (kh, _, qh), dec in zip(ks, decs)]
    tinvs = _nilpotent_inverses([-a for a in a_mats], cs, refine=True)
    sols = [_mm(t, jnp.concatenate([vb, kbeg], axis=1)) for t, vb, kbeg in zip(tinvs, vbs, kbegs)]
    us = [sol[:, 0:hd] for sol in sols]
    ws = [sol[:, hd:2 * hd] for sol in sols]

    states = [st_ref[h] for h in range(nh)]
    outs = [[] for _ in range(nh)]
    for c in range(nchunk):
        idx = [c * nh + h for h in range(nh)]
        proj = [_mm(jnp.concatenate([ws[i], qs[i]], axis=0), states[h]) for h, i in enumerate(idx)]
        v_new = [us[i] - pr[0:cs] for i, pr in zip(idx, proj)]
        o = [pr[cs:2 * cs] + _mm(qks[i], vn) for i, pr, vn in zip(idx, proj, v_new)]
        states = [glast[i] * states[h] + _mm(kdts[i], vn) for (h, i), vn in zip(enumerate(idx), v_new)]
        for h in range(nh):
            outs[h].append(o[h])
    for h in range(nh):
        lo = h * hd
        st_ref[h] = states[h]
        o = jnp.concatenate(outs[h], axis=0) if nchunk > 1 else outs[h][0]
        o = _rms(o, nw_ref[...]) * _silu(zg[:, lo:lo + hd])
        out_ref[0, :, lo:lo + hd] = o.astype(out_ref.dtype)


def _gdn(z, cw, alog, dtb, nw, ts):
    b, s, _ = z.shape
    w = GROUP_WIDTH
    nh = w // HEAD128
    return pl.pallas_call(
        functools.partial(_gdn_kernel, ts),
        grid=(b, s // ts),
        in_specs=[
            pl.BlockSpec((1, ts, 4 * w + HEAD128), lambda bi, ti: (bi, ti, 0)),
            _const_spec((CONV_TAPS, 3 * w)), _const_spec((1, HEAD128)), _const_spec((1, HEAD128)),
            _const_spec((1, HEAD128)),
        ],
        out_specs=pl.BlockSpec((1, ts, w), lambda bi, ti: (bi, ti, 0)),
        out_shape=jax.ShapeDtypeStruct((b, s, w), BF16),
        scratch_shapes=[pltpu.VMEM((HALO + ts, 3 * w), F32), pltpu.VMEM((nh, HEAD128, HEAD128), F32)],
        compiler_params=pltpu.CompilerParams(
            dimension_semantics=("parallel", "arbitrary"), vmem_limit_bytes=VMEM_LIMIT),
        name="gated_deltanet",
    )(z, cw, alog, dtb, nw)


def _hgrn2_kernel(ts, layer_j, z_ref, lbraw_ref, nw_ref, out_ref, st_ref, o_ref):
    w = GROUP_WIDTH
    hd = HEAD128
    nh = w // hd
    tstep = pl.program_id(1)

    @pl.when(tstep == 0)
    def _():
        st_ref[...] = jnp.zeros_like(st_ref)

    lbraw = lbraw_ref[...]
    e = jnp.exp(lbraw - jnp.max(lbraw, axis=0, keepdims=True))
    soft = e / jnp.sum(e, axis=0, keepdims=True)
    lb = jnp.sum(soft[0:layer_j + 1, :], axis=0, keepdims=True) - soft[0:1, :]
    lb = jnp.maximum(lb, 0.0)

    q = _silu(z_ref[0, :, 0:w])
    f = z_ref[0, :, w:2 * w]
    v = z_ref[0, :, 2 * w:3 * w]
    gate = z_ref[0, :, 3 * w:4 * w]
    kin = (1.0 - lb) * _sigmoid(-f)
    la = jnp.log(lb)
    lbm = jnp.log1p(-lb) - _softplus(-f)
    mx = jnp.maximum(la, lbm)
    log_f = mx + jnp.log(1.0 + jnp.exp(-jnp.abs(la - lbm)))

    cs = HG_CHUNK
    nchunk = ts // cs
    b = _sel_mm(_block_tri(ts, cs, "incl"), log_f)
    b2 = b * math.log2(math.e)
    row = _iota2((ts, w), 0)
    tt = _iota2((cs, cs), 0)
    ss = _iota2((cs, cs), 1)
    items = [(c, h) for c in range(nchunk) for h in range(nh)]
    sl = lambda c, h: (slice(c * cs, (c + 1) * cs), slice(h * hd, (h + 1) * hd))

    scores = [jnp.where(tt == ss, _mm_nt(q[sl(c, h)], kin[sl(c, h)]), 0.0) for c, h in items]
    block_end = b2
    m = 1
    while m < cs:
        prev_end = pltpu.roll(block_end, m, 0)
        q_m = q * jnp.exp2(b2 - prev_end)
        k_m = kin * jnp.exp2(block_end - b2)
        pair = ((tt // m) % 2 == 1) & ((ss // m) == (tt // m) - 1)
        scores = [jnp.where(pair, _mm_nt(q_m[sl(c, h)], k_m[sl(c, h)]), sc)
                  for sc, (c, h) in zip(scores, items)]
        block_end = jnp.where((row // m) % 2 == 0, pltpu.roll(block_end, ts - m, 0), block_end)
        m *= 2
    o_intra = [_mm(sc, v[sl(c, h)]) for sc, (c, h) in zip(scores, items)]

    btot = _chunk_last(b, cs)
    qe = q * jnp.exp(b)
    k_dec = kin * jnp.exp(btot - b)
    dec = jnp.exp(btot)
    incs = [_mm(v[sl(c, h)].T, k_dec[sl(c, h)]) for c, h in items]
    states = [st_ref[h] for h in range(nh)]
    for c in range(nchunk):
        for h in range(nh):
            i = c * nh + h
            o_ref[c * cs:(c + 1) * cs, h * hd:(h + 1) * hd] = o_intra[i] + _mm_nt(qe[sl(c, h)], states[h])
        states = [states[h] * dec[c * cs:c * cs + 1, h * hd:(h + 1) * hd] + incs[c * nh + h]
                  for h in range(nh)]
    for h in range(nh):
        st_ref[h] = states[h]

    out_ref[0] = (_rms(o_ref[...], nw_ref[...]) * _sigmoid(gate)).astype(out_ref.dtype)


def _hgrn2(z, lbraw, nw, layer_j, ts):
    b, s, _ = z.shape
    w = GROUP_WIDTH
    nh = w // HEAD128
    return pl.pallas_call(
        functools.partial(_hgrn2_kernel, ts, layer_j),
        grid=(b, s // ts),
        in_specs=[
            pl.BlockSpec((1, ts, 4 * w), lambda bi, ti: (bi, ti, 0)),
            _const_spec(lbraw.shape), _const_spec((1, w)),
        ],
        out_specs=pl.BlockSpec((1, ts, w), lambda bi, ti: (bi, ti, 0)),
        out_shape=jax.ShapeDtypeStruct((b, s, w), BF16),
        scratch_shapes=[pltpu.VMEM((nh, HEAD128, HEAD128), F32), pltpu.VMEM((ts, w), F32)],
        compiler_params=pltpu.CompilerParams(
            dimension_semantics=("parallel", "arbitrary"), vmem_limit_bytes=VMEM_LIMIT),
        name="hgrn2",
    )(z, lbraw, nw)


def _rwkv7_kernel(ts, z_ref, mu_ref, w0_ref, w2_ref, a0_ref, a2_ref, g2_ref, kk_ref, ka_ref, rk_ref,
                  lnw_ref, lnb_ref, out_ref, xbuf_ref, st_ref):
    w = GROUP_WIDTH
    lp = R7_LORA_PAD
    ch = R7_CHUNK
    npair = w // (2 * R7_N)
    tstep = pl.program_id(1)

    @pl.when(tstep == 0)
    def _():
        xbuf_ref[0:HALO, :] = jnp.zeros((HALO, R7_IN_PAD), F32)
        st_ref[...] = jnp.zeros_like(st_ref)

    pw = 2 * R7_N
    n = R7_N
    gch = min(R7_GROUP, ts // ch)
    grows = gch * ch
    lane = _iota2((ch, pw), 1)
    in_h0 = lane < n
    t_row = _iota2((ch, pw), 0)
    s_col = lane % n
    seg_w = 2 * HEAD128
    ones_bd = _block_ones(seg_w, R7_N)

    def head_sum(x):
        xb = x.astype(BF16)
        return jnp.concatenate(
            [jnp.dot(xb[:, c0:c0 + seg_w], ones_bd, preferred_element_type=F32) for c0 in range(0, w, seg_w)],
            axis=1)

    tri_incl = _block_tri(grows, ch, "incl")
    items = [(c, p) for c in range(gch) for p in range(npair)]
    sl = lambda c, p: (slice(c * ch, (c + 1) * ch), slice(p * pw, (p + 1) * pw))

    def stack_heads(x):
        return jnp.concatenate([jnp.where(in_h0, x, 0.0), jnp.where(in_h0, 0.0, x)], axis=0)

    z_all = z_ref[0]
    z_prev_all = pltpu.roll(jnp.concatenate([xbuf_ref[0:HALO, :], z_all], axis=0), 1, 0)[HALO:HALO + ts, :]
    states = [st_ref[p] for p in range(npair)]
    for g0 in range(0, ts, grows):
        z = z_all[g0:g0 + grows, :]
        z_prev = z_prev_all[g0:g0 + grows, :]
        zd = z + mu_ref[...] * (z_prev - z)
        r = zd[:, 0:w]
        k = zd[:, w:2 * w]
        v = zd[:, 2 * w:3 * w]
        w_l = zd[:, 3 * w:3 * w + lp]
        a_l = zd[:, 3 * w + lp:3 * w + 2 * lp]
        g_l = zd[:, 3 * w + 2 * lp:3 * w + 3 * lp]
        log_w = -jnp.exp(-_softplus(-(w0_ref[...] + _mm(jnp.tanh(w_l), w2_ref[...]))) - 0.5)
        a = _sigmoid(a0_ref[...] + _mm(a_l, a2_ref[...]))
        gate = _mm(_sigmoid(g_l), g2_ref[...])
        kk = k * kk_ref[...]
        kk = kk * lax.rsqrt(head_sum(kk * kk) + L2_EPS)
        k = k * (1.0 + (a - 1.0) * ka_ref[...])
        cum = _sel_mm(tri_incl, log_w)
        total = _chunk_last(cum, ch)
        e_inv = jnp.exp(-cum)
        r_t = r * jnp.exp(cum)
        a_t = -kk * jnp.exp(cum - log_w)
        kka = kk * a
        b_h = kka * e_inv
        k_h = k * e_inv
        e_suf = jnp.exp(total - cum)
        b_dc = kka * e_suf
        k_dc = k * e_suf
        d_end = jnp.exp(total)

        v_ss = [stack_heads(v[sl(c, p)]) for c, p in items]
        grams = [
            _mm_nt(jnp.concatenate([a_t[sl(c, p)], r_t[sl(c, p)]], axis=0),
                   jnp.concatenate([stack_heads(b_h[sl(c, p)]), stack_heads(k_h[sl(c, p)])], axis=0))
            for c, p in items]
        a_abs = [jnp.where(s_col < t_row, g[0:ch, 0:pw], 0.0) for g in grams]
        a_rbs = [jnp.where(s_col <= t_row, g[ch:2 * ch, 0:pw], 0.0) for g in grams]
        a_k = [jnp.concatenate([jnp.where(s_col < t_row, g[0:ch, pw:2 * pw], 0.0),
                                jnp.where(s_col <= t_row, g[ch:2 * ch, pw:2 * pw], 0.0)], axis=0)
               for g in grams]
        eye_cat = jnp.where(s_col == t_row, 1.0, 0.0).astype(F32)
        t_cats = [eye_cat + x for x in a_abs]
        p_cats = [jnp.dot(p.astype(BF16), stack_heads(p).astype(BF16), preferred_element_type=F32)
                  for p in a_abs]
        k_pow = 2
        while 2 * k_pow < ch:
            both = [jnp.dot(jnp.concatenate([p, t], axis=0).astype(BF16), stack_heads(p).astype(BF16),
                            preferred_element_type=F32) for p, t in zip(p_cats, t_cats)]
            p_cats = [x[0:ch] for x in both]
            t_cats = [t + x[ch:2 * ch] for t, x in zip(t_cats, both)]
            k_pow *= 2
        t_cats = [t + jnp.dot(t.astype(BF16), stack_heads(p).astype(BF16), preferred_element_type=F32)
                  for p, t in zip(p_cats, t_cats)]
        tinvs = [stack_heads(t) for t in t_cats]
        kv = [_mm(x, vs) for x, vs in zip(a_k, v_ss)]
        w_ss = [_mm(t, jnp.concatenate([stack_heads(a_t[sl(c, p)]), stack_heads(x[0:ch])], axis=1))
                for t, x, (c, p) in zip(tinvs, kv, items)]
        ry = [_mm(x, w_s) for x, w_s in zip(a_rbs, w_ss)]
        r_eff = [r_t[sl(c, p)] + x[:, 0:pw] for x, (c, p) in zip(ry, items)]
        y0 = [x[:, pw:2 * pw] + y[ch:2 * ch] for x, y in zip(ry, kv)]
        bdc_ss = [stack_heads(b_dc[sl(c, p)]) for c, p in items]
        phi_t = [_mm(w_s[:, 0:pw].T, bd).astype(BF16) for w_s, bd in zip(w_ss, bdc_ss)]
        gamma_t = [_mm(jnp.concatenate([w_s[:, pw:2 * pw].T, vs.T], axis=1),
                       jnp.concatenate([bd, stack_heads(k_dc[sl(c, p)])], axis=0))
                   for w_s, vs, bd, (c, p) in zip(w_ss, v_ss, bdc_ss, items)]

        ys = [[] for _ in range(npair)]
        for c in range(gch):
            st_bs = [st.astype(BF16) for st in states]
            states = [states[p] * d_end[c * ch:c * ch + 1, p * pw:(p + 1) * pw]
                      + jnp.dot(st_bs[p], phi_t[c * npair + p], preferred_element_type=F32)
                      + gamma_t[c * npair + p] for p in range(npair)]
            for p in range(npair):
                i = c * npair + p
                ys[p].append(_mm_nt(r_eff[i], st_bs[p]) + y0[i])
        y = jnp.concatenate([jnp.concatenate(yp, axis=0) if gch > 1 else yp[0] for yp in ys], axis=1)

        mean = head_sum(y) * (1.0 / R7_N)
        yc = y - mean
        var = head_sum(yc * yc) * (1.0 / R7_N)
        yn = yc * lax.rsqrt(var + GN_EPS) * lnw_ref[...] + lnb_ref[...]
        bonus = head_sum(r * k * rk_ref[...]) * v
        out_ref[0, g0:g0 + grows, :] = ((yn + bonus) * gate).astype(out_ref.dtype)
    for p in range(npair):
        st_ref[p] = states[p]
    xbuf_ref[0:HALO, :] = z_all[ts - HALO:ts, :]


def _rwkv7(z, mu, w0, w2, a0, a2, g2, k_k, k_a, r_k, ln_w, ln_b, ts):
    b, s, fin = z.shape
    w = GROUP_WIDTH
    npair = w // (2 * R7_N)
    vec = _const_spec((1, w))
    return pl.pallas_call(
        functools.partial(_rwkv7_kernel, ts),
        grid=(b, s // ts),
        in_specs=[
            pl.BlockSpec((1, ts, fin), lambda bi, ti: (bi, ti, 0)),
            _const_spec((1, fin)), vec, _const_spec(w2.shape), vec, _const_spec(a2.shape),
            _const_spec(g2.shape), vec, vec, vec, vec, vec,
        ],
        out_specs=pl.BlockSpec((1, ts, w), lambda bi, ti: (bi, ti, 0)),
        out_shape=jax.ShapeDtypeStruct((b, s, w), BF16),
        scratch_shapes=[pltpu.VMEM((HALO + ts, fin), F32), pltpu.VMEM((npair, 2 * R7_N, 2 * R7_N), F32)],
        compiler_params=pltpu.CompilerParams(
            dimension_semantics=("parallel", "arbitrary"), vmem_limit_bytes=VMEM_LIMIT),
        name="rwkv7",
    )(z, mu, w0, w2, a0, a2, g2, k_k, k_a, r_k, ln_w, ln_b)


def _even_weights(w_in, rg_w_r, rg_w_i, rg_b_r, rg_b_i, gdn_a_log, gdn_dt_bias):
    w = GROUP_WIDTH
    nh = w // HEAD128
    w_rg = w_in[:, 0:2 * w]
    w_gdn = jnp.pad(w_in[:, 2 * w:6 * w + 2 * nh], ((0, 0), (0, HEAD128 - 2 * nh)))
    wri = jnp.concatenate([jax.scipy.linalg.block_diag(*rg_w_r), jax.scipy.linalg.block_diag(*rg_w_i)], axis=1)
    bri = jnp.concatenate([rg_b_r, rg_b_i]).reshape(1, 2 * w)
    alog = jnp.pad(gdn_a_log, (0, HEAD128 - nh)).reshape(1, HEAD128)
    dtb = jnp.pad(gdn_dt_bias, (0, HEAD128 - nh)).reshape(1, HEAD128)
    return w_rg.astype(BF16), w_gdn.astype(BF16), wri.astype(BF16), bri, alog, dtb


def _pad_groups(x, axis):
    w = GROUP_WIDTH
    sizes = (w, w, w, 64, 64, 128)
    parts = []
    start = 0
    for sz in sizes:
        piece = lax.slice_in_dim(x, start, start + sz, axis=axis)
        pad = (-sz) % R7_LORA_PAD
        if pad:
            cfg = [(0, 0)] * x.ndim
            cfg[axis] = (0, pad)
            piece = jnp.pad(piece, cfg)
        parts.append(piece)
        start += sz
    return jnp.concatenate(parts, axis=axis)


def _pad_rows(x, rows):
    return jnp.pad(x, ((0, rows - x.shape[0]), (0, 0)))


def _tiles(b, s):
    t = b * s
    tm_proj = 512 if s % 512 == 0 else 256
    tm_post = 256
    ts = 256 if s % 256 == 0 else 128
    return tm_proj, tm_post, ts


def kernel(x, p, norm_mix, norm_ffn, norm_ple, norm_final, e_w_in, rg_conv_w, rg_conv_b, rg_w_r, rg_b_r, rg_w_i, rg_b_i, rg_lambda, gdn_conv_w, gdn_a_log, gdn_dt_bias, gdn_norm_w, e_w_out, o_w_in, hg_lower_bounds, hg_norm_w, r7_mu, r7_w0, r7_w2, r7_a0, r7_a2, r7_g2, r7_k_k, r7_k_a, r7_r_k, r7_ln_w, r7_ln_b, o_w_out, ffn_w_gate, ffn_w_up, ffn_w_down, ple_w_up, ple_w_gate):
    b, s, d = x.shape
    depth = p.shape[0]
    t = b * s
    w = GROUP_WIDTH
    tm_proj, tm_post, ts = _tiles(b, s)
    h = x.reshape(t, d)
    row = lambda vec: vec.reshape(1, -1)
    for layer in range(depth):
        j = layer // 2
        if layer % 2 == 0:
            w_rg, w_gdn, wri, bri, alog, dtb = _even_weights(
                e_w_in[j], rg_w_r[j], rg_w_i[j], rg_b_r[j], rg_b_i[j], gdn_a_log[j], gdn_dt_bias[j])
            out_a, z_b = _proj_rglru(h, norm_mix[layer], w_rg, w_gdn, rg_conv_w[j], row(rg_conv_b[j]),
                                     wri, bri, row(rg_lambda[j]), tm_proj, s)
            out_b = _gdn(z_b.reshape(b, s, -1), gdn_conv_w[j], alog, dtb, row(gdn_norm_w[j]),
                         2 * ts if s % (2 * ts) == 0 else ts)
            w_out = e_w_out[j]
        else:
            hg_in = 4 * w
            w_hg = o_w_in[j][:, 0:hg_in].astype(BF16)
            w_r7 = _pad_groups(o_w_in[j][:, hg_in:], axis=1).astype(BF16)
            z_a, z_b = _norm_proj(h, norm_mix[layer], w_hg, w_r7, tm_proj)
            out_a = _hgrn2(z_a.reshape(b, s, -1), hg_lower_bounds, row(hg_norm_w[j]), j, ts)
            out_b = _rwkv7(z_b.reshape(b, s, -1), row(_pad_groups(r7_mu[j], axis=0)), row(r7_w0[j]),
                           _pad_rows(r7_w2[j], R7_LORA_PAD).astype(BF16), row(r7_a0[j]),
                           _pad_rows(r7_a2[j], R7_LORA_PAD).astype(BF16), r7_g2[j].astype(BF16),
                           row(r7_k_k[j]), row(r7_k_a[j]), row(r7_r_k[j].reshape(-1)),
                           row(r7_ln_w[j]), row(r7_ln_b[j]), ts)
            w_out = o_w_out[j]
        h = _post(h, out_a.reshape(t, w), out_b.reshape(t, w), p.reshape(depth, t, PLE_DIM), layer,
                  w_out.astype(BF16), norm_ffn[layer], ffn_w_gate[layer].astype(BF16),
                  ffn_w_up[layer].astype(BF16), ffn_w_down[layer].astype(BF16), norm_ple[layer],
                  ple_w_gate[layer].astype(BF16), ple_w_up[layer].astype(BF16), norm_final,
                  layer == depth - 1, tm_post)
    return h.reshape(b, s, d)
```

```python
import functools
import math

import jax
import jax.numpy as jnp
from jax import lax
from jax.experimental import pallas as pl
from jax.experimental.pallas import tpu as pltpu

F32 = jnp.float32
BF16 = jnp.bfloat16

D_MODEL = 1024
PLE_DIM = 256
D_FF = 2816
GROUP_WIDTH = 512
RMS_EPS = 1e-6
L2_EPS = 1e-6
GN_EPS = 64e-5
RG_C = 8.0
RG_BLOCKS = 8
RG_ROWS = 64
CONV_TAPS = 4
HEAD128 = 128
R7_N = 64
R7_LORA_PAD = 128
R7_IN_PAD = 3 * GROUP_WIDTH + 3 * R7_LORA_PAD

GDN_CHUNK = 128
HG_CHUNK = 128
R7_CHUNK = 64
R7_GROUP = 4
HALO = 8

VMEM_LIMIT = 56 * 1024 * 1024


def _sigmoid(x):
    return 1.0 / (1.0 + jnp.exp(-x))


def _softplus(x):
    return jnp.maximum(x, 0.0) + jnp.log(1.0 + jnp.exp(-jnp.abs(x)))


def _silu(x):
    return x * _sigmoid(x)


def _gelu_tanh(x):
    c = math.sqrt(2.0 / math.pi)
    return 0.5 * x * (1.0 + jnp.tanh(c * (x + 0.044715 * (x * x * x))))


def _rms(x, w):
    ms = jnp.mean(x * x, axis=-1, keepdims=True)
    return x * lax.rsqrt(ms + RMS_EPS) * w


def _mm(a, b):
    return jnp.dot(a.astype(BF16), b.astype(BF16), preferred_element_type=F32)


def _mm_nt(a, b):
    return lax.dot_general(a.astype(BF16), b.astype(BF16), (((1,), (1,)), ((), ())),
                           preferred_element_type=F32)


def _split(x, terms):
    parts = []
    r = x
    for _ in range(terms):
        hi = r.astype(BF16)
        parts.append(hi)
        r = r - hi.astype(F32)
    return parts


def _sel_mm(sel, x, terms=3):
    acc = None
    for part in _split(x, terms):
        d = jnp.dot(sel, part, preferred_element_type=F32)
        acc = d if acc is None else acc + d
    return acc


def _mm_sel(x, sel, terms=2):
    acc = None
    for part in _split(x, terms):
        d = jnp.dot(part, sel, preferred_element_type=F32)
        acc = d if acc is None else acc + d
    return acc


def _mm_hi(a, b):
    a_hi, a_lo = _split(a, 2)
    b_hi, b_lo = _split(b, 2)
    return (jnp.dot(a_hi, b_hi, preferred_element_type=F32)
            + jnp.dot(a_lo, b_hi, preferred_element_type=F32)
            + jnp.dot(a_hi, b_lo, preferred_element_type=F32))


def _iota2(shape, axis):
    return lax.broadcasted_iota(jnp.int32, shape, axis)


def _block_tri(n, block, kind):
    t = _iota2((n, n), 0)
    s = _iota2((n, n), 1)
    same = (t // block) == (s // block)
    cond = (s <= t) if kind == "incl" else (s > t)
    return jnp.where(same & cond, 1.0, 0.0).astype(BF16)


def _chunk_last(x, chunk):
    cols = x.shape[1]
    return jnp.concatenate(
        [jnp.broadcast_to(x[r + chunk - 1:r + chunk, :], (chunk, cols)) for r in range(0, x.shape[0], chunk)],
        axis=0)


def _block_ones(n, block):
    t = _iota2((n, n), 0)
    s = _iota2((n, n), 1)
    return jnp.where((t // block) == (s // block), 1.0, 0.0).astype(BF16)


def _nilpotent_inverses(xs, n, refine):
    shape = xs[0].shape
    eye = jnp.where(_iota2(shape, 0) == _iota2(shape, 1), 1.0, 0.0).astype(F32)
    ts = [eye + x for x in xs]
    ps = list(xs)
    k = 2
    while k < n:
        ps = [_mm(p, p) for p in ps]
        ts = [t + _mm(t, p) for t, p in zip(ts, ps)]
        k *= 2
    if refine:
        res = [(eye - t) + _mm_hi(x, t) for x, t in zip(xs, ts)]
        ts = [t + _mm(t, r) for t, r in zip(ts, res)]
    return ts


def _norm_proj_kernel(h_ref, nw_ref, wa_ref, wb_ref, za_ref, zb_ref):
    xn = _rms(h_ref[...], nw_ref[...]).astype(BF16)
    za_ref[...] = jnp.dot(xn, wa_ref[...], preferred_element_type=F32)
    zb_ref[...] = jnp.dot(xn, wb_ref[...], preferred_element_type=F32)


def _const_spec(shape):
    nd = len(shape)
    return pl.BlockSpec(shape, lambda *_: (0,) * nd, pipeline_mode=pl.Buffered(1))


def _norm_proj(h, norm_w, wa, wb, tm):
    t, d = h.shape
    fa, fb = wa.shape[1], wb.shape[1]
    return pl.pallas_call(
        _norm_proj_kernel,
        grid=(t // tm,),
        in_specs=[
            pl.BlockSpec((tm, d), lambda i: (i, 0)),
            _const_spec((1, d)),
            _const_spec((d, fa)),
            _const_spec((d, fb)),
        ],
        out_specs=[
            pl.BlockSpec((tm, fa), lambda i: (i, 0)),
            pl.BlockSpec((tm, fb), lambda i: (i, 0)),
        ],
        out_shape=[jax.ShapeDtypeStruct((t, fa), F32), jax.ShapeDtypeStruct((t, fb), F32)],
        compiler_params=pltpu.CompilerParams(
            dimension_semantics=("parallel",), vmem_limit_bytes=VMEM_LIMIT),
        name="norm_proj",
    )(h, norm_w.reshape(1, d), wa, wb)


def _post_kernel(final_norm, h_ref, ma_ref, mb_ref, p_ref, wo_ref, nf_ref, wg_ref, wu_ref, wd_ref,
                 npl_ref, wpg_ref, wpu_ref, nfin_ref, out_ref):
    h = h_ref[...]
    mix = (jnp.dot(ma_ref[...], wo_ref[0:GROUP_WIDTH, :], preferred_element_type=F32)
           + jnp.dot(mb_ref[...], wo_ref[GROUP_WIDTH:2 * GROUP_WIDTH, :], preferred_element_type=F32))
    h1 = h + mix
    hn = _rms(h1, nf_ref[...]).astype(BF16)
    g = jnp.dot(hn, wg_ref[...], preferred_element_type=F32)
    u = jnp.dot(hn, wu_ref[...], preferred_element_type=F32)
    act = (_silu(g) * u).astype(BF16)
    h2 = h1 + jnp.dot(act, wd_ref[...], preferred_element_type=F32)
    hp = _rms(h2, npl_ref[...]).astype(BF16)
    gate = _sigmoid(jnp.dot(hp, wpg_ref[...], preferred_element_type=F32))
    up = jnp.dot(p_ref[...].astype(BF16), wpu_ref[...], preferred_element_type=F32)
    h3 = h2 + gate * up
    if final_norm:
        h3 = _rms(h3, nfin_ref[...])
    out_ref[...] = h3


def _post(h, ma, mb, p_all, layer, wo, nf, wg, wu, wd, npl, wpg, wpu, nfin, final_norm, tm):
    t, d = h.shape
    row = lambda w: pl.BlockSpec((tm, w), lambda i: (i, 0))
    return pl.pallas_call(
        functools.partial(_post_kernel, final_norm),
        grid=(t // tm,),
        in_specs=[
            row(d), row(GROUP_WIDTH), row(GROUP_WIDTH),
            pl.BlockSpec((None, tm, PLE_DIM), lambda i: (layer, i, 0)),
            _const_spec(wo.shape), _const_spec((1, d)),
            _const_spec(wg.shape), _const_spec(wu.shape), _const_spec(wd.shape),
            _const_spec((1, d)), _const_spec(wpg.shape), _const_spec(wpu.shape),
            _const_spec((1, d)),
        ],
        out_specs=row(d),
        out_shape=jax.ShapeDtypeStruct((t, d), F32),
        compiler_params=pltpu.CompilerParams(
            dimension_semantics=("parallel",), vmem_limit_bytes=VMEM_LIMIT),
        name="post_mixer",
    )(h, ma, mb, p_all, wo, nf.reshape(1, d), wg, wu, wd, npl.reshape(1, d), wpg, wpu, nfin.reshape(1, d))


def _causal_conv(xbuf_ref, x, w_ref, ts):
    xcat = jnp.concatenate([xbuf_ref[0:HALO, :], x], axis=0)
    acc = w_ref[CONV_TAPS - 1:CONV_TAPS, :] * x
    for back in range(1, CONV_TAPS):
        shifted = pltpu.roll(xcat, back, 0)[HALO:HALO + ts, :]
        acc = acc + w_ref[CONV_TAPS - 1 - back:CONV_TAPS - back, :] * shifted
    xbuf_ref[0:HALO, :] = x[ts - HALO:ts, :]
    return acc


def _proj_rglru_kernel(tm, tiles_per_seq, h_ref, nw_ref, wa_ref, wb_ref, cw_ref, cb_ref, wri_ref, bri_ref,
                       lam_ref, out_ref, zb_ref, xbuf_ref, hc_ref):
    w = GROUP_WIDTH
    first = (pl.program_id(0) % tiles_per_seq) == 0

    @pl.when(first)
    def _():
        xbuf_ref[0:HALO, :] = jnp.zeros((HALO, w), F32)
        hc_ref[...] = jnp.zeros_like(hc_ref)

    xn = _rms(h_ref[...], nw_ref[...]).astype(BF16)
    z_rg = jnp.dot(xn, wa_ref[...], preferred_element_type=F32)
    xa = z_rg[:, 0:w]
    ya = z_rg[:, w:2 * w]
    x = _causal_conv(xbuf_ref, xa, cw_ref, tm) + cb_ref[...]
    ri = jnp.dot(x.astype(BF16), wri_ref[...], preferred_element_type=F32) + bri_ref[...]
    zb_ref[...] = jnp.dot(xn, wb_ref[...], preferred_element_type=F32)
    sp_lam = _softplus(-lam_ref[...])
    lanes = HEAD128
    row = _iota2((RG_ROWS, lanes), 0)
    for l0 in range(0, w, lanes):
        ls = slice(l0, l0 + lanes)
        carry = hc_ref[:, ls]
        for r0 in range(0, tm, RG_ROWS):
            rs = slice(r0, r0 + RG_ROWS)
            r = _sigmoid(ri[rs, l0:l0 + lanes])
            i = _sigmoid(ri[rs, w + l0:w + l0 + lanes])
            log_a = -RG_C * r * sp_lam[:, ls]
            a = jnp.exp(log_a)
            mult = jnp.sqrt(-jnp.tanh(log_a) * (a * a + 1.0))
            if r0 == 0:
                mult = jnp.where((row == 0) & first, 1.0, mult)
            acc_a, acc_h = a, mult * (i * x[rs, ls])
            d = 1
            while d < RG_ROWS:
                if d % HALO == 0:
                    sh_a = jnp.concatenate([jnp.ones((d, lanes), F32), acc_a[0:RG_ROWS - d, :]], axis=0)
                    sh_h = jnp.concatenate([jnp.zeros((d, lanes), F32), acc_h[0:RG_ROWS - d, :]], axis=0)
                else:
                    sh_a = jnp.where(row >= d, pltpu.roll(acc_a, d, 0), 1.0)
                    sh_h = jnp.where(row >= d, pltpu.roll(acc_h, d, 0), 0.0)
                acc_h = acc_a * sh_h + acc_h
                acc_a = acc_a * sh_a
                d *= 2
            h = acc_h + acc_a * carry
            carry = h[RG_ROWS - 1:RG_ROWS, :]
            out_ref[rs, ls] = (h * _gelu_tanh(ya[rs, ls])).astype(out_ref.dtype)
        hc_ref[:, ls] = carry


def _proj_rglru(h, norm_w, wa, wb, cw, cb, wri, bri, lam, tm, seq_len):
    t, d = h.shape
    w = GROUP_WIDTH
    fb = wb.shape[1]
    return pl.pallas_call(
        functools.partial(_proj_rglru_kernel, tm, seq_len // tm),
        grid=(t // tm,),
        in_specs=[
            pl.BlockSpec((tm, d), lambda i: (i, 0)),
            _const_spec((1, d)), _const_spec(wa.shape), _const_spec(wb.shape),
            _const_spec((CONV_TAPS, w)), _const_spec((1, w)),
            _const_spec((w, 2 * w)), _const_spec((1, 2 * w)), _const_spec((1, w)),
        ],
        out_specs=[
            pl.BlockSpec((tm, w), lambda i: (i, 0)),
            pl.BlockSpec((tm, fb), lambda i: (i, 0)),
        ],
        out_shape=[jax.ShapeDtypeStruct((t, w), BF16), jax.ShapeDtypeStruct((t, fb), F32)],
        scratch_shapes=[pltpu.VMEM((HALO + tm, w), F32), pltpu.VMEM((1, w), F32)],
        compiler_params=pltpu.CompilerParams(
            dimension_semantics=("arbitrary",), vmem_limit_bytes=VMEM_LIMIT),
        name="proj_rglru",
    )(h, norm_w.reshape(1, d), wa, wb, cw, cb, wri, bri, lam)


def _gdn_kernel(ts, z_ref, cw_ref, alog_ref, dtb_ref, nw_ref, out_ref, xbuf_ref, st_ref):
    w = GROUP_WIDTH
    hd = HEAD128
    nh = w // hd
    cs = GDN_CHUNK
    nchunk = ts // cs
    tstep = pl.program_id(1)

    @pl.when(tstep == 0)
    def _():
        xbuf_ref[0:HALO, :] = jnp.zeros((HALO, 3 * w), F32)
        st_ref[...] = jnp.zeros_like(st_ref)

    qkv = _silu(_causal_conv(xbuf_ref, z_ref[0, :, 0:3 * w], cw_ref, ts))
    seg_w = 2 * hd
    ones_bd = _block_ones(seg_w, hd)

    def head_sum(x):
        xb = x.astype(BF16)
        return jnp.concatenate(
            [jnp.dot(xb[:, c0:c0 + seg_w], ones_bd, preferred_element_type=F32)
             for c0 in range(0, x.shape[1], seg_w)], axis=1)

    qk = qkv[:, 0:2 * w]
    qk = qk * lax.rsqrt(head_sum(qk * qk) + L2_EPS)
    zg = z_ref[0, :, 3 * w:4 * w]
    ab = z_ref[0, :, 4 * w:4 * w + hd]
    g_n = -jnp.exp(alog_ref[...]) * _softplus(ab + dtb_ref[...])
    gc_n = _sel_mm(_block_tri(ts, cs, "incl"), g_n)
    src = _iota2((hd, w), 0)
    dst_head = _iota2((hd, w), 1) // hd
    gc_all = _mm_sel(gc_n, jnp.where(src == dst_head, 1.0, 0.0).astype(BF16), terms=3)
    beta = _mm_sel(_sigmoid(ab), jnp.where(src == dst_head + nh, 1.0, 0.0).astype(BF16), terms=3)
    gs_all = _chunk_last(gc_all, cs) - gc_all
    eg_all = jnp.exp(gc_all)
    scale = hd ** -0.5
    tt = _iota2((cs, cs), 0)
    ss = _iota2((cs, cs), 1)

    items = [(h, c) for c in range(nchunk) for h in range(nh)]
    qs, ks, vbs, kbegs, decs, kdts, glast = [], [], [], [], [], [], []
    for h, c in items:
        lo = h * hd
        rows = slice(c * cs, (c + 1) * cs)
        hs = slice(lo, lo + hd)
        qh = qk[rows, lo:lo + hd] * scale
        kh = qk[rows, w + lo:w + lo + hd]
        vh = qkv[rows, 2 * w + lo:2 * w + lo + hd]
        gc = gc_all[rows, hs]
        bt = beta[rows, hs]
        eg = eg_all[rows, hs]
        kb = kh * bt
        diff = gc - gc.T
        decs.append(jnp.exp(jnp.where(ss <= tt, diff, -jnp.inf)))
        qs.append(qh * eg)
        ks.append((kh, kb, qh))
        vbs.append(vh * bt)
        kbegs.append(kb * eg)
        kdts.append((kh * jnp.exp(gs_all[rows, hs])).T)
        glast.append(jnp.exp(gc_all[(c + 1) * cs - 1:(c + 1) * cs, hs]))
    a_mats = [jnp.where(ss < tt, _mm_nt(kb, kh) * dec, 0.0) for (kh, kb, _), dec in zip(ks, decs)]
    qks = [_mm_nt(qh, kh) * dec for (kh, _, qh), dec in zip(ks, decs)]
    tinvs = _nilpotent_inverses([-a for a in a_mats], cs, refine=True)
    sols = [_mm(t, jnp.concatenate([vb, kbeg], axis=1)) for t, vb, kbeg in zip(tinvs, vbs, kbegs)]
    us = [sol[:, 0:hd] for sol in sols]
    ws = [sol[:, hd:2 * hd] for sol in sols]

    states = [st_ref[h] for h in range(nh)]
    outs = [[] for _ in range(nh)]
    for c in range(nchunk):
        idx = [c * nh + h for h in range(nh)]
        proj = [_mm(jnp.concatenate([ws[i], qs[i]], axis=0), states[h]) for h, i in enumerate(idx)]
        v_new = [us[i] - pr[0:cs] for i, pr in zip(idx, proj)]
        o = [pr[cs:2 * cs] + _mm(qks[i], vn) for i, pr, vn in zip(idx, proj, v_new)]
        states = [glast[i] * states[h] + _mm(kdts[i], vn) for (h, i), vn in zip(enumerate(idx), v_new)]
        for h in range(nh):
            outs[h].append(o[h])
    for h in range(nh):
        lo = h * hd
        st_ref[h] = states[h]
        o = jnp.concatenate(outs[h], axis=0) if nchunk > 1 else outs[h][0]
        o = _rms(o, nw_ref[...]) * _silu(zg[:, lo:lo + hd])
        out_ref[0, :, lo:lo + hd] = o.astype(out_ref.dtype)


def _gdn(z, cw, alog, dtb, nw, ts):
    b, s, _ = z.shape
    w = GROUP_WIDTH
    nh = w // HEAD128
    return pl.pallas_call(
        functools.partial(_gdn_kernel, ts),
        grid=(b, s // ts),
        in_specs=[
            pl.BlockSpec((1, ts, 4 * w + HEAD128), lambda bi, ti: (bi, ti, 0)),
            _const_spec((CONV_TAPS, 3 * w)), _const_spec((1, HEAD128)), _const_spec((1, HEAD128)),
            _const_spec((1, HEAD128)),
        ],
        out_specs=pl.BlockSpec((1, ts, w), lambda bi, ti: (bi, ti, 0)),
        out_shape=jax.ShapeDtypeStruct((b, s, w), BF16),
        scratch_shapes=[pltpu.VMEM((HALO + ts, 3 * w), F32), pltpu.VMEM((nh, HEAD128, HEAD128), F32)],
        compiler_params=pltpu.CompilerParams(
            dimension_semantics=("parallel", "arbitrary"), vmem_limit_bytes=VMEM_LIMIT),
        name="gated_deltanet",
    )(z, cw, alog, dtb, nw)


def _hgrn2_kernel(ts, layer_j, z_ref, lbraw_ref, nw_ref, out_ref, st_ref, o_ref):
    w = GROUP_WIDTH
    hd = HEAD128
    nh = w // hd
    tstep = pl.program_id(1)

    @pl.when(tstep == 0)
    def _():
        st_ref[...] = jnp.zeros_like(st_ref)

    lbraw = lbraw_ref[...]
    e = jnp.exp(lbraw - jnp.max(lbraw, axis=0, keepdims=True))
    soft = e / jnp.sum(e, axis=0, keepdims=True)
    lb = jnp.sum(soft[0:layer_j + 1, :], axis=0, keepdims=True) - soft[0:1, :]
    lb = jnp.maximum(lb, 0.0)

    q = _silu(z_ref[0, :, 0:w])
    f = z_ref[0, :, w:2 * w]
    v = z_ref[0, :, 2 * w:3 * w]
    gate = z_ref[0, :, 3 * w:4 * w]
    kin = (1.0 - lb) * _sigmoid(-f)
    la = jnp.log(lb)
    lbm = jnp.log1p(-lb) - _softplus(-f)
    mx = jnp.maximum(la, lbm)
    log_f = mx + jnp.log(1.0 + jnp.exp(-jnp.abs(la - lbm)))

    cs = HG_CHUNK
    nchunk = ts // cs
    b = _sel_mm(_block_tri(ts, cs, "incl"), log_f)
    b2 = b * math.log2(math.e)
    row = _iota2((ts, w), 0)
    tt = _iota2((cs, cs), 0)
    ss = _iota2((cs, cs), 1)
    items = [(c, h) for c in range(nchunk) for h in range(nh)]
    sl = lambda c, h: (slice(c * cs, (c + 1) * cs), slice(h * hd, (h + 1) * hd))

    scores = [jnp.where(tt == ss, _mm_nt(q[sl(c, h)], kin[sl(c, h)]), 0.0) for c, h in items]
    block_end = b2
    m = 1
    while m < cs:
        prev_end = pltpu.roll(block_end, m, 0)
        q_m = q * jnp.exp2(b2 - prev_end)
        k_m = kin * jnp.exp2(block_end - b2)
        pair = ((tt // m) % 2 == 1) & ((ss // m) == (tt // m) - 1)
        scores = [jnp.where(pair, _mm_nt(q_m[sl(c, h)], k_m[sl(c, h)]), sc)
                  for sc, (c, h) in zip(scores, items)]
        block_end = jnp.where((row // m) % 2 == 0, pltpu.roll(block_end, ts - m, 0), block_end)
        m *= 2
    o_intra = [_mm(sc, v[sl(c, h)]) for sc, (c, h) in zip(scores, items)]

    btot = _chunk_last(b, cs)
    qe = q * jnp.exp(b)
    k_dec = kin * jnp.exp(btot - b)
    dec = jnp.exp(btot)
    incs = [_mm(v[sl(c, h)].T, k_dec[sl(c, h)]) for c, h in items]
    states = [st_ref[h] for h in range(nh)]
    for c in range(nchunk):
        for h in range(nh):
            i = c * nh + h
            o_ref[c * cs:(c + 1) * cs, h * hd:(h + 1) * hd] = o_intra[i] + _mm_nt(qe[sl(c, h)], states[h])
        states = [states[h] * dec[c * cs:c * cs + 1, h * hd:(h + 1) * hd] + incs[c * nh + h]
                  for h in range(nh)]
    for h in range(nh):
        st_ref[h] = states[h]

    out_ref[0] = (_rms(o_ref[...], nw_ref[...]) * _sigmoid(gate)).astype(out_ref.dtype)


def _hgrn2(z, lbraw, nw, layer_j, ts):
    b, s, _ = z.shape
    w = GROUP_WIDTH
    nh = w // HEAD128
    return pl.pallas_call(
        functools.partial(_hgrn2_kernel, ts, layer_j),
        grid=(b, s // ts),
        in_specs=[
            pl.BlockSpec((1, ts, 4 * w), lambda bi, ti: (bi, ti, 0)),
            _const_spec(lbraw.shape), _const_spec((1, w)),
        ],
        out_specs=pl.BlockSpec((1, ts, w), lambda bi, ti: (bi, ti, 0)),
        out_shape=jax.ShapeDtypeStruct((b, s, w), BF16),
        scratch_shapes=[pltpu.VMEM((nh, HEAD128, HEAD128), F32), pltpu.VMEM((ts, w), F32)],
        compiler_params=pltpu.CompilerParams(
            dimension_semantics=("parallel", "arbitrary"), vmem_limit_bytes=VMEM_LIMIT),
        name="hgrn2",
    )(z, lbraw, nw)


def _rwkv7_kernel(ts, z_ref, mu_ref, w0_ref, w2_ref, a0_ref, a2_ref, g2_ref, kk_ref, ka_ref, rk_ref,
                  lnw_ref, lnb_ref, out_ref, xbuf_ref, st_ref):
    w = GROUP_WIDTH
    lp = R7_LORA_PAD
    ch = R7_CHUNK
    npair = w // (2 * R7_N)
    tstep = pl.program_id(1)

    @pl.when(tstep == 0)
    def _():
        xbuf_ref[0:HALO, :] = jnp.zeros((HALO, R7_IN_PAD), F32)
        st_ref[...] = jnp.zeros_like(st_ref)

    pw = 2 * R7_N
    n = R7_N
    gch = min(R7_GROUP, ts // ch)
    grows = gch * ch
    lane = _iota2((ch, pw), 1)
    in_h0 = lane < n
    t_row = _iota2((ch, pw), 0)
    s_col = lane % n
    seg_w = 2 * HEAD128
    ones_bd = _block_ones(seg_w, R7_N)

    def head_sum(x):
        xb = x.astype(BF16)
        return jnp.concatenate(
            [jnp.dot(xb[:, c0:c0 + seg_w], ones_bd, preferred_element_type=F32) for c0 in range(0, w, seg_w)],
            axis=1)

    tri_incl = _block_tri(grows, ch, "incl")
    items = [(c, p) for c in range(gch) for p in range(npair)]
    sl = lambda c, p: (slice(c * ch, (c + 1) * ch), slice(p * pw, (p + 1) * pw))

    def stack_heads(x):
        return jnp.concatenate([jnp.where(in_h0, x, 0.0), jnp.where(in_h0, 0.0, x)], axis=0)

    z_all = z_ref[0]
    z_prev_all = pltpu.roll(jnp.concatenate([xbuf_ref[0:HALO, :], z_all], axis=0), 1, 0)[HALO:HALO + ts, :]
    states = [st_ref[p] for p in range(npair)]
    for g0 in range(0, ts, grows):
        z = z_all[g0:g0 + grows, :]
        z_prev = z_prev_all[g0:g0 + grows, :]
        zd = z + mu_ref[...] * (z_prev - z)
        r = zd[:, 0:w]
        k = zd[:, w:2 * w]
        v = zd[:, 2 * w:3 * w]
        w_l = zd[:, 3 * w:3 * w + lp]
        a_l = zd[:, 3 * w + lp:3 * w + 2 * lp]
        g_l = zd[:, 3 * w + 2 * lp:3 * w + 3 * lp]
        log_w = -jnp.exp(-_softplus(-(w0_ref[...] + _mm(jnp.tanh(w_l), w2_ref[...]))) - 0.5)
        a = _sigmoid(a0_ref[...] + _mm(a_l, a2_ref[...]))
        gate = _mm(_sigmoid(g_l), g2_ref[...])
        kk = k * kk_ref[...]
        kk = kk * lax.rsqrt(head_sum(kk * kk) + L2_EPS)
        k = k * (1.0 + (a - 1.0) * ka_ref[...])
        cum = _sel_mm(tri_incl, log_w)
        total = _chunk_last(cum, ch)
        e_inv = jnp.exp(-cum)
        r_t = r * jnp.exp(cum)
        a_t = -kk * jnp.exp(cum - log_w)
        kka = kk * a
        b_h = kka * e_inv
        k_h = k * e_inv
        e_suf = jnp.exp(total - cum)
        b_dc = kka * e_suf
        k_dc = k * e_suf
        d_end = jnp.exp(total)

        v_ss = [stack_heads(v[sl(c, p)]) for c, p in items]
        grams = [
            _mm_nt(jnp.concatenate([a_t[sl(c, p)], r_t[sl(c, p)]], axis=0),
                   jnp.concatenate([stack_heads(b_h[sl(c, p)]), stack_heads(k_h[sl(c, p)])], axis=0))
            for c, p in items]
        a_abs = [jnp.where(s_col < t_row, g[0:ch, 0:pw], 0.0) for g in grams]
        a_rbs = [jnp.where(s_col <= t_row, g[ch:2 * ch, 0:pw], 0.0) for g in grams]
        a_k = [jnp.concatenate([jnp.where(s_col < t_row, g[0:ch, pw:2 * pw], 0.0),
                                jnp.where(s_col <= t_row, g[ch:2 * ch, pw:2 * pw], 0.0)], axis=0)
               for g in grams]
        eye_cat = jnp.where(s_col == t_row, 1.0, 0.0).astype(F32)
        t_cats = [eye_cat + x for x in a_abs]
        p_cats = [jnp.dot(p.astype(BF16), stack_heads(p).astype(BF16), preferred_element_type=F32)
                  for p in a_abs]
        k_pow = 2
        while 2 * k_pow < ch:
            both = [jnp.dot(jnp.concatenate([p, t], axis=0).astype(BF16), stack_heads(p).astype(BF16),
                            preferred_element_type=F32) for p, t in zip(p_cats, t_cats)]
            p_cats = [x[0:ch] for x in both]
            t_cats = [t + x[ch:2 * ch] for t, x in zip(t_cats, both)]
            k_pow *= 2
        t_cats = [t + jnp.dot(t.astype(BF16), stack_heads(p).astype(BF16), preferred_element_type=F32)
                  for p, t in zip(p_cats, t_cats)]
        tinvs = [stack_heads(t) for t in t_cats]
        kv = [_mm(x, vs) for x, vs in zip(a_k, v_ss)]
        w_ss = [_mm(t, jnp.concatenate([stack_heads(a_t[sl(c, p)]), stack_heads(x[0:ch])], axis=1))
                for t, x, (c, p) in zip(tinvs, kv, items)]
        ry = [_mm(x, w_s) for x, w_s in zip(a_rbs, w_ss)]
        r_eff = [r_t[sl(c, p)] + x[:, 0:pw] for x, (c, p) in zip(ry, items)]
        y0 = [x[:, pw:2 * pw] + y[ch:2 * ch] for x, y in zip(ry, kv)]
        bdc_ss = [stack_heads(b_dc[sl(c, p)]) for c, p in items]
        phi_t = [_mm(w_s[:, 0:pw].T, bd).astype(BF16) for w_s, bd in zip(w_ss, bdc_ss)]
        gamma_t = [_mm(jnp.concatenate([w_s[:, pw:2 * pw].T, vs.T], axis=1),
                       jnp.concatenate([bd, stack_heads(k_dc[sl(c, p)])], axis=0))
                   for w_s, vs, bd, (c, p) in zip(w_ss, v_ss, bdc_ss, items)]

        ys = [[] for _ in range(npair)]
        for c in range(gch):
            st_bs = [st.astype(BF16) for st in states]
            states = [states[p] * d_end[c * ch:c * ch + 1, p * pw:(p + 1) * pw]
                      + jnp.dot(st_bs[p], phi_t[c * npair + p], preferred_element_type=F32)
                      + gamma_t[c * npair + p] for p in range(npair)]
            for p in range(npair):
                i = c * npair + p
                ys[p].append(_mm_nt(r_eff[i], st_bs[p]) + y0[i])
        y = jnp.concatenate([jnp.concatenate(yp, axis=0) if gch > 1 else yp[0] for yp in ys], axis=1)

        mean = head_sum(y) * (1.0 / R7_N)
        yc = y - mean
        var = head_sum(yc * yc) * (1.0 / R7_N)
        yn = yc * lax.rsqrt(var + GN_EPS) * lnw_ref[...] + lnb_ref[...]
        bonus = head_sum(r * k * rk_ref[...]) * v
        out_ref[0, g0:g0 + grows, :] = ((yn + bonus) * gate).astype(out_ref.dtype)
    for p in range(npair):
        st_ref[p] = states[p]
    xbuf_ref[0:HALO, :] = z_all[ts - HALO:ts, :]


def _rwkv7(z, mu, w0, w2, a0, a2, g2, k_k, k_a, r_k, ln_w, ln_b, ts):
    b, s, fin = z.shape
    w = GROUP_WIDTH
    npair = w // (2 * R7_N)
    vec = _const_spec((1, w))
    return pl.pallas_call(
        functools.partial(_rwkv7_kernel, ts),
        grid=(b, s // ts),
        in_specs=[
            pl.BlockSpec((1, ts, fin), lambda bi, ti: (bi, ti, 0)),
            _const_spec((1, fin)), vec, _const_spec(w2.shape), vec, _const_spec(a2.shape),
            _const_spec(g2.shape), vec, vec, vec, vec, vec,
        ],
        out_specs=pl.BlockSpec((1, ts, w), lambda bi, ti: (bi, ti, 0)),
        out_shape=jax.ShapeDtypeStruct((b, s, w), BF16),
        scratch_shapes=[pltpu.VMEM((HALO + ts, fin), F32), pltpu.VMEM((npair, 2 * R7_N, 2 * R7_N), F32)],
        compiler_params=pltpu.CompilerParams(
            dimension_semantics=("parallel", "arbitrary"), vmem_limit_bytes=VMEM_LIMIT),
        name="rwkv7",
    )(z, mu, w0, w2, a0, a2, g2, k_k, k_a, r_k, ln_w, ln_b)


def _even_weights(w_in, rg_w_r, rg_w_i, rg_b_r, rg_b_i, gdn_a_log, gdn_dt_bias):
    w = GROUP_WIDTH
    nh = w // HEAD128
    w_rg = w_in[:, 0:2 * w]
    w_gdn = jnp.pad(w_in[:, 2 * w:6 * w + 2 * nh], ((0, 0), (0, HEAD128 - 2 * nh)))
    wri = jnp.concatenate([jax.scipy.linalg.block_diag(*rg_w_r), jax.scipy.linalg.block_diag(*rg_w_i)], axis=1)
    bri = jnp.concatenate([rg_b_r, rg_b_i]).reshape(1, 2 * w)
    alog = jnp.pad(gdn_a_log, (0, HEAD128 - nh)).reshape(1, HEAD128)
    dtb = jnp.pad(gdn_dt_bias, (0, HEAD128 - nh)).reshape(1, HEAD128)
    return w_rg.astype(BF16), w_gdn.astype(BF16), wri.astype(BF16), bri, alog, dtb


def _pad_groups(x, axis):
    w = GROUP_WIDTH
    sizes = (w, w, w, 64, 64, 128)
    parts = []
    start = 0
    for sz in sizes:
        piece = lax.slice_in_dim(x, start, start + sz, axis=axis)
        pad = (-sz) % R7_LORA_PAD
        if pad:
            cfg = [(0, 0)] * x.ndim
            cfg[axis] = (0, pad)
            piece = jnp.pad(piece, cfg)
        parts.append(piece)
        start += sz
    return jnp.concatenate(parts, axis=axis)


def _pad_rows(x, rows):
    return jnp.pad(x, ((0, rows - x.shape[0]), (0, 0)))


def _tiles(s):
    tm_proj = 512 if s % 512 == 0 else 256
    tm_post = 512
    ts = next(c for c in (512, 256, 128) if s % c == 0)
    return tm_proj, tm_post, ts


def kernel(x, p, norm_mix, norm_ffn, norm_ple, norm_final, e_w_in, rg_conv_w, rg_conv_b, rg_w_r, rg_b_r, rg_w_i, rg_b_i, rg_lambda, gdn_conv_w, gdn_a_log, gdn_dt_bias, gdn_norm_w, e_w_out, o_w_in, hg_lower_bounds, hg_norm_w, r7_mu, r7_w0, r7_w2, r7_a0, r7_a2, r7_g2, r7_k_k, r7_k_a, r7_r_k, r7_ln_w, r7_ln_b, o_w_out, ffn_w_gate, ffn_w_up, ffn_w_down, ple_w_up, ple_w_gate):
    b, s, d = x.shape
    depth = p.shape[0]
    t = b * s
    w = GROUP_WIDTH
    tm_proj, tm_post, ts = _tiles(s)
    h = x.reshape(t, d)
    row = lambda vec: vec.reshape(1, -1)
    for layer in range(depth):
        j = layer // 2
        if layer % 2 == 0:
            w_rg, w_gdn, wri, bri, alog, dtb = _even_weights(
                e_w_in[j], rg_w_r[j], rg_w_i[j], rg_b_r[j], rg_b_i[j], gdn_a_log[j], gdn_dt_bias[j])
            out_a, z_b = _proj_rglru(h, norm_mix[layer], w_rg, w_gdn, rg_conv_w[j], row(rg_conv_b[j]),
                                     wri, bri, row(rg_lambda[j]), tm_proj, s)
            out_b = _gdn(z_b.reshape(b, s, -1), gdn_conv_w[j], alog, dtb, row(gdn_norm_w[j]), ts)
            w_out = e_w_out[j]
        else:
            hg_in = 4 * w
            w_hg = o_w_in[j][:, 0:hg_in].astype(BF16)
            w_r7 = _pad_groups(o_w_in[j][:, hg_in:], axis=1).astype(BF16)
            z_a, z_b = _norm_proj(h, norm_mix[layer], w_hg, w_r7, tm_proj)
            out_a = _hgrn2(z_a.reshape(b, s, -1), hg_lower_bounds, row(hg_norm_w[j]), j, ts)
            out_b = _rwkv7(z_b.reshape(b, s, -1), row(_pad_groups(r7_mu[j], axis=0)), row(r7_w0[j]),
                           _pad_rows(r7_w2[j], R7_LORA_PAD).astype(BF16), row(r7_a0[j]),
                           _pad_rows(r7_a2[j], R7_LORA_PAD).astype(BF16), r7_g2[j].astype(BF16),
                           row(r7_k_k[j]), row(r7_k_a[j]), row(r7_r_k[j].reshape(-1)),
                           row(r7_ln_w[j]), row(r7_ln_b[j]), ts)
            w_out = o_w_out[j]
        h = _post(h, out_a.reshape(t, w), out_b.reshape(t, w), p.reshape(depth, t, PLE_DIM), layer,
                  w_out.astype(BF16), norm_ffn[layer], ffn_w_gate[layer].astype(BF16),
                  ffn_w_up[layer].astype(BF16), ffn_w_down[layer].astype(BF16), norm_ple[layer],
                  ple_w_gate[layer].astype(BF16), ple_w_up[layer].astype(BF16), norm_final,
                  layer == depth - 1, tm_post)
    return h.reshape(b, s, d)
```

```python
import functools
import math

import jax
import jax.numpy as jnp
from jax import lax
from jax.experimental import pallas as pl
from jax.experimental.pallas import tpu as pltpu

F32 = jnp.float32
BF16 = jnp.bfloat16

D_MODEL = 1024
PLE_DIM = 256
D_FF = 2816
GROUP_WIDTH = 512
RMS_EPS = 1e-6
L2_EPS = 1e-6
GN_EPS = 64e-5
RG_C = 8.0
RG_BLOCKS = 8
RG_ROWS = 64
CONV_TAPS = 4
HEAD128 = 128
R7_N = 64
R7_LORA_PAD = 128
R7_IN_PAD = 3 * GROUP_WIDTH + 3 * R7_LORA_PAD

GDN_CHUNK = 128
HG_CHUNK = 128
R7_CHUNK = 64
R7_GROUP = 4
HALO = 8

VMEM_LIMIT = 56 * 1024 * 1024


def _sigmoid(x):
    return 1.0 / (1.0 + jnp.exp(-x))


def _softplus(x):
    return jnp.maximum(x, 0.0) + jnp.log(1.0 + jnp.exp(-jnp.abs(x)))


def _silu(x):
    return x * _sigmoid(x)


def _gelu_tanh(x):
    c = math.sqrt(2.0 / math.pi)
    return 0.5 * x * (1.0 + jnp.tanh(c * (x + 0.044715 * (x * x * x))))


def _rms(x, w):
    ms = jnp.mean(x * x, axis=-1, keepdims=True)
    return x * lax.rsqrt(ms + RMS_EPS) * w


def _mm(a, b):
    return jnp.dot(a.astype(BF16), b.astype(BF16), preferred_element_type=F32)


def _mm_nt(a, b):
    return lax.dot_general(a.astype(BF16), b.astype(BF16), (((1,), (1,)), ((), ())),
                           preferred_element_type=F32)


def _split(x, terms):
    parts = []
    r = x
    for _ in range(terms):
        hi = r.astype(BF16)
        parts.append(hi)
        r = r - hi.astype(F32)
    return parts


def _sel_mm(sel, x, terms=3):
    acc = None
    for part in _split(x, terms):
        d = jnp.dot(sel, part, preferred_element_type=F32)
        acc = d if acc is None else acc + d
    return acc


def _mm_sel(x, sel, terms=2):
    acc = None
    for part in _split(x, terms):
        d = jnp.dot(part, sel, preferred_element_type=F32)
        acc = d if acc is None else acc + d
    return acc


def _mm_hi(a, b):
    a_hi, a_lo = _split(a, 2)
    b_hi, b_lo = _split(b, 2)
    return (jnp.dot(a_hi, b_hi, preferred_element_type=F32)
            + jnp.dot(a_lo, b_hi, preferred_element_type=F32)
            + jnp.dot(a_hi, b_lo, preferred_element_type=F32))


def _iota2(shape, axis):
    return lax.broadcasted_iota(jnp.int32, shape, axis)


def _block_tri(n, block, kind):
    t = _iota2((n, n), 0)
    s = _iota2((n, n), 1)
    same = (t // block) == (s // block)
    cond = (s <= t) if kind == "incl" else (s > t)
    return jnp.where(same & cond, 1.0, 0.0).astype(BF16)


def _chunk_last(x, chunk):
    cols = x.shape[1]
    return jnp.concatenate(
        [jnp.broadcast_to(x[r + chunk - 1:r + chunk, :], (chunk, cols)) for r in range(0, x.shape[0], chunk)],
        axis=0)


def _block_ones(n, block):
    t = _iota2((n, n), 0)
    s = _iota2((n, n), 1)
    return jnp.where((t // block) == (s // block), 1.0, 0.0).astype(BF16)


def _nilpotent_inverses(xs, n, refine):
    shape = xs[0].shape
    eye = jnp.where(_iota2(shape, 0) == _iota2(shape, 1), 1.0, 0.0).astype(F32)
    ts = [eye + x for x in xs]
    ps = list(xs)
    k = 2
    while k < n:
        ps = [_mm(p, p) for p in ps]
        ts = [t + _mm(t, p) for t, p in zip(ts, ps)]
        k *= 2
    if refine:
        res = [(eye - t) + _mm_hi(x, t) for x, t in zip(xs, ts)]
        ts = [t + _mm(t, r) for t, r in zip(ts, res)]
    return ts


def _norm_proj_kernel(h_ref, nw_ref, wa_ref, wb_ref, za_ref, zb_ref):
    xn = _rms(h_ref[...], nw_ref[...]).astype(BF16)
    za_ref[...] = jnp.dot(xn, wa_ref[...], preferred_element_type=F32)
    zb_ref[...] = jnp.dot(xn, wb_ref[...], preferred_element_type=F32)


def _const_spec(shape):
    nd = len(shape)
    return pl.BlockSpec(shape, lambda *_: (0,) * nd, pipeline_mode=pl.Buffered(1))


def _norm_proj(h, norm_w, wa, wb, tm):
    t, d = h.shape
    fa, fb = wa.shape[1], wb.shape[1]
    return pl.pallas_call(
        _norm_proj_kernel,
        grid=(t // tm,),
        in_specs=[
            pl.BlockSpec((tm, d), lambda i: (i, 0)),
            _const_spec((1, d)),
            _const_spec((d, fa)),
            _const_spec((d, fb)),
        ],
        out_specs=[
            pl.BlockSpec((tm, fa), lambda i: (i, 0)),
            pl.BlockSpec((tm, fb), lambda i: (i, 0)),
        ],
        out_shape=[jax.ShapeDtypeStruct((t, fa), F32), jax.ShapeDtypeStruct((t, fb), F32)],
        compiler_params=pltpu.CompilerParams(
            dimension_semantics=("parallel",), vmem_limit_bytes=VMEM_LIMIT),
        name="norm_proj",
    )(h, norm_w.reshape(1, d), wa, wb)


def _post_kernel(final_norm, h_ref, ma_ref, mb_ref, p_ref, wo_ref, nf_ref, wg_ref, wu_ref, wd_ref,
                 npl_ref, wpg_ref, wpu_ref, nfin_ref, out_ref):
    h = h_ref[...]
    mix = (jnp.dot(ma_ref[...], wo_ref[0:GROUP_WIDTH, :], preferred_element_type=F32)
           + jnp.dot(mb_ref[...], wo_ref[GROUP_WIDTH:2 * GROUP_WIDTH, :], preferred_element_type=F32))
    h1 = h + mix
    hn = _rms(h1, nf_ref[...]).astype(BF16)
    g = jnp.dot(hn, wg_ref[...], preferred_element_type=F32)
    u = jnp.dot(hn, wu_ref[...], preferred_element_type=F32)
    act = (_silu(g) * u).astype(BF16)
    h2 = h1 + jnp.dot(act, wd_ref[...], preferred_element_type=F32)
    hp = _rms(h2, npl_ref[...]).astype(BF16)
    gate = _sigmoid(jnp.dot(hp, wpg_ref[...], preferred_element_type=F32))
    up = jnp.dot(p_ref[...].astype(BF16), wpu_ref[...], preferred_element_type=F32)
    h3 = h2 + gate * up
    if final_norm:
        h3 = _rms(h3, nfin_ref[...])
    out_ref[...] = h3


def _post(h, ma, mb, p_all, layer, wo, nf, wg, wu, wd, npl, wpg, wpu, nfin, final_norm, tm):
    t, d = h.shape
    row = lambda w: pl.BlockSpec((tm, w), lambda i: (i, 0))
    return pl.pallas_call(
        functools.partial(_post_kernel, final_norm),
        grid=(t // tm,),
        in_specs=[
            row(d), row(GROUP_WIDTH), row(GROUP_WIDTH),
            pl.BlockSpec((None, tm, PLE_DIM), lambda i: (layer, i, 0)),
            _const_spec(wo.shape), _const_spec((1, d)),
            _const_spec(wg.shape), _const_spec(wu.shape), _const_spec(wd.shape),
            _const_spec((1, d)), _const_spec(wpg.shape), _const_spec(wpu.shape),
            _const_spec((1, d)),
        ],
        out_specs=row(d),
        out_shape=jax.ShapeDtypeStruct((t, d), F32),
        compiler_params=pltpu.CompilerParams(
            dimension_semantics=("parallel",), vmem_limit_bytes=VMEM_LIMIT),
        name="post_mixer",
    )(h, ma, mb, p_all, wo, nf.reshape(1, d), wg, wu, wd, npl.reshape(1, d), wpg, wpu, nfin.reshape(1, d))


def _causal_conv(xbuf_ref, x, w_ref, ts):
    xcat = jnp.concatenate([xbuf_ref[...], x], axis=0)
    acc = w_ref[CONV_TAPS - 1:CONV_TAPS, :] * x
    for back in range(1, CONV_TAPS):
        shifted = pltpu.roll(xcat, back, 0)[HALO:HALO + ts, :]
        acc = acc + w_ref[CONV_TAPS - 1 - back:CONV_TAPS - back, :] * shifted
    xbuf_ref[...] = x[ts - HALO:ts, :]
    return acc


def _proj_rglru_kernel(tm, tiles_per_seq, h_ref, nw_ref, wa_ref, wb_ref, cw_ref, cb_ref, wri_ref, bri_ref,
                       lam_ref, out_ref, zb_ref, xbuf_ref, hc_ref):
    w = GROUP_WIDTH
    first = (pl.program_id(0) % tiles_per_seq) == 0

    @pl.when(first)
    def _():
        xbuf_ref[0:HALO, :] = jnp.zeros((HALO, w), F32)
        hc_ref[...] = jnp.zeros_like(hc_ref)

    xn = _rms(h_ref[...], nw_ref[...]).astype(BF16)
    z_rg = jnp.dot(xn, wa_ref[...], preferred_element_type=F32)
    xa = z_rg[:, 0:w]
    ya = z_rg[:, w:2 * w]
    x = _causal_conv(xbuf_ref, xa, cw_ref, tm) + cb_ref[...]
    ri = jnp.dot(x.astype(BF16), wri_ref[...], preferred_element_type=F32) + bri_ref[...]
    zb_ref[...] = jnp.dot(xn, wb_ref[...], preferred_element_type=F32)
    sp_lam = _softplus(-lam_ref[...])
    lanes = HEAD128
    row = _iota2((RG_ROWS, lanes), 0)
    for l0 in range(0, w, lanes):
        ls = slice(l0, l0 + lanes)
        carry = hc_ref[:, ls]
        for r0 in range(0, tm, RG_ROWS):
            rs = slice(r0, r0 + RG_ROWS)
            r = _sigmoid(ri[rs, l0:l0 + lanes])
            i = _sigmoid(ri[rs, w + l0:w + l0 + lanes])
            log_a = -RG_C * r * sp_lam[:, ls]
            a = jnp.exp(log_a)
            mult = jnp.sqrt(-jnp.tanh(log_a) * (a * a + 1.0))
            if r0 == 0:
                mult = jnp.where((row == 0) & first, 1.0, mult)
            acc_a, acc_h = a, mult * (i * x[rs, ls])
            d = 1
            while d < RG_ROWS:
                if d % HALO == 0:
                    sh_a = jnp.concatenate([jnp.ones((d, lanes), F32), acc_a[0:RG_ROWS - d, :]], axis=0)
                    sh_h = jnp.concatenate([jnp.zeros((d, lanes), F32), acc_h[0:RG_ROWS - d, :]], axis=0)
                else:
                    sh_a = jnp.where(row >= d, pltpu.roll(acc_a, d, 0), 1.0)
                    sh_h = jnp.where(row >= d, pltpu.roll(acc_h, d, 0), 0.0)
                acc_h = acc_a * sh_h + acc_h
                acc_a = acc_a * sh_a
                d *= 2
            h = acc_h + acc_a * carry
            carry = h[RG_ROWS - 1:RG_ROWS, :]
            out_ref[rs, ls] = (h * _gelu_tanh(ya[rs, ls])).astype(out_ref.dtype)
        hc_ref[:, ls] = carry


def _proj_rglru(h, norm_w, wa, wb, cw, cb, wri, bri, lam, tm, seq_len):
    t, d = h.shape
    w = GROUP_WIDTH
    fb = wb.shape[1]
    return pl.pallas_call(
        functools.partial(_proj_rglru_kernel, tm, seq_len // tm),
        grid=(t // tm,),
        in_specs=[
            pl.BlockSpec((tm, d), lambda i: (i, 0)),
            _const_spec((1, d)), _const_spec(wa.shape), _const_spec(wb.shape),
            _const_spec((CONV_TAPS, w)), _const_spec((1, w)),
            _const_spec((w, 2 * w)), _const_spec((1, 2 * w)), _const_spec((1, w)),
        ],
        out_specs=[
            pl.BlockSpec((tm, w), lambda i: (i, 0)),
            pl.BlockSpec((tm, fb), lambda i: (i, 0)),
        ],
        out_shape=[jax.ShapeDtypeStruct((t, w), BF16), jax.ShapeDtypeStruct((t, fb), F32)],
        scratch_shapes=[pltpu.VMEM((HALO, w), F32), pltpu.VMEM((1, w), F32)],
        compiler_params=pltpu.CompilerParams(
            dimension_semantics=("arbitrary",), vmem_limit_bytes=VMEM_LIMIT),
        name="proj_rglru",
    )(h, norm_w.reshape(1, d), wa, wb, cw, cb, wri, bri, lam)


def _gdn_kernel(ts, z_ref, cw_ref, alog_ref, dtb_ref, nw_ref, out_ref, xbuf_ref, st_ref):
    w = GROUP_WIDTH
    hd = HEAD128
    nh = w // hd
    cs = GDN_CHUNK
    nchunk = ts // cs
    tstep = pl.program_id(1)

    @pl.when(tstep == 0)
    def _():
        xbuf_ref[0:HALO, :] = jnp.zeros((HALO, 3 * w), F32)
        st_ref[...] = jnp.zeros_like(st_ref)

    qkv = _silu(_causal_conv(xbuf_ref, z_ref[0, :, 0:3 * w], cw_ref, ts))
    seg_w = 2 * hd
    ones_bd = _block_ones(seg_w, hd)

    def head_sum(x):
        xb = x.astype(BF16)
        return jnp.concatenate(
            [jnp.dot(xb[:, c0:c0 + seg_w], ones_bd, preferred_element_type=F32)
             for c0 in range(0, x.shape[1], seg_w)], axis=1)

    qk = qkv[:, 0:2 * w]
    qk = qk * lax.rsqrt(head_sum(qk * qk) + L2_EPS)
    zg = z_ref[0, :, 3 * w:4 * w]
    ab = z_ref[0, :, 4 * w:4 * w + hd]
    g_n = -jnp.exp(alog_ref[...]) * _softplus(ab + dtb_ref[...])
    gc_n = _sel_mm(_block_tri(ts, cs, "incl"), g_n)
    src = _iota2((hd, w), 0)
    dst_head = _iota2((hd, w), 1) // hd
    gc_all = _mm_sel(gc_n, jnp.where(src == dst_head, 1.0, 0.0).astype(BF16), terms=3)
    beta = _mm_sel(_sigmoid(ab), jnp.where(src == dst_head + nh, 1.0, 0.0).astype(BF16), terms=3)
    gs_all = _chunk_last(gc_all, cs) - gc_all
    eg_all = jnp.exp(gc_all)
    scale = hd ** -0.5
    tt = _iota2((cs, cs), 0)
    ss = _iota2((cs, cs), 1)

    items = [(h, c) for c in range(nchunk) for h in range(nh)]
    qs, ks, vbs, kbegs, decs, kdts, glast = [], [], [], [], [], [], []
    for h, c in items:
        lo = h * hd
        rows = slice(c * cs, (c + 1) * cs)
        hs = slice(lo, lo + hd)
        qh = qk[rows, lo:lo + hd] * scale
        kh = qk[rows, w + lo:w + lo + hd]
        vh = qkv[rows, 2 * w + lo:2 * w + lo + hd]
        gc = gc_all[rows, hs]
        bt = beta[rows, hs]
        eg = eg_all[rows, hs]
        kb = kh * bt
        diff = gc - gc.T
        decs.append(jnp.exp(jnp.where(ss <= tt, diff, -jnp.inf)))
        qs.append(qh * eg)
        ks.append((kh, kb, qh))
        vbs.append(vh * bt)
        kbegs.append(kb * eg)
        kdts.append((kh * jnp.exp(gs_all[rows, hs])).T)
        glast.append(jnp.exp(gc_all[(c + 1) * cs - 1:(c + 1) * cs, hs]))
    a_mats = [jnp.where(ss < tt, _mm_nt(kb, kh) * dec, 0.0) for (kh, kb, _), dec in zip(ks, decs)]
    qks = [_mm_nt(qh, kh) * dec for (kh, _, qh), dec in zip(ks, decs)]
    tinvs = _nilpotent_inverses([-a for a in a_mats], cs, refine=True)
    sols = [_mm(t, jnp.concatenate([vb, kbeg], axis=1)) for t, vb, kbeg in zip(tinvs, vbs, kbegs)]
    us = [sol[:, 0:hd] for sol in sols]
    ws = [sol[:, hd:2 * hd] for sol in sols]

    states = [st_ref[h] for h in range(nh)]
    outs = [[] for _ in range(nh)]
    for c in range(nchunk):
        idx = [c * nh + h for h in range(nh)]
        proj = [_mm(jnp.concatenate([ws[i], qs[i]], axis=0), states[h]) for h, i in enumerate(idx)]
        v_new = [us[i] - pr[0:cs] for i, pr in zip(idx, proj)]
        o = [pr[cs:2 * cs] + _mm(qks[i], vn) for i, pr, vn in zip(idx, proj, v_new)]
        states = [glast[i] * states[h] + _mm(kdts[i], vn) for (h, i), vn in zip(enumerate(idx), v_new)]
        for h in range(nh):
            outs[h].append(o[h])
    for h in range(nh):
        lo = h * hd
        st_ref[h] = states[h]
        o = jnp.concatenate(outs[h], axis=0) if nchunk > 1 else outs[h][0]
        o = _rms(o, nw_ref[...]) * _silu(zg[:, lo:lo + hd])
        out_ref[0, :, lo:lo + hd] = o.astype(out_ref.dtype)


def _gdn(z, cw, alog, dtb, nw, ts):
    b, s, _ = z.shape
    w = GROUP_WIDTH
    nh = w // HEAD128
    return pl.pallas_call(
        functools.partial(_gdn_kernel, ts),
        grid=(b, s // ts),
        in_specs=[
            pl.BlockSpec((1, ts, 4 * w + HEAD128), lambda bi, ti: (bi, ti, 0)),
            _const_spec((CONV_TAPS, 3 * w)), _const_spec((1, HEAD128)), _const_spec((1, HEAD128)),
            _const_spec((1, HEAD128)),
        ],
        out_specs=pl.BlockSpec((1, ts, w), lambda bi, ti: (bi, ti, 0)),
        out_shape=jax.ShapeDtypeStruct((b, s, w), BF16),
        scratch_shapes=[pltpu.VMEM((HALO, 3 * w), F32), pltpu.VMEM((nh, HEAD128, HEAD128), F32)],
        compiler_params=pltpu.CompilerParams(
            dimension_semantics=("parallel", "arbitrary"), vmem_limit_bytes=VMEM_LIMIT),
        name="gated_deltanet",
    )(z, cw, alog, dtb, nw)


def _hgrn2_kernel(ts, layer_j, z_ref, lbraw_ref, nw_ref, out_ref, st_ref, o_ref):
    w = GROUP_WIDTH
    hd = HEAD128
    nh = w // hd
    tstep = pl.program_id(1)

    @pl.when(tstep == 0)
    def _():
        st_ref[...] = jnp.zeros_like(st_ref)

    lbraw = lbraw_ref[...]
    e = jnp.exp(lbraw - jnp.max(lbraw, axis=0, keepdims=True))
    soft = e / jnp.sum(e, axis=0, keepdims=True)
    lb = jnp.sum(soft[0:layer_j + 1, :], axis=0, keepdims=True) - soft[0:1, :]
    lb = jnp.maximum(lb, 0.0)

    q = _silu(z_ref[0, :, 0:w])
    f = z_ref[0, :, w:2 * w]
    v = z_ref[0, :, 2 * w:3 * w]
    gate = z_ref[0, :, 3 * w:4 * w]
    kin = (1.0 - lb) * _sigmoid(-f)
    la = jnp.log(lb)
    lbm = jnp.log1p(-lb) - _softplus(-f)
    mx = jnp.maximum(la, lbm)
    log_f = mx + jnp.log(1.0 + jnp.exp(-jnp.abs(la - lbm)))

    cs = HG_CHUNK
    nchunk = ts // cs
    b = _sel_mm(_block_tri(ts, cs, "incl"), log_f)
    b2 = b * math.log2(math.e)
    row = _iota2((ts, w), 0)
    tt = _iota2((cs, cs), 0)
    ss = _iota2((cs, cs), 1)
    items = [(c, h) for c in range(nchunk) for h in range(nh)]
    sl = lambda c, h: (slice(c * cs, (c + 1) * cs), slice(h * hd, (h + 1) * hd))

    scores = [jnp.where(tt == ss, _mm_nt(q[sl(c, h)], kin[sl(c, h)]), 0.0) for c, h in items]
    block_end = b2
    m = 1
    while m < cs:
        prev_end = pltpu.roll(block_end, m, 0)
        q_m = q * jnp.exp2(b2 - prev_end)
        k_m = kin * jnp.exp2(block_end - b2)
        pair = ((tt // m) % 2 == 1) & ((ss // m) == (tt // m) - 1)
        scores = [jnp.where(pair, _mm_nt(q_m[sl(c, h)], k_m[sl(c, h)]), sc)
                  for sc, (c, h) in zip(scores, items)]
        block_end = jnp.where((row // m) % 2 == 0, pltpu.roll(block_end, ts - m, 0), block_end)
        m *= 2
    o_intra = [_mm(sc, v[sl(c, h)]) for sc, (c, h) in zip(scores, items)]

    btot = _chunk_last(b, cs)
    qe = q * jnp.exp(b)
    k_dec = kin * jnp.exp(btot - b)
    dec = jnp.exp(btot)
    incs = [_mm(v[sl(c, h)].T, k_dec[sl(c, h)]) for c, h in items]
    states = [st_ref[h] for h in range(nh)]
    for c in range(nchunk):
        for h in range(nh):
            i = c * nh + h
            o_ref[c * cs:(c + 1) * cs, h * hd:(h + 1) * hd] = o_intra[i] + _mm_nt(qe[sl(c, h)], states[h])
        states = [states[h] * dec[c * cs:c * cs + 1, h * hd:(h + 1) * hd] + incs[c * nh + h]
                  for h in range(nh)]
    for h in range(nh):
        st_ref[h] = states[h]

    out_ref[0] = (_rms(o_ref[...], nw_ref[...]) * _sigmoid(gate)).astype(out_ref.dtype)


def _hgrn2(z, lbraw, nw, layer_j, ts):
    b, s, _ = z.shape
    w = GROUP_WIDTH
    nh = w // HEAD128
    return pl.pallas_call(
        functools.partial(_hgrn2_kernel, ts, layer_j),
        grid=(b, s // ts),
        in_specs=[
            pl.BlockSpec((1, ts, 4 * w), lambda bi, ti: (bi, ti, 0)),
            _const_spec(lbraw.shape), _const_spec((1, w)),
        ],
        out_specs=pl.BlockSpec((1, ts, w), lambda bi, ti: (bi, ti, 0)),
        out_shape=jax.ShapeDtypeStruct((b, s, w), BF16),
        scratch_shapes=[pltpu.VMEM((nh, HEAD128, HEAD128), F32), pltpu.VMEM((ts, w), F32)],
        compiler_params=pltpu.CompilerParams(
            dimension_semantics=("parallel", "arbitrary"), vmem_limit_bytes=VMEM_LIMIT),
        name="hgrn2",
    )(z, lbraw, nw)


def _rwkv7_kernel(ts, z_ref, mu_ref, w0_ref, w2_ref, a0_ref, a2_ref, g2_ref, kk_ref, ka_ref, rk_ref,
                  lnw_ref, lnb_ref, out_ref, xbuf_ref, st_ref):
    w = GROUP_WIDTH
    lp = R7_LORA_PAD
    ch = R7_CHUNK
    npair = w // (2 * R7_N)
    tstep = pl.program_id(1)

    @pl.when(tstep == 0)
    def _():
        xbuf_ref[0:HALO, :] = jnp.zeros((HALO, R7_IN_PAD), F32)
        st_ref[...] = jnp.zeros_like(st_ref)

    pw = 2 * R7_N
    n = R7_N
    gch = min(R7_GROUP, ts // ch)
    grows = gch * ch
    lane = _iota2((ch, pw), 1)
    in_h0 = lane < n
    t_row = _iota2((ch, pw), 0)
    s_col = lane % n
    seg_w = 2 * HEAD128
    ones_bd = _block_ones(seg_w, R7_N)

    def head_sum(x):
        xb = x.astype(BF16)
        return jnp.concatenate(
            [jnp.dot(xb[:, c0:c0 + seg_w], ones_bd, preferred_element_type=F32) for c0 in range(0, w, seg_w)],
            axis=1)

    tri_incl = _block_tri(grows, ch, "incl")
    items = [(c, p) for c in range(gch) for p in range(npair)]
    sl = lambda c, p: (slice(c * ch, (c + 1) * ch), slice(p * pw, (p + 1) * pw))

    def stack_heads(x):
        return jnp.concatenate([jnp.where(in_h0, x, 0.0), jnp.where(in_h0, 0.0, x)], axis=0)

    z_all = z_ref[0]
    z_prev_all = pltpu.roll(jnp.concatenate([xbuf_ref[0:HALO, :], z_all], axis=0), 1, 0)[HALO:HALO + ts, :]
    states = [st_ref[p] for p in range(npair)]
    for g0 in range(0, ts, grows):
        z = z_all[g0:g0 + grows, :]
        z_prev = z_prev_all[g0:g0 + grows, :]
        zd = z + mu_ref[...] * (z_prev - z)
        r = zd[:, 0:w]
        k = zd[:, w:2 * w]
        v = zd[:, 2 * w:3 * w]
        w_l = zd[:, 3 * w:3 * w + lp]
        a_l = zd[:, 3 * w + lp:3 * w + 2 * lp]
        g_l = zd[:, 3 * w + 2 * lp:3 * w + 3 * lp]
        log_w = -jnp.exp(-_softplus(-(w0_ref[...] + _mm(jnp.tanh(w_l), w2_ref[...]))) - 0.5)
        a = _sigmoid(a0_ref[...] + _mm(a_l, a2_ref[...]))
        gate = _mm(_sigmoid(g_l), g2_ref[...])
        kk = k * kk_ref[...]
        kk = kk * lax.rsqrt(head_sum(kk * kk) + L2_EPS)
        k = k * (1.0 + (a - 1.0) * ka_ref[...])
        cum = _sel_mm(tri_incl, log_w)
        total = _chunk_last(cum, ch)
        e_inv = jnp.exp(-cum)
        r_t = r * jnp.exp(cum)
        a_t = -kk * jnp.exp(cum - log_w)
        kka = kk * a
        b_h = kka * e_inv
        k_h = k * e_inv
        e_suf = jnp.exp(total - cum)
        b_dc = kka * e_suf
        k_dc = k * e_suf
        d_end = jnp.exp(total)

        v_ss = [stack_heads(v[sl(c, p)]) for c, p in items]
        grams = [
            _mm_nt(jnp.concatenate([a_t[sl(c, p)], r_t[sl(c, p)]], axis=0),
                   jnp.concatenate([stack_heads(b_h[sl(c, p)]), stack_heads(k_h[sl(c, p)])], axis=0))
            for c, p in items]
        a_abs = [jnp.where(s_col < t_row, g[0:ch, 0:pw], 0.0) for g in grams]
        a_rbs = [jnp.where(s_col <= t_row, g[ch:2 * ch, 0:pw], 0.0) for g in grams]
        a_k = [jnp.concatenate([jnp.where(s_col < t_row, g[0:ch, pw:2 * pw], 0.0),
                                jnp.where(s_col <= t_row, g[ch:2 * ch, pw:2 * pw], 0.0)], axis=0)
               for g in grams]
        eye_cat = jnp.where(s_col == t_row, 1.0, 0.0).astype(F32)
        t_cats = [eye_cat + x for x in a_abs]
        p_cats = [jnp.dot(p.astype(BF16), stack_heads(p).astype(BF16), preferred_element_type=F32)
                  for p in a_abs]
        k_pow = 2
        while 2 * k_pow < ch:
            both = [jnp.dot(jnp.concatenate([p, t], axis=0).astype(BF16), stack_heads(p).astype(BF16),
                            preferred_element_type=F32) for p, t in zip(p_cats, t_cats)]
            p_cats = [x[0:ch] for x in both]
            t_cats = [t + x[ch:2 * ch] for t, x in zip(t_cats, both)]
            k_pow *= 2
        t_cats = [t + jnp.dot(t.astype(BF16), stack_heads(p).astype(BF16), preferred_element_type=F32)
                  for p, t in zip(p_cats, t_cats)]
        tinvs = [stack_heads(t) for t in t_cats]
        kv = [_mm(x, vs) for x, vs in zip(a_k, v_ss)]
        w_ss = [_mm(t, jnp.concatenate([stack_heads(a_t[sl(c, p)]), stack_heads(x[0:ch])], axis=1))
                for t, x, (c, p) in zip(tinvs, kv, items)]
        ry = [_mm(x, w_s) for x, w_s in zip(a_rbs, w_ss)]
        r_eff = [r_t[sl(c, p)] + x[:, 0:pw] for x, (c, p) in zip(ry, items)]
        y0 = [x[:, pw:2 * pw] + y[ch:2 * ch] for x, y in zip(ry, kv)]
        bdc_ss = [stack_heads(b_dc[sl(c, p)]) for c, p in items]
        phi_t = [_mm(w_s[:, 0:pw].T, bd).astype(BF16) for w_s, bd in zip(w_ss, bdc_ss)]
        gamma_t = [_mm(jnp.concatenate([w_s[:, pw:2 * pw].T, vs.T], axis=1),
                       jnp.concatenate([bd, stack_heads(k_dc[sl(c, p)])], axis=0))
                   for w_s, vs, bd, (c, p) in zip(w_ss, v_ss, bdc_ss, items)]

        ys = [[] for _ in range(npair)]
        for c in range(gch):
            st_bs = [st.astype(BF16) for st in states]
            states = [states[p] * d_end[c * ch:c * ch + 1, p * pw:(p + 1) * pw]
                      + jnp.dot(st_bs[p], phi_t[c * npair + p], preferred_element_type=F32)
                      + gamma_t[c * npair + p] for p in range(npair)]
            for p in range(npair):
                i = c * npair + p
                ys[p].append(_mm_nt(r_eff[i], st_bs[p]) + y0[i])
        y = jnp.concatenate([jnp.concatenate(yp, axis=0) if gch > 1 else yp[0] for yp in ys], axis=1)

        mean = head_sum(y) * (1.0 / R7_N)
        yc = y - mean
        var = head_sum(yc * yc) * (1.0 / R7_N)
        yn = yc * lax.rsqrt(var + GN_EPS) * lnw_ref[...] + lnb_ref[...]
        bonus = head_sum(r * k * rk_ref[...]) * v
        out_ref[0, g0:g0 + grows, :] = ((yn + bonus) * gate).astype(out_ref.dtype)
    for p in range(npair):
        st_ref[p] = states[p]
    xbuf_ref[0:HALO, :] = z_all[ts - HALO:ts, :]


def _rwkv7(z, mu, w0, w2, a0, a2, g2, k_k, k_a, r_k, ln_w, ln_b, ts):
    b, s, fin = z.shape
    w = GROUP_WIDTH
    npair = w // (2 * R7_N)
    vec = _const_spec((1, w))
    return pl.pallas_call(
        functools.partial(_rwkv7_kernel, ts),
        grid=(b, s // ts),
        in_specs=[
            pl.BlockSpec((1, ts, fin), lambda bi, ti: (bi, ti, 0)),
            _const_spec((1, fin)), vec, _const_spec(w2.shape), vec, _const_spec(a2.shape),
            _const_spec(g2.shape), vec, vec, vec, vec, vec,
        ],
        out_specs=pl.BlockSpec((1, ts, w), lambda bi, ti: (bi, ti, 0)),
        out_shape=jax.ShapeDtypeStruct((b, s, w), BF16),
        scratch_shapes=[pltpu.VMEM((HALO, fin), F32), pltpu.VMEM((npair, 2 * R7_N, 2 * R7_N), F32)],
        compiler_params=pltpu.CompilerParams(
            dimension_semantics=("parallel", "arbitrary"), vmem_limit_bytes=VMEM_LIMIT),
        name="rwkv7",
    )(z, mu, w0, w2, a0, a2, g2, k_k, k_a, r_k, ln_w, ln_b)


def _even_weights(w_in, rg_w_r, rg_w_i, rg_b_r, rg_b_i, gdn_a_log, gdn_dt_bias):
    w = GROUP_WIDTH
    nh = w // HEAD128
    w_rg = w_in[:, 0:2 * w]
    w_gdn = jnp.pad(w_in[:, 2 * w:6 * w + 2 * nh], ((0, 0), (0, HEAD128 - 2 * nh)))
    wri = jnp.concatenate([jax.scipy.linalg.block_diag(*rg_w_r), jax.scipy.linalg.block_diag(*rg_w_i)], axis=1)
    bri = jnp.concatenate([rg_b_r, rg_b_i]).reshape(1, 2 * w)
    alog = jnp.pad(gdn_a_log, (0, HEAD128 - nh)).reshape(1, HEAD128)
    dtb = jnp.pad(gdn_dt_bias, (0, HEAD128 - nh)).reshape(1, HEAD128)
    return w_rg.astype(BF16), w_gdn.astype(BF16), wri.astype(BF16), bri, alog, dtb


def _pad_groups(x, axis):
    w = GROUP_WIDTH
    sizes = (w, w, w, 64, 64, 128)
    parts = []
    start = 0
    for sz in sizes:
        piece = lax.slice_in_dim(x, start, start + sz, axis=axis)
        pad = (-sz) % R7_LORA_PAD
        if pad:
            cfg = [(0, 0)] * x.ndim
            cfg[axis] = (0, pad)
            piece = jnp.pad(piece, cfg)
        parts.append(piece)
        start += sz
    return jnp.concatenate(parts, axis=axis)


def _pad_rows(x, rows):
    return jnp.pad(x, ((0, rows - x.shape[0]), (0, 0)))


def _tiles(s):
    tm_proj = 512 if s % 512 == 0 else 256
    tm_post = 512
    ts = next(c for c in (512, 256, 128) if s % c == 0)
    ts_hg = min(ts, 256)
    return tm_proj, tm_post, ts, ts_hg


def kernel(x, p, norm_mix, norm_ffn, norm_ple, norm_final, e_w_in, rg_conv_w, rg_conv_b, rg_w_r, rg_b_r, rg_w_i, rg_b_i, rg_lambda, gdn_conv_w, gdn_a_log, gdn_dt_bias, gdn_norm_w, e_w_out, o_w_in, hg_lower_bounds, hg_norm_w, r7_mu, r7_w0, r7_w2, r7_a0, r7_a2, r7_g2, r7_k_k, r7_k_a, r7_r_k, r7_ln_w, r7_ln_b, o_w_out, ffn_w_gate, ffn_w_up, ffn_w_down, ple_w_up, ple_w_gate):
    b, s, d = x.shape
    depth = p.shape[0]
    t = b * s
    w = GROUP_WIDTH
    tm_proj, tm_post, ts, ts_hg = _tiles(s)
    h = x.reshape(t, d)
    row = lambda vec: vec.reshape(1, -1)
    for layer in range(depth):
        j = layer // 2
        if layer % 2 == 0:
            w_rg, w_gdn, wri, bri, alog, dtb = _even_weights(
                e_w_in[j], rg_w_r[j], rg_w_i[j], rg_b_r[j], rg_b_i[j], gdn_a_log[j], gdn_dt_bias[j])
            out_a, z_b = _proj_rglru(h, norm_mix[layer], w_rg, w_gdn, rg_conv_w[j], row(rg_conv_b[j]),
                                     wri, bri, row(rg_lambda[j]), tm_proj, s)
            out_b = _gdn(z_b.reshape(b, s, -1), gdn_conv_w[j], alog, dtb, row(gdn_norm_w[j]), ts)
            w_out = e_w_out[j]
        else:
            hg_in = 4 * w
            w_hg = o_w_in[j][:, 0:hg_in].astype(BF16)
            w_r7 = _pad_groups(o_w_in[j][:, hg_in:], axis=1).astype(BF16)
            z_a, z_b = _norm_proj(h, norm_mix[layer], w_hg, w_r7, tm_proj)
            out_a = _hgrn2(z_a.reshape(b, s, -1), hg_lower_bounds, row(hg_norm_w[j]), j, ts_hg)
            out_b = _rwkv7(z_b.reshape(b, s, -1), row(_pad_groups(r7_mu[j], axis=0)), row(r7_w0[j]),
                           _pad_rows(r7_w2[j], R7_LORA_PAD).astype(BF16), row(r7_a0[j]),
                           _pad_rows(r7_a2[j], R7_LORA_PAD).astype(BF16), r7_g2[j].astype(BF16),
                           row(r7_k_k[j]), row(r7_k_a[j]), row(r7_r_k[j].reshape(-1)),
                           row(r7_ln_w[j]), row(r7_ln_b[j]), ts)
            w_out = o_w_out[j]
        h = _post(h, out_a.reshape(t, w), out_b.reshape(t, w), p.reshape(depth, t, PLE_DIM), layer,
                  w_out.astype(BF16), norm_ffn[layer], ffn_w_gate[layer].astype(BF16),
                  ffn_w_up[layer].astype(BF16), ffn_w_down[layer].astype(BF16), norm_ple[layer],
                  ple_w_gate[layer].astype(BF16), ple_w_up[layer].astype(BF16), norm_final,
                  layer == depth - 1, tm_post)
    return h.reshape(b, s, d)
```

```python
import functools
import math

import jax
import jax.numpy as jnp
from jax import lax
from jax.experimental import pallas as pl
from jax.experimental.pallas import tpu as pltpu

F32 = jnp.float32
BF16 = jnp.bfloat16

PLE_DIM = 256
GROUP_WIDTH = 512
RMS_EPS = 1e-6
L2_EPS = 1e-6
GN_EPS = 64e-5
RG_C = 8.0
RG_ROWS = 64
CONV_TAPS = 4
HEAD128 = 128
R7_N = 64
R7_LORA_PAD = 128
R7_IN_PAD = 3 * GROUP_WIDTH + 3 * R7_LORA_PAD

GDN_CHUNK = 128
HG_CHUNK = 128
R7_CHUNK = 64
R7_GROUP = 4
HALO = 8

VMEM_LIMIT = 56 * 1024 * 1024


def _sigmoid(x):
    return 1.0 / (1.0 + jnp.exp2(x * -math.log2(math.e)))


def _softplus(x):
    return jnp.maximum(x, 0.0) + jnp.log(1.0 + jnp.exp(-jnp.abs(x)))


def _silu(x):
    return x * _sigmoid(x)


def _gelu_tanh(x):
    c = math.sqrt(2.0 / math.pi)
    return 0.5 * x * (1.0 + jnp.tanh(c * (x + 0.044715 * (x * x * x))))


def _rms(x, w):
    ms = jnp.mean(x * x, axis=-1, keepdims=True)
    return x * lax.rsqrt(ms + RMS_EPS) * w


def _mm(a, b):
    return jnp.dot(a.astype(BF16), b.astype(BF16), preferred_element_type=F32)


def _mm_nt(a, b):
    return lax.dot_general(a.astype(BF16), b.astype(BF16), (((1,), (1,)), ((), ())),
                           preferred_element_type=F32)


def _split(x, terms):
    parts = []
    r = x
    for _ in range(terms):
        hi = r.astype(BF16)
        parts.append(hi)
        r = r - hi.astype(F32)
    return parts


def _sel_mm(sel, x, terms=3):
    acc = None
    for part in _split(x, terms):
        d = jnp.dot(sel, part, preferred_element_type=F32)
        acc = d if acc is None else acc + d
    return acc


def _mm_sel(x, sel, terms=2):
    acc = None
    for part in _split(x, terms):
        d = jnp.dot(part, sel, preferred_element_type=F32)
        acc = d if acc is None else acc + d
    return acc


def _mm_hi(a, b):
    a_hi, a_lo = _split(a, 2)
    b_hi, b_lo = _split(b, 2)
    return (jnp.dot(a_hi, b_hi, preferred_element_type=F32)
            + jnp.dot(a_lo, b_hi, preferred_element_type=F32)
            + jnp.dot(a_hi, b_lo, preferred_element_type=F32))


def _iota2(shape, axis):
    return lax.broadcasted_iota(jnp.int32, shape, axis)


def _block_prefix(n, block):
    t = _iota2((n, n), 0)
    s = _iota2((n, n), 1)
    return jnp.where(((t // block) == (s // block)) & (s <= t), 1.0, 0.0).astype(BF16)


def _chunk_last(x, chunk):
    cols = x.shape[1]
    return jnp.concatenate(
        [jnp.broadcast_to(x[r + chunk - 1:r + chunk, :], (chunk, cols)) for r in range(0, x.shape[0], chunk)],
        axis=0)


def _block_ones(n, block):
    t = _iota2((n, n), 0)
    s = _iota2((n, n), 1)
    return jnp.where((t // block) == (s // block), 1.0, 0.0).astype(BF16)


def _nilpotent_inverses(xs, n, refine):
    shape = xs[0].shape
    eye = jnp.where(_iota2(shape, 0) == _iota2(shape, 1), 1.0, 0.0).astype(F32)
    ts = [eye + x for x in xs]
    ps = list(xs)
    k = 2
    while k < n:
        ps = [_mm(p, p) for p in ps]
        ts = [t + _mm(t, p) for t, p in zip(ts, ps)]
        k *= 2
    if refine:
        res = [(eye - t) + _mm_hi(x, t) for x, t in zip(xs, ts)]
        ts = [t + _mm(t, r) for t, r in zip(ts, res)]
    return ts


def _norm_proj_kernel(h_ref, nw_ref, wa_ref, wb_ref, za_ref, zb_ref):
    xn = _rms(h_ref[...], nw_ref[...]).astype(BF16)
    za_ref[...] = jnp.dot(xn, wa_ref[...], preferred_element_type=F32)
    zb_ref[...] = jnp.dot(xn, wb_ref[...], preferred_element_type=F32)


def _const_spec(shape):
    nd = len(shape)
    return pl.BlockSpec(shape, lambda *_: (0,) * nd, pipeline_mode=pl.Buffered(1))


def _norm_proj(h, norm_w, wa, wb, tm):
    t, d = h.shape
    fa, fb = wa.shape[1], wb.shape[1]
    return pl.pallas_call(
        _norm_proj_kernel,
        grid=(t // tm,),
        in_specs=[
            pl.BlockSpec((tm, d), lambda i: (i, 0)),
            _const_spec((1, d)),
            _const_spec((d, fa)),
            _const_spec((d, fb)),
        ],
        out_specs=[
            pl.BlockSpec((tm, fa), lambda i: (i, 0)),
            pl.BlockSpec((tm, fb), lambda i: (i, 0)),
        ],
        out_shape=[jax.ShapeDtypeStruct((t, fa), F32), jax.ShapeDtypeStruct((t, fb), F32)],
        compiler_params=pltpu.CompilerParams(
            dimension_semantics=("parallel",), vmem_limit_bytes=VMEM_LIMIT),
        name="norm_proj",
    )(h, norm_w.reshape(1, d), wa, wb)


def _post_kernel(final_norm, h_ref, ma_ref, mb_ref, p_ref, wo_ref, nf_ref, wg_ref, wu_ref, wd_ref,
                 npl_ref, wpg_ref, wpu_ref, nfin_ref, out_ref):
    h = h_ref[...]
    mix = (jnp.dot(ma_ref[...], wo_ref[0:GROUP_WIDTH, :], preferred_element_type=F32)
           + jnp.dot(mb_ref[...], wo_ref[GROUP_WIDTH:2 * GROUP_WIDTH, :], preferred_element_type=F32))
    h1 = h + mix
    hn = _rms(h1, nf_ref[...]).astype(BF16)
    g = jnp.dot(hn, wg_ref[...], preferred_element_type=F32)
    u = jnp.dot(hn, wu_ref[...], preferred_element_type=F32)
    act = (_silu(g) * u).astype(BF16)
    h2 = h1 + jnp.dot(act, wd_ref[...], preferred_element_type=F32)
    hp = _rms(h2, npl_ref[...]).astype(BF16)
    gate = _sigmoid(jnp.dot(hp, wpg_ref[...], preferred_element_type=F32))
    up = jnp.dot(p_ref[...].astype(BF16), wpu_ref[...], preferred_element_type=F32)
    h3 = h2 + gate * up
    if final_norm:
        h3 = _rms(h3, nfin_ref[...])
    out_ref[...] = h3


def _post(h, ma, mb, p_all, layer, wo, nf, wg, wu, wd, npl, wpg, wpu, nfin, final_norm, tm):
    t, d = h.shape
    row = lambda w: pl.BlockSpec((tm, w), lambda i: (i, 0))
    return pl.pallas_call(
        functools.partial(_post_kernel, final_norm),
        grid=(t // tm,),
        in_specs=[
            row(d), row(GROUP_WIDTH), row(GROUP_WIDTH),
            pl.BlockSpec((None, tm, PLE_DIM), lambda i: (layer, i, 0)),
            _const_spec(wo.shape), _const_spec((1, d)),
            _const_spec(wg.shape), _const_spec(wu.shape), _const_spec(wd.shape),
            _const_spec((1, d)), _const_spec(wpg.shape), _const_spec(wpu.shape),
            _const_spec((1, d)),
        ],
        out_specs=row(d),
        out_shape=jax.ShapeDtypeStruct((t, d), F32),
        compiler_params=pltpu.CompilerParams(
            dimension_semantics=("parallel",), vmem_limit_bytes=VMEM_LIMIT),
        name="post_mixer",
    )(h, ma, mb, p_all, wo, nf.reshape(1, d), wg, wu, wd, npl.reshape(1, d), wpg, wpu, nfin.reshape(1, d))


def _causal_conv(xbuf_ref, x, w_ref, ts):
    xcat = jnp.concatenate([xbuf_ref[...], x], axis=0)
    acc = w_ref[CONV_TAPS - 1:CONV_TAPS, :] * x
    for back in range(1, CONV_TAPS):
        shifted = pltpu.roll(xcat, back, 0)[HALO:HALO + ts, :]
        acc = acc + w_ref[CONV_TAPS - 1 - back:CONV_TAPS - back, :] * shifted
    xbuf_ref[...] = x[ts - HALO:ts, :]
    return acc


def _proj_rglru_kernel(tm, tiles_per_seq, h_ref, nw_ref, wa_ref, wb_ref, cw_ref, cb_ref, wri_ref, bri_ref,
                       lam_ref, out_ref, zb_ref, xbuf_ref, hc_ref):
    w = GROUP_WIDTH
    first = (pl.program_id(0) % tiles_per_seq) == 0

    @pl.when(first)
    def _():
        xbuf_ref[0:HALO, :] = jnp.zeros((HALO, w), F32)
        hc_ref[...] = jnp.zeros_like(hc_ref)

    xn = _rms(h_ref[...], nw_ref[...]).astype(BF16)
    z_rg = jnp.dot(xn, wa_ref[...], preferred_element_type=F32)
    xa = z_rg[:, 0:w]
    ya = z_rg[:, w:2 * w]
    x = _causal_conv(xbuf_ref, xa, cw_ref, tm) + cb_ref[...]
    ri = jnp.dot(x.astype(BF16), wri_ref[...], preferred_element_type=F32) + bri_ref[...]
    zb_ref[...] = jnp.dot(xn, wb_ref[...], preferred_element_type=F32)
    sp_lam = _softplus(-lam_ref[...])
    lanes = HEAD128
    row = _iota2((RG_ROWS, lanes), 0)
    for l0 in range(0, w, lanes):
        ls = slice(l0, l0 + lanes)
        carry = hc_ref[:, ls]
        for r0 in range(0, tm, RG_ROWS):
            rs = slice(r0, r0 + RG_ROWS)
            r = _sigmoid(ri[rs, l0:l0 + lanes])
            i = _sigmoid(ri[rs, w + l0:w + l0 + lanes])
            log_a = -RG_C * r * sp_lam[:, ls]
            a = jnp.exp(log_a)
            mult = jnp.sqrt(-jnp.tanh(log_a) * (a * a + 1.0))
            if r0 == 0:
                mult = jnp.where((row == 0) & first, 1.0, mult)
            acc_a, acc_h = a, mult * (i * x[rs, ls])
            d = 1
            while d < RG_ROWS:
                if d % HALO == 0:
                    sh_a = jnp.concatenate([jnp.ones((d, lanes), F32), acc_a[0:RG_ROWS - d, :]], axis=0)
                    sh_h = jnp.concatenate([jnp.zeros((d, lanes), F32), acc_h[0:RG_ROWS - d, :]], axis=0)
                else:
                    sh_a = jnp.where(row >= d, pltpu.roll(acc_a, d, 0), 1.0)
                    sh_h = jnp.where(row >= d, pltpu.roll(acc_h, d, 0), 0.0)
                acc_h = acc_a * sh_h + acc_h
                acc_a = acc_a * sh_a
                d *= 2
            h = acc_h + acc_a * carry
            carry = h[RG_ROWS - 1:RG_ROWS, :]
            out_ref[rs, ls] = (h * _gelu_tanh(ya[rs, ls])).astype(out_ref.dtype)
        hc_ref[:, ls] = carry


def _proj_rglru(h, norm_w, wa, wb, cw, cb, wri, bri, lam, tm, seq_len):
    t, d = h.shape
    w = GROUP_WIDTH
    fb = wb.shape[1]
    return pl.pallas_call(
        functools.partial(_proj_rglru_kernel, tm, seq_len // tm),
        grid=(t // tm,),
        in_specs=[
            pl.BlockSpec((tm, d), lambda i: (i, 0)),
            _const_spec((1, d)), _const_spec(wa.shape), _const_spec(wb.shape),
            _const_spec((CONV_TAPS, w)), _const_spec((1, w)),
            _const_spec((w, 2 * w)), _const_spec((1, 2 * w)), _const_spec((1, w)),
        ],
        out_specs=[
            pl.BlockSpec((tm, w), lambda i: (i, 0)),
            pl.BlockSpec((tm, fb), lambda i: (i, 0)),
        ],
        out_shape=[jax.ShapeDtypeStruct((t, w), BF16), jax.ShapeDtypeStruct((t, fb), F32)],
        scratch_shapes=[pltpu.VMEM((HALO, w), F32), pltpu.VMEM((1, w), F32)],
        compiler_params=pltpu.CompilerParams(
            dimension_semantics=("arbitrary",), vmem_limit_bytes=VMEM_LIMIT),
        name="proj_rglru",
    )(h, norm_w.reshape(1, d), wa, wb, cw, cb, wri, bri, lam)


def _gdn_kernel(ts, z_ref, cw_ref, alog_ref, dtb_ref, nw_ref, out_ref, xbuf_ref, st_ref):
    w = GROUP_WIDTH
    hd = HEAD128
    nh = w // hd
    cs = GDN_CHUNK
    nchunk = ts // cs
    tstep = pl.program_id(1)

    @pl.when(tstep == 0)
    def _():
        xbuf_ref[0:HALO, :] = jnp.zeros((HALO, 3 * w), F32)
        st_ref[...] = jnp.zeros_like(st_ref)

    qkv = _silu(_causal_conv(xbuf_ref, z_ref[0, :, 0:3 * w], cw_ref, ts))
    seg_w = 2 * hd
    ones_bd = _block_ones(seg_w, hd)

    def head_sum(x):
        xb = x.astype(BF16)
        return jnp.concatenate(
            [jnp.dot(xb[:, c0:c0 + seg_w], ones_bd, preferred_element_type=F32)
             for c0 in range(0, x.shape[1], seg_w)], axis=1)

    qk = qkv[:, 0:2 * w]
    qk = qk * lax.rsqrt(head_sum(qk * qk) + L2_EPS)
    zg = z_ref[0, :, 3 * w:4 * w]
    ab = z_ref[0, :, 4 * w:4 * w + hd]
    g_n = -jnp.exp(alog_ref[...]) * _softplus(ab + dtb_ref[...])
    gc_n = _sel_mm(_block_prefix(ts, cs), g_n)
    src = _iota2((hd, w), 0)
    dst_head = _iota2((hd, w), 1) // hd
    gc_all = _mm_sel(gc_n, jnp.where(src == dst_head, 1.0, 0.0).astype(BF16), terms=3)
    beta = _mm_sel(_sigmoid(ab), jnp.where(src == dst_head + nh, 1.0, 0.0).astype(BF16), terms=3)
    gs_all = _chunk_last(gc_all, cs) - gc_all
    eg_all = jnp.exp(gc_all)
    scale = hd ** -0.5
    tt = _iota2((cs, cs), 0)
    ss = _iota2((cs, cs), 1)

    items = [(h, c) for c in range(nchunk) for h in range(nh)]
    qs, ks, vbs, kbegs, decs, kdts, glast = [], [], [], [], [], [], []
    for h, c in items:
        lo = h * hd
        rows = slice(c * cs, (c + 1) * cs)
        hs = slice(lo, lo + hd)
        qh = qk[rows, lo:lo + hd] * scale
        kh = qk[rows, w + lo:w + lo + hd]
        vh = qkv[rows, 2 * w + lo:2 * w + lo + hd]
        gc = gc_all[rows, hs]
        bt = beta[rows, hs]
        eg = eg_all[rows, hs]
        kb = kh * bt
        diff = gc - gc.T
        decs.append(jnp.exp(jnp.where(ss <= tt, diff, -jnp.inf)))
        qs.append(qh * eg)
        ks.append((kh, kb, qh))
        vbs.append(vh * bt)
        kbegs.append(kb * eg)
        kdts.append((kh * jnp.exp(gs_all[rows, hs])).T)
        glast.append(jnp.exp(gc_all[(c + 1) * cs - 1:(c + 1) * cs, hs]))
    a_mats = [jnp.where(ss < tt, _mm_nt(kb, kh) * dec, 0.0) for (kh, kb, _), dec in zip(ks, decs)]
    qks = [_mm_nt(qh, kh) * dec for (kh, _, qh), dec in zip(ks, decs)]
    tinvs = _nilpotent_inverses([-a for a in a_mats], cs, refine=True)
    sols = [_mm(t, jnp.concatenate([vb, kbeg], axis=1)) for t, vb, kbeg in zip(tinvs, vbs, kbegs)]
    us = [sol[:, 0:hd] for sol in sols]
    ws = [sol[:, hd:2 * hd] for sol in sols]

    states = [st_ref[h] for h in range(nh)]
    outs = [[] for _ in range(nh)]
    for c in range(nchunk):
        idx = [c * nh + h for h in range(nh)]
        proj = [_mm(jnp.concatenate([ws[i], qs[i]], axis=0), states[h]) for h, i in enumerate(idx)]
        v_new = [us[i] - pr[0:cs] for i, pr in zip(idx, proj)]
        o = [pr[cs:2 * cs] + _mm(qks[i], vn) for i, pr, vn in zip(idx, proj, v_new)]
        states = [glast[i] * states[h] + _mm(kdts[i], vn) for (h, i), vn in zip(enumerate(idx), v_new)]
        for h in range(nh):
            outs[h].append(o[h])
    for h in range(nh):
        lo = h * hd
        st_ref[h] = states[h]
        o = jnp.concatenate(outs[h], axis=0) if nchunk > 1 else outs[h][0]
        o = _rms(o, nw_ref[...]) * _silu(zg[:, lo:lo + hd])
        out_ref[0, :, lo:lo + hd] = o.astype(out_ref.dtype)


def _gdn(z, cw, alog, dtb, nw, ts):
    b, s, _ = z.shape
    w = GROUP_WIDTH
    nh = w // HEAD128
    return pl.pallas_call(
        functools.partial(_gdn_kernel, ts),
        grid=(b, s // ts),
        in_specs=[
            pl.BlockSpec((1, ts, 4 * w + HEAD128), lambda bi, ti: (bi, ti, 0)),
            _const_spec((CONV_TAPS, 3 * w)), _const_spec((1, HEAD128)), _const_spec((1, HEAD128)),
            _const_spec((1, HEAD128)),
        ],
        out_specs=pl.BlockSpec((1, ts, w), lambda bi, ti: (bi, ti, 0)),
        out_shape=jax.ShapeDtypeStruct((b, s, w), BF16),
        scratch_shapes=[pltpu.VMEM((HALO, 3 * w), F32), pltpu.VMEM((nh, HEAD128, HEAD128), F32)],
        compiler_params=pltpu.CompilerParams(
            dimension_semantics=("parallel", "arbitrary"), vmem_limit_bytes=VMEM_LIMIT),
        name="gated_deltanet",
    )(z, cw, alog, dtb, nw)


def _hgrn2_kernel(ts, layer_j, z_ref, lbraw_ref, nw_ref, out_ref, st_ref, o_ref):
    w = GROUP_WIDTH
    hd = HEAD128
    nh = w // hd
    tstep = pl.program_id(1)

    @pl.when(tstep == 0)
    def _():
        st_ref[...] = jnp.zeros_like(st_ref)

    lbraw = lbraw_ref[...]
    e = jnp.exp(lbraw - jnp.max(lbraw, axis=0, keepdims=True))
    soft = e / jnp.sum(e, axis=0, keepdims=True)
    lb = jnp.sum(soft[0:layer_j + 1, :], axis=0, keepdims=True) - soft[0:1, :]
    lb = jnp.maximum(lb, 0.0)

    q = _silu(z_ref[0, :, 0:w])
    f = z_ref[0, :, w:2 * w]
    v = z_ref[0, :, 2 * w:3 * w]
    gate = z_ref[0, :, 3 * w:4 * w]
    kin = (1.0 - lb) * _sigmoid(-f)
    la = jnp.log(lb)
    lbm = jnp.log1p(-lb) - _softplus(-f)
    mx = jnp.maximum(la, lbm)
    log_f = mx + jnp.log(1.0 + jnp.exp(-jnp.abs(la - lbm)))

    cs = HG_CHUNK
    nchunk = ts // cs
    b = _sel_mm(_block_prefix(ts, cs), log_f)
    b2 = b * math.log2(math.e)
    row = _iota2((ts, w), 0)
    tt = _iota2((cs, cs), 0)
    ss = _iota2((cs, cs), 1)
    items = [(c, h) for c in range(nchunk) for h in range(nh)]
    sl = lambda c, h: (slice(c * cs, (c + 1) * cs), slice(h * hd, (h + 1) * hd))

    scores = [jnp.where(tt == ss, _mm_nt(q[sl(c, h)], kin[sl(c, h)]), 0.0) for c, h in items]
    block_end = b2
    m = 1
    while m < cs:
        prev_end = pltpu.roll(block_end, m, 0)
        q_m = q * jnp.exp2(b2 - prev_end)
        k_m = kin * jnp.exp2(block_end - b2)
        pair = ((tt // m) % 2 == 1) & ((ss // m) == (tt // m) - 1)
        scores = [jnp.where(pair, _mm_nt(q_m[sl(c, h)], k_m[sl(c, h)]), sc)
                  for sc, (c, h) in zip(scores, items)]
        block_end = jnp.where((row // m) % 2 == 0, pltpu.roll(block_end, ts - m, 0), block_end)
        m *= 2
    o_intra = [_mm(sc, v[sl(c, h)]) for sc, (c, h) in zip(scores, items)]

    btot = _chunk_last(b, cs)
    qe = q * jnp.exp(b)
    k_dec = kin * jnp.exp(btot - b)
    dec = jnp.exp(btot)
    incs = [_mm(v[sl(c, h)].T, k_dec[sl(c, h)]) for c, h in items]
    states = [st_ref[h] for h in range(nh)]
    for c in range(nchunk):
        for h in range(nh):
            i = c * nh + h
            o_ref[c * cs:(c + 1) * cs, h * hd:(h + 1) * hd] = o_intra[i] + _mm_nt(qe[sl(c, h)], states[h])
        states = [states[h] * dec[c * cs:c * cs + 1, h * hd:(h + 1) * hd] + incs[c * nh + h]
                  for h in range(nh)]
    for h in range(nh):
        st_ref[h] = states[h]

    out_ref[0] = (_rms(o_ref[...], nw_ref[...]) * _sigmoid(gate)).astype(out_ref.dtype)


def _hgrn2(z, lbraw, nw, layer_j, ts):
    b, s, _ = z.shape
    w = GROUP_WIDTH
    nh = w // HEAD128
    return pl.pallas_call(
        functools.partial(_hgrn2_kernel, ts, layer_j),
        grid=(b, s // ts),
        in_specs=[
            pl.BlockSpec((1, ts, 4 * w), lambda bi, ti: (bi, ti, 0)),
            _const_spec(lbraw.shape), _const_spec((1, w)),
        ],
        out_specs=pl.BlockSpec((1, ts, w), lambda bi, ti: (bi, ti, 0)),
        out_shape=jax.ShapeDtypeStruct((b, s, w), BF16),
        scratch_shapes=[pltpu.VMEM((nh, HEAD128, HEAD128), F32), pltpu.VMEM((ts, w), F32)],
        compiler_params=pltpu.CompilerParams(
            dimension_semantics=("parallel", "arbitrary"), vmem_limit_bytes=VMEM_LIMIT),
        name="hgrn2",
    )(z, lbraw, nw)


def _rwkv7_kernel(ts, z_ref, mu_ref, w0_ref, w2_ref, a0_ref, a2_ref, g2_ref, kk_ref, ka_ref, rk_ref,
                  lnw_ref, lnb_ref, out_ref, xbuf_ref, st_ref):
    w = GROUP_WIDTH
    lp = R7_LORA_PAD
    ch = R7_CHUNK
    npair = w // (2 * R7_N)
    tstep = pl.program_id(1)

    @pl.when(tstep == 0)
    def _():
        xbuf_ref[0:HALO, :] = jnp.zeros((HALO, R7_IN_PAD), F32)
        st_ref[...] = jnp.zeros_like(st_ref)

    pw = 2 * R7_N
    n = R7_N
    gch = min(R7_GROUP, ts // ch)
    grows = gch * ch
    lane = _iota2((ch, pw), 1)
    in_h0 = lane < n
    t_row = _iota2((ch, pw), 0)
    s_col = lane % n
    seg_w = 2 * HEAD128
    ones_bd = _block_ones(seg_w, R7_N)

    def head_sum(x):
        xb = x.astype(BF16)
        return jnp.concatenate(
            [jnp.dot(xb[:, c0:c0 + seg_w], ones_bd, preferred_element_type=F32) for c0 in range(0, w, seg_w)],
            axis=1)

    tri_incl = _block_prefix(grows, ch)
    items = [(c, p) for c in range(gch) for p in range(npair)]
    sl = lambda c, p: (slice(c * ch, (c + 1) * ch), slice(p * pw, (p + 1) * pw))

    def stack_heads(x):
        return jnp.concatenate([jnp.where(in_h0, x, 0.0), jnp.where(in_h0, 0.0, x)], axis=0)

    z_all = z_ref[0]
    z_prev_all = pltpu.roll(jnp.concatenate([xbuf_ref[0:HALO, :], z_all], axis=0), 1, 0)[HALO:HALO + ts, :]
    states = [st_ref[p] for p in range(npair)]
    for g0 in range(0, ts, grows):
        z = z_all[g0:g0 + grows, :]
        z_prev = z_prev_all[g0:g0 + grows, :]
        zd = z + mu_ref[...] * (z_prev - z)
        r = zd[:, 0:w]
        k = zd[:, w:2 * w]
        v = zd[:, 2 * w:3 * w]
        w_l = zd[:, 3 * w:3 * w + lp]
        a_l = zd[:, 3 * w + lp:3 * w + 2 * lp]
        g_l = zd[:, 3 * w + 2 * lp:3 * w + 3 * lp]
        log_w = -jnp.exp(-_softplus(-(w0_ref[...] + _mm(jnp.tanh(w_l), w2_ref[...]))) - 0.5)
        a = _sigmoid(a0_ref[...] + _mm(a_l, a2_ref[...]))
        gate = _mm(_sigmoid(g_l), g2_ref[...])
        kk = k * kk_ref[...]
        kk = kk * lax.rsqrt(head_sum(kk * kk) + L2_EPS)
        k = k * (1.0 + (a - 1.0) * ka_ref[...])
        cum = _sel_mm(tri_incl, log_w)
        total = _chunk_last(cum, ch)
        e_inv = jnp.exp(-cum)
        r_t = r * jnp.exp(cum)
        a_t = -kk * jnp.exp(cum - log_w)
        kka = kk * a
        b_h = kka * e_inv
        k_h = k * e_inv
        e_suf = jnp.exp(total - cum)
        b_dc = kka * e_suf
        k_dc = k * e_suf
        d_end = jnp.exp(total)

        v_ss = [stack_heads(v[sl(c, p)]) for c, p in items]
        grams = [
            _mm_nt(jnp.concatenate([a_t[sl(c, p)], r_t[sl(c, p)]], axis=0),
                   jnp.concatenate([stack_heads(b_h[sl(c, p)]), stack_heads(k_h[sl(c, p)])], axis=0))
            for c, p in items]
        a_abs = [jnp.where(s_col < t_row, g[0:ch, 0:pw], 0.0) for g in grams]
        a_rbs = [jnp.where(s_col <= t_row, g[ch:2 * ch, 0:pw], 0.0) for g in grams]
        a_k = [jnp.concatenate([jnp.where(s_col < t_row, g[0:ch, pw:2 * pw], 0.0),
                                jnp.where(s_col <= t_row, g[ch:2 * ch, pw:2 * pw], 0.0)], axis=0)
               for g in grams]
        eye_cat = jnp.where(s_col == t_row, 1.0, 0.0).astype(F32)
        t_cats = [eye_cat + x for x in a_abs]
        p_cats = [jnp.dot(p.astype(BF16), stack_heads(p).astype(BF16), preferred_element_type=F32)
                  for p in a_abs]
        k_pow = 2
        while 2 * k_pow < ch:
            both = [jnp.dot(jnp.concatenate([p, t], axis=0).astype(BF16), stack_heads(p).astype(BF16),
                            preferred_element_type=F32) for p, t in zip(p_cats, t_cats)]
            p_cats = [x[0:ch] for x in both]
            t_cats = [t + x[ch:2 * ch] for t, x in zip(t_cats, both)]
            k_pow *= 2
        t_cats = [t + jnp.dot(t.astype(BF16), stack_heads(p).astype(BF16), preferred_element_type=F32)
                  for p, t in zip(p_cats, t_cats)]
        tinvs = [stack_heads(t) for t in t_cats]
        kv = [_mm(x, vs) for x, vs in zip(a_k, v_ss)]
        w_ss = [_mm(t, jnp.concatenate([stack_heads(a_t[sl(c, p)]), stack_heads(x[0:ch])], axis=1))
                for t, x, (c, p) in zip(tinvs, kv, items)]
        ry = [_mm(x, w_s) for x, w_s in zip(a_rbs, w_ss)]
        r_eff = [r_t[sl(c, p)] + x[:, 0:pw] for x, (c, p) in zip(ry, items)]
        y0 = [x[:, pw:2 * pw] + y[ch:2 * ch] for x, y in zip(ry, kv)]
        bdc_ss = [stack_heads(b_dc[sl(c, p)]) for c, p in items]
        phi_t = [_mm(w_s[:, 0:pw].T, bd).astype(BF16) for w_s, bd in zip(w_ss, bdc_ss)]
        gamma_t = [_mm(jnp.concatenate([w_s[:, pw:2 * pw].T, vs.T], axis=1),
                       jnp.concatenate([bd, stack_heads(k_dc[sl(c, p)])], axis=0))
                   for w_s, vs, bd, (c, p) in zip(w_ss, v_ss, bdc_ss, items)]

        ys = [[] for _ in range(npair)]
        for c in range(gch):
            st_bs = [st.astype(BF16) for st in states]
            states = [states[p] * d_end[c * ch:c * ch + 1, p * pw:(p + 1) * pw]
                      + jnp.dot(st_bs[p], phi_t[c * npair + p], preferred_element_type=F32)
                      + gamma_t[c * npair + p] for p in range(npair)]
            for p in range(npair):
                i = c * npair + p
                ys[p].append(_mm_nt(r_eff[i], st_bs[p]) + y0[i])
        y = jnp.concatenate([jnp.concatenate(yp, axis=0) if gch > 1 else yp[0] for yp in ys], axis=1)

        mean = head_sum(y) * (1.0 / R7_N)
        yc = y - mean
        var = head_sum(yc * yc) * (1.0 / R7_N)
        yn = yc * lax.rsqrt(var + GN_EPS) * lnw_ref[...] + lnb_ref[...]
        bonus = head_sum(r * k * rk_ref[...]) * v
        out_ref[0, g0:g0 + grows, :] = ((yn + bonus) * gate).astype(out_ref.dtype)
    for p in range(npair):
        st_ref[p] = states[p]
    xbuf_ref[0:HALO, :] = z_all[ts - HALO:ts, :]


def _rwkv7(z, mu, w0, w2, a0, a2, g2, k_k, k_a, r_k, ln_w, ln_b, ts):
    b, s, fin = z.shape
    w = GROUP_WIDTH
    npair = w // (2 * R7_N)
    vec = _const_spec((1, w))
    return pl.pallas_call(
        functools.partial(_rwkv7_kernel, ts),
        grid=(b, s // ts),
        in_specs=[
            pl.BlockSpec((1, ts, fin), lambda bi, ti: (bi, ti, 0)),
            _const_spec((1, fin)), vec, _const_spec(w2.shape), vec, _const_spec(a2.shape),
            _const_spec(g2.shape), vec, vec, vec, vec, vec,
        ],
        out_specs=pl.BlockSpec((1, ts, w), lambda bi, ti: (bi, ti, 0)),
        out_shape=jax.ShapeDtypeStruct((b, s, w), BF16),
        scratch_shapes=[pltpu.VMEM((HALO, fin), F32), pltpu.VMEM((npair, 2 * R7_N, 2 * R7_N), F32)],
        compiler_params=pltpu.CompilerParams(
            dimension_semantics=("parallel", "arbitrary"), vmem_limit_bytes=VMEM_LIMIT),
        name="rwkv7",
    )(z, mu, w0, w2, a0, a2, g2, k_k, k_a, r_k, ln_w, ln_b)


def _even_weights(w_in, rg_w_r, rg_w_i, rg_b_r, rg_b_i, gdn_a_log, gdn_dt_bias):
    w = GROUP_WIDTH
    nh = w // HEAD128
    w_rg = w_in[:, 0:2 * w]
    w_gdn = jnp.pad(w_in[:, 2 * w:6 * w + 2 * nh], ((0, 0), (0, HEAD128 - 2 * nh)))
    wri = jnp.concatenate([jax.scipy.linalg.block_diag(*rg_w_r), jax.scipy.linalg.block_diag(*rg_w_i)], axis=1)
    bri = jnp.concatenate([rg_b_r, rg_b_i]).reshape(1, 2 * w)
    alog = jnp.pad(gdn_a_log, (0, HEAD128 - nh)).reshape(1, HEAD128)
    dtb = jnp.pad(gdn_dt_bias, (0, HEAD128 - nh)).reshape(1, HEAD128)
    return w_rg.astype(BF16), w_gdn.astype(BF16), wri.astype(BF16), bri, alog, dtb


def _pad_groups(x, axis):
    w = GROUP_WIDTH
    sizes = (w, w, w, 64, 64, 128)
    parts = []
    start = 0
    for sz in sizes:
        piece = lax.slice_in_dim(x, start, start + sz, axis=axis)
        pad = (-sz) % R7_LORA_PAD
        if pad:
            cfg = [(0, 0)] * x.ndim
            cfg[axis] = (0, pad)
            piece = jnp.pad(piece, cfg)
        parts.append(piece)
        start += sz
    return jnp.concatenate(parts, axis=axis)


def _pad_rows(x, rows):
    return jnp.pad(x, ((0, rows - x.shape[0]), (0, 0)))


def _tiles(s):
    tm_proj = 512 if s % 512 == 0 else 256
    tm_post = 512
    ts = next(c for c in (512, 256, 128) if s % c == 0)
    ts_hg = min(ts, 256)
    return tm_proj, tm_post, ts, ts_hg


def kernel(x, p, norm_mix, norm_ffn, norm_ple, norm_final, e_w_in, rg_conv_w, rg_conv_b, rg_w_r, rg_b_r, rg_w_i, rg_b_i, rg_lambda, gdn_conv_w, gdn_a_log, gdn_dt_bias, gdn_norm_w, e_w_out, o_w_in, hg_lower_bounds, hg_norm_w, r7_mu, r7_w0, r7_w2, r7_a0, r7_a2, r7_g2, r7_k_k, r7_k_a, r7_r_k, r7_ln_w, r7_ln_b, o_w_out, ffn_w_gate, ffn_w_up, ffn_w_down, ple_w_up, ple_w_gate):
    b, s, d = x.shape
    depth = p.shape[0]
    t = b * s
    w = GROUP_WIDTH
    tm_proj, tm_post, ts, ts_hg = _tiles(s)
    h = x.reshape(t, d)
    row = lambda vec: vec.reshape(1, -1)
    for layer in range(depth):
        j = layer // 2
        if layer % 2 == 0:
            w_rg, w_gdn, wri, bri, alog, dtb = _even_weights(
                e_w_in[j], rg_w_r[j], rg_w_i[j], rg_b_r[j], rg_b_i[j], gdn_a_log[j], gdn_dt_bias[j])
            out_a, z_b = _proj_rglru(h, norm_mix[layer], w_rg, w_gdn, rg_conv_w[j], row(rg_conv_b[j]),
                                     wri, bri, row(rg_lambda[j]), tm_proj, s)
            out_b = _gdn(z_b.reshape(b, s, -1), gdn_conv_w[j], alog, dtb, row(gdn_norm_w[j]), ts)
            w_out = e_w_out[j]
        else:
            hg_in = 4 * w
            w_hg = o_w_in[j][:, 0:hg_in].astype(BF16)
            w_r7 = _pad_groups(o_w_in[j][:, hg_in:], axis=1).astype(BF16)
            z_a, z_b = _norm_proj(h, norm_mix[layer], w_hg, w_r7, tm_proj)
            out_a = _hgrn2(z_a.reshape(b, s, -1), hg_lower_bounds, row(hg_norm_w[j]), j, ts_hg)
            out_b = _rwkv7(z_b.reshape(b, s, -1), row(_pad_groups(r7_mu[j], axis=0)), row(r7_w0[j]),
                           _pad_rows(r7_w2[j], R7_LORA_PAD).astype(BF16), row(r7_a0[j]),
                           _pad_rows(r7_a2[j], R7_LORA_PAD).astype(BF16), r7_g2[j].astype(BF16),
                           row(r7_k_k[j]), row(r7_k_a[j]), row(r7_r_k[j].reshape(-1)),
                           row(r7_ln_w[j]), row(r7_ln_b[j]), ts)
            w_out = o_w_out[j]
        h = _post(h, out_a.reshape(t, w), out_b.reshape(t, w), p.reshape(depth, t, PLE_DIM), layer,
                  w_out.astype(BF16), norm_ffn[layer], ffn_w_gate[layer].astype(BF16),
                  ffn_w_up[layer].astype(BF16), ffn_w_down[layer].astype(BF16), norm_ple[layer],
                  ple_w_gate[layer].astype(BF16), ple_w_up[layer].astype(BF16), norm_final,
                  layer == depth - 1, tm_post)
    return h.reshape(b, s, d)
```

```python
import functools
import math

import jax
import jax.numpy as jnp
from jax import lax
from jax.experimental import pallas as pl
from jax.experimental.pallas import tpu as pltpu

F32 = jnp.float32
BF16 = jnp.bfloat16

PLE_DIM = 256
GROUP_WIDTH = 512
RMS_EPS = 1e-6
L2_EPS = 1e-6
GN_EPS = 64e-5
RG_C = 8.0
RG_ROWS = 64
CONV_TAPS = 4
HEAD128 = 128
R7_N = 64
R7_LORA_PAD = 128
R7_IN_PAD = 3 * GROUP_WIDTH + 3 * R7_LORA_PAD

GDN_CHUNK = 128
HG_CHUNK = 128
R7_CHUNK = 64
R7_GROUP = 4
HALO = 8

VMEM_LIMIT = 56 * 1024 * 1024


def _sigmoid(x):
    return 1.0 / (1.0 + jnp.exp2(x * -math.log2(math.e)))


def _softplus(x):
    return jnp.maximum(x, 0.0) + jnp.log(1.0 + jnp.exp(-jnp.abs(x)))


def _silu(x):
    return x * _sigmoid(x)


def _gelu_tanh(x):
    c = math.sqrt(2.0 / math.pi)
    return 0.5 * x * (1.0 + jnp.tanh(c * (x + 0.044715 * (x * x * x))))


def _rms(x, w):
    ms = jnp.mean(x * x, axis=-1, keepdims=True)
    return x * lax.rsqrt(ms + RMS_EPS) * w


def _mm(a, b):
    return jnp.dot(a.astype(BF16), b.astype(BF16), preferred_element_type=F32)


def _mm_nt(a, b):
    return lax.dot_general(a.astype(BF16), b.astype(BF16), (((1,), (1,)), ((), ())),
                           preferred_element_type=F32)


def _split(x, terms):
    parts = []
    r = x
    for _ in range(terms):
        hi = r.astype(BF16)
        parts.append(hi)
        r = r - hi.astype(F32)
    return parts


def _sel_mm(sel, x, terms=3):
    acc = None
    for part in _split(x, terms):
        d = jnp.dot(sel, part, preferred_element_type=F32)
        acc = d if acc is None else acc + d
    return acc


def _mm_sel(x, sel, terms=2):
    acc = None
    for part in _split(x, terms):
        d = jnp.dot(part, sel, preferred_element_type=F32)
        acc = d if acc is None else acc + d
    return acc


def _mm_hi(a, b):
    a_hi, a_lo = _split(a, 2)
    b_hi, b_lo = _split(b, 2)
    return (jnp.dot(a_hi, b_hi, preferred_element_type=F32)
            + jnp.dot(a_lo, b_hi, preferred_element_type=F32)
            + jnp.dot(a_hi, b_lo, preferred_element_type=F32))


def _iota2(shape, axis):
    return lax.broadcasted_iota(jnp.int32, shape, axis)


def _block_prefix(n, block):
    t = _iota2((n, n), 0)
    s = _iota2((n, n), 1)
    return jnp.where(((t // block) == (s // block)) & (s <= t), 1.0, 0.0).astype(BF16)


def _chunk_last(x, chunk):
    cols = x.shape[1]
    return jnp.concatenate(
        [jnp.broadcast_to(x[r + chunk - 1:r + chunk, :], (chunk, cols)) for r in range(0, x.shape[0], chunk)],
        axis=0)


def _block_ones(n, block):
    t = _iota2((n, n), 0)
    s = _iota2((n, n), 1)
    return jnp.where((t // block) == (s // block), 1.0, 0.0).astype(BF16)


def _nilpotent_inverses(xs, n, refine):
    shape = xs[0].shape
    eye = jnp.where(_iota2(shape, 0) == _iota2(shape, 1), 1.0, 0.0).astype(F32)
    ts = [eye + x for x in xs]
    ps = list(xs)
    k = 2
    while k < n:
        ps = [_mm(p, p) for p in ps]
        ts = [t + _mm(t, p) for t, p in zip(ts, ps)]
        k *= 2
    if refine:
        res = [(eye - t) + _mm_hi(x, t) for x, t in zip(xs, ts)]
        ts = [t + _mm(t, r) for t, r in zip(ts, res)]
    return ts


def _norm_proj_kernel(h_ref, nw_ref, wa_ref, wb_ref, za_ref, zb_ref):
    xn = _rms(h_ref[...], nw_ref[...]).astype(BF16)
    za_ref[...] = jnp.dot(xn, wa_ref[...], preferred_element_type=F32)
    zb_ref[...] = jnp.dot(xn, wb_ref[...], preferred_element_type=F32)


def _const_spec(shape):
    nd = len(shape)
    return pl.BlockSpec(shape, lambda *_: (0,) * nd, pipeline_mode=pl.Buffered(1))


def _norm_proj(h, norm_w, wa, wb, tm):
    t, d = h.shape
    fa, fb = wa.shape[1], wb.shape[1]
    return pl.pallas_call(
        _norm_proj_kernel,
        grid=(t // tm,),
        in_specs=[
            pl.BlockSpec((tm, d), lambda i: (i, 0)),
            _const_spec((1, d)),
            _const_spec((d, fa)),
            _const_spec((d, fb)),
        ],
        out_specs=[
            pl.BlockSpec((tm, fa), lambda i: (i, 0)),
            pl.BlockSpec((tm, fb), lambda i: (i, 0)),
        ],
        out_shape=[jax.ShapeDtypeStruct((t, fa), F32), jax.ShapeDtypeStruct((t, fb), F32)],
        compiler_params=pltpu.CompilerParams(
            dimension_semantics=("parallel",), vmem_limit_bytes=VMEM_LIMIT),
        name="norm_proj",
    )(h, norm_w.reshape(1, d), wa, wb)


def _post_kernel(final_norm, h_ref, ma_ref, mb_ref, p_ref, wo_ref, nf_ref, wg_ref, wu_ref, wd_ref,
                 npl_ref, wpg_ref, wpu_ref, nfin_ref, out_ref):
    h = h_ref[...]
    mix = (jnp.dot(ma_ref[...], wo_ref[0:GROUP_WIDTH, :], preferred_element_type=F32)
           + jnp.dot(mb_ref[...], wo_ref[GROUP_WIDTH:2 * GROUP_WIDTH, :], preferred_element_type=F32))
    h1 = h + mix
    hn = _rms(h1, nf_ref[...]).astype(BF16)
    g = jnp.dot(hn, wg_ref[...], preferred_element_type=F32)
    u = jnp.dot(hn, wu_ref[...], preferred_element_type=F32)
    act = (_silu(g) * u).astype(BF16)
    h2 = h1 + jnp.dot(act, wd_ref[...], preferred_element_type=F32)
    hp = _rms(h2, npl_ref[...]).astype(BF16)
    gate = _sigmoid(jnp.dot(hp, wpg_ref[...], preferred_element_type=F32))
    up = jnp.dot(p_ref[...].astype(BF16), wpu_ref[...], preferred_element_type=F32)
    h3 = h2 + gate * up
    if final_norm:
        h3 = _rms(h3, nfin_ref[...])
    out_ref[...] = h3


def _post(h, ma, mb, p_all, layer, wo, nf, wg, wu, wd, npl, wpg, wpu, nfin, final_norm, tm):
    t, d = h.shape
    row = lambda w: pl.BlockSpec((tm, w), lambda i: (i, 0))
    slab = lambda a: pl.BlockSpec((None,) + a.shape[1:], lambda i: (layer, 0, 0), pipeline_mode=pl.Buffered(1))
    return pl.pallas_call(
        functools.partial(_post_kernel, final_norm),
        grid=(t // tm,),
        in_specs=[
            row(d), row(GROUP_WIDTH), row(GROUP_WIDTH),
            pl.BlockSpec((None, tm, PLE_DIM), lambda i: (layer, i, 0)),
            _const_spec(wo.shape), _const_spec((1, d)),
            slab(wg), slab(wu), slab(wd),
            _const_spec((1, d)), slab(wpg), slab(wpu),
            _const_spec((1, d)),
        ],
        out_specs=row(d),
        out_shape=jax.ShapeDtypeStruct((t, d), F32),
        compiler_params=pltpu.CompilerParams(
            dimension_semantics=("parallel",), vmem_limit_bytes=VMEM_LIMIT),
        name="post_mixer",
    )(h, ma, mb, p_all, wo, nf.reshape(1, d), wg, wu, wd, npl.reshape(1, d), wpg, wpu, nfin.reshape(1, d))


def _causal_conv(xbuf_ref, x, w_ref, ts):
    xcat = jnp.concatenate([xbuf_ref[...], x], axis=0)
    acc = w_ref[CONV_TAPS - 1:CONV_TAPS, :] * x
    for back in range(1, CONV_TAPS):
        shifted = pltpu.roll(xcat, back, 0)[HALO:HALO + ts, :]
        acc = acc + w_ref[CONV_TAPS - 1 - back:CONV_TAPS - back, :] * shifted
    xbuf_ref[...] = x[ts - HALO:ts, :]
    return acc


def _proj_rglru_kernel(tm, tiles_per_seq, h_ref, nw_ref, wa_ref, wb_ref, cw_ref, cb_ref, wri_ref, bri_ref,
                       lam_ref, out_ref, zb_ref, xbuf_ref, hc_ref):
    w = GROUP_WIDTH
    first = (pl.program_id(0) % tiles_per_seq) == 0

    @pl.when(first)
    def _():
        xbuf_ref[0:HALO, :] = jnp.zeros((HALO, w), F32)
        hc_ref[...] = jnp.zeros_like(hc_ref)

    xn = _rms(h_ref[...], nw_ref[...]).astype(BF16)
    z_rg = jnp.dot(xn, wa_ref[...], preferred_element_type=F32)
    xa = z_rg[:, 0:w]
    ya = z_rg[:, w:2 * w]
    x = _causal_conv(xbuf_ref, xa, cw_ref, tm) + cb_ref[...]
    ri = jnp.dot(x.astype(BF16), wri_ref[...], preferred_element_type=F32) + bri_ref[...]
    zb_ref[...] = jnp.dot(xn, wb_ref[...], preferred_element_type=F32)
    sp_lam = _softplus(-lam_ref[...])
    lanes = HEAD128
    row = _iota2((RG_ROWS, lanes), 0)
    for l0 in range(0, w, lanes):
        ls = slice(l0, l0 + lanes)
        carry = hc_ref[:, ls]
        for r0 in range(0, tm, RG_ROWS):
            rs = slice(r0, r0 + RG_ROWS)
            r = _sigmoid(ri[rs, l0:l0 + lanes])
            i = _sigmoid(ri[rs, w + l0:w + l0 + lanes])
            log_a = -RG_C * r * sp_lam[:, ls]
            a = jnp.exp(log_a)
            mult = jnp.sqrt(-jnp.tanh(log_a) * (a * a + 1.0))
            if r0 == 0:
                mult = jnp.where((row == 0) & first, 1.0, mult)
            acc_a, acc_h = a, mult * (i * x[rs, ls])
            d = 1
            while d < RG_ROWS:
                if d % HALO == 0:
                    sh_a = jnp.concatenate([jnp.ones((d, lanes), F32), acc_a[0:RG_ROWS - d, :]], axis=0)
                    sh_h = jnp.concatenate([jnp.zeros((d, lanes), F32), acc_h[0:RG_ROWS - d, :]], axis=0)
                else:
                    sh_a = jnp.where(row >= d, pltpu.roll(acc_a, d, 0), 1.0)
                    sh_h = jnp.where(row >= d, pltpu.roll(acc_h, d, 0), 0.0)
                acc_h = acc_a * sh_h + acc_h
                acc_a = acc_a * sh_a
                d *= 2
            h = acc_h + acc_a * carry
            carry = h[RG_ROWS - 1:RG_ROWS, :]
            out_ref[rs, ls] = (h * _gelu_tanh(ya[rs, ls])).astype(out_ref.dtype)
        hc_ref[:, ls] = carry


def _proj_rglru(h, norm_w, wa, wb, cw, cb, wri, bri, lam, tm, seq_len):
    t, d = h.shape
    w = GROUP_WIDTH
    fb = wb.shape[1]
    return pl.pallas_call(
        functools.partial(_proj_rglru_kernel, tm, seq_len // tm),
        grid=(t // tm,),
        in_specs=[
            pl.BlockSpec((tm, d), lambda i: (i, 0)),
            _const_spec((1, d)), _const_spec(wa.shape), _const_spec(wb.shape),
            _const_spec((CONV_TAPS, w)), _const_spec((1, w)),
            _const_spec((w, 2 * w)), _const_spec((1, 2 * w)), _const_spec((1, w)),
        ],
        out_specs=[
            pl.BlockSpec((tm, w), lambda i: (i, 0)),
            pl.BlockSpec((tm, fb), lambda i: (i, 0)),
        ],
        out_shape=[jax.ShapeDtypeStruct((t, w), BF16), jax.ShapeDtypeStruct((t, fb), F32)],
        scratch_shapes=[pltpu.VMEM((HALO, w), F32), pltpu.VMEM((1, w), F32)],
        compiler_params=pltpu.CompilerParams(
            dimension_semantics=("arbitrary",), vmem_limit_bytes=VMEM_LIMIT),
        name="proj_rglru",
    )(h, norm_w.reshape(1, d), wa, wb, cw, cb, wri, bri, lam)


def _gdn_kernel(ts, z_ref, cw_ref, alog_ref, dtb_ref, nw_ref, out_ref, xbuf_ref, st_ref):
    w = GROUP_WIDTH
    hd = HEAD128
    nh = w // hd
    cs = GDN_CHUNK
    nchunk = ts // cs
    tstep = pl.program_id(1)

    @pl.when(tstep == 0)
    def _():
        xbuf_ref[0:HALO, :] = jnp.zeros((HALO, 3 * w), F32)
        st_ref[...] = jnp.zeros_like(st_ref)

    qkv = _silu(_causal_conv(xbuf_ref, z_ref[0, :, 0:3 * w], cw_ref, ts))
    seg_w = 2 * hd
    ones_bd = _block_ones(seg_w, hd)

    def head_sum(x):
        xb = x.astype(BF16)
        return jnp.concatenate(
            [jnp.dot(xb[:, c0:c0 + seg_w], ones_bd, preferred_element_type=F32)
             for c0 in range(0, x.shape[1], seg_w)], axis=1)

    qk = qkv[:, 0:2 * w]
    qk = qk * lax.rsqrt(head_sum(qk * qk) + L2_EPS)
    zg = z_ref[0, :, 3 * w:4 * w]
    ab = z_ref[0, :, 4 * w:4 * w + hd]
    g_n = -jnp.exp(alog_ref[...]) * _softplus(ab + dtb_ref[...])
    gc_n = _sel_mm(_block_prefix(ts, cs), g_n)
    src = _iota2((hd, w), 0)
    dst_head = _iota2((hd, w), 1) // hd
    gc_all = _mm_sel(gc_n, jnp.where(src == dst_head, 1.0, 0.0).astype(BF16), terms=3)
    beta = _mm_sel(_sigmoid(ab), jnp.where(src == dst_head + nh, 1.0, 0.0).astype(BF16), terms=3)
    gs_all = _chunk_last(gc_all, cs) - gc_all
    eg_all = jnp.exp(gc_all)
    scale = hd ** -0.5
    tt = _iota2((cs, cs), 0)
    ss = _iota2((cs, cs), 1)

    items = [(h, c) for c in range(nchunk) for h in range(nh)]
    qs, ks, vbs, kbegs, decs, kdts, glast = [], [], [], [], [], [], []
    for h, c in items:
        lo = h * hd
        rows = slice(c * cs, (c + 1) * cs)
        hs = slice(lo, lo + hd)
        qh = qk[rows, lo:lo + hd] * scale
        kh = qk[rows, w + lo:w + lo + hd]
        vh = qkv[rows, 2 * w + lo:2 * w + lo + hd]
        gc = gc_all[rows, hs]
        bt = beta[rows, hs]
        eg = eg_all[rows, hs]
        kb = kh * bt
        diff = gc - gc.T
        decs.append(jnp.exp(jnp.where(ss <= tt, diff, -jnp.inf)))
        qs.append(qh * eg)
        ks.append((kh, kb, qh))
        vbs.append(vh * bt)
        kbegs.append(kb * eg)
        kdts.append((kh * jnp.exp(gs_all[rows, hs])).T)
        glast.append(jnp.exp(gc_all[(c + 1) * cs - 1:(c + 1) * cs, hs]))
    a_mats = [jnp.where(ss < tt, _mm_nt(kb, kh) * dec, 0.0) for (kh, kb, _), dec in zip(ks, decs)]
    qks = [_mm_nt(qh, kh) * dec for (kh, _, qh), dec in zip(ks, decs)]
    tinvs = _nilpotent_inverses([-a for a in a_mats], cs, refine=True)
    sols = [_mm(t, jnp.concatenate([vb, kbeg], axis=1)) for t, vb, kbeg in zip(tinvs, vbs, kbegs)]
    us = [sol[:, 0:hd] for sol in sols]
    ws = [sol[:, hd:2 * hd] for sol in sols]

    states = [st_ref[h] for h in range(nh)]
    outs = [[] for _ in range(nh)]
    for c in range(nchunk):
        idx = [c * nh + h for h in range(nh)]
        proj = [_mm(jnp.concatenate([ws[i], qs[i]], axis=0), states[h]) for h, i in enumerate(idx)]
        v_new = [us[i] - pr[0:cs] for i, pr in zip(idx, proj)]
        o = [pr[cs:2 * cs] + _mm(qks[i], vn) for i, pr, vn in zip(idx, proj, v_new)]
        states = [glast[i] * states[h] + _mm(kdts[i], vn) for (h, i), vn in zip(enumerate(idx), v_new)]
        for h in range(nh):
            outs[h].append(o[h])
    for h in range(nh):
        lo = h * hd
        st_ref[h] = states[h]
        o = jnp.concatenate(outs[h], axis=0) if nchunk > 1 else outs[h][0]
        o = _rms(o, nw_ref[...]) * _silu(zg[:, lo:lo + hd])
        out_ref[0, :, lo:lo + hd] = o.astype(out_ref.dtype)


def _gdn(z, cw, alog, dtb, nw, ts):
    b, s, _ = z.shape
    w = GROUP_WIDTH
    nh = w // HEAD128
    return pl.pallas_call(
        functools.partial(_gdn_kernel, ts),
        grid=(b, s // ts),
        in_specs=[
            pl.BlockSpec((1, ts, 4 * w + HEAD128), lambda bi, ti: (bi, ti, 0)),
            _const_spec((CONV_TAPS, 3 * w)), _const_spec((1, HEAD128)), _const_spec((1, HEAD128)),
            _const_spec((1, HEAD128)),
        ],
        out_specs=pl.BlockSpec((1, ts, w), lambda bi, ti: (bi, ti, 0)),
        out_shape=jax.ShapeDtypeStruct((b, s, w), BF16),
        scratch_shapes=[pltpu.VMEM((HALO, 3 * w), F32), pltpu.VMEM((nh, HEAD128, HEAD128), F32)],
        compiler_params=pltpu.CompilerParams(
            dimension_semantics=("parallel", "arbitrary"), vmem_limit_bytes=VMEM_LIMIT),
        name="gated_deltanet",
    )(z, cw, alog, dtb, nw)


def _hgrn2_kernel(ts, layer_j, z_ref, lbraw_ref, nw_ref, out_ref, st_ref, o_ref):
    w = GROUP_WIDTH
    hd = HEAD128
    nh = w // hd
    tstep = pl.program_id(1)

    @pl.when(tstep == 0)
    def _():
        st_ref[...] = jnp.zeros_like(st_ref)

    lbraw = lbraw_ref[...]
    e = jnp.exp(lbraw - jnp.max(lbraw, axis=0, keepdims=True))
    soft = e / jnp.sum(e, axis=0, keepdims=True)
    lb = jnp.sum(soft[0:layer_j + 1, :], axis=0, keepdims=True) - soft[0:1, :]
    lb = jnp.maximum(lb, 0.0)

    q = _silu(z_ref[0, :, 0:w])
    f = z_ref[0, :, w:2 * w]
    v = z_ref[0, :, 2 * w:3 * w]
    gate = z_ref[0, :, 3 * w:4 * w]
    kin = (1.0 - lb) * _sigmoid(-f)
    la = jnp.log(lb)
    lbm = jnp.log1p(-lb) - _softplus(-f)
    mx = jnp.maximum(la, lbm)
    log_f = mx + jnp.log(1.0 + jnp.exp(-jnp.abs(la - lbm)))

    cs = HG_CHUNK
    nchunk = ts // cs
    b = _sel_mm(_block_prefix(ts, cs), log_f)
    b2 = b * math.log2(math.e)
    row = _iota2((ts, w), 0)
    tt = _iota2((cs, cs), 0)
    ss = _iota2((cs, cs), 1)
    items = [(c, h) for c in range(nchunk) for h in range(nh)]
    sl = lambda c, h: (slice(c * cs, (c + 1) * cs), slice(h * hd, (h + 1) * hd))

    scores = [jnp.where(tt == ss, _mm_nt(q[sl(c, h)], kin[sl(c, h)]), 0.0) for c, h in items]
    block_end = b2
    m = 1
    while m < cs:
        prev_end = pltpu.roll(block_end, m, 0)
        q_m = q * jnp.exp2(b2 - prev_end)
        k_m = kin * jnp.exp2(block_end - b2)
        pair = ((tt // m) % 2 == 1) & ((ss // m) == (tt // m) - 1)
        scores = [jnp.where(pair, _mm_nt(q_m[sl(c, h)], k_m[sl(c, h)]), sc)
                  for sc, (c, h) in zip(scores, items)]
        block_end = jnp.where((row // m) % 2 == 0, pltpu.roll(block_end, ts - m, 0), block_end)
        m *= 2
    o_intra = [_mm(sc, v[sl(c, h)]) for sc, (c, h) in zip(scores, items)]

    btot = _chunk_last(b, cs)
    qe = q * jnp.exp(b)
    k_dec = kin * jnp.exp(btot - b)
    dec = jnp.exp(btot)
    incs = [_mm(v[sl(c, h)].T, k_dec[sl(c, h)]) for c, h in items]
    states = [st_ref[h] for h in range(nh)]
    for c in range(nchunk):
        for h in range(nh):
            i = c * nh + h
            o_ref[c * cs:(c + 1) * cs, h * hd:(h + 1) * hd] = o_intra[i] + _mm_nt(qe[sl(c, h)], states[h])
        states = [states[h] * dec[c * cs:c * cs + 1, h * hd:(h + 1) * hd] + incs[c * nh + h]
                  for h in range(nh)]
    for h in range(nh):
        st_ref[h] = states[h]

    out_ref[0] = (_rms(o_ref[...], nw_ref[...]) * _sigmoid(gate)).astype(out_ref.dtype)


def _hgrn2(z, lbraw, nw, layer_j, ts):
    b, s, _ = z.shape
    w = GROUP_WIDTH
    nh = w // HEAD128
    return pl.pallas_call(
        functools.partial(_hgrn2_kernel, ts, layer_j),
        grid=(b, s // ts),
        in_specs=[
            pl.BlockSpec((1, ts, 4 * w), lambda bi, ti: (bi, ti, 0)),
            _const_spec(lbraw.shape), _const_spec((1, w)),
        ],
        out_specs=pl.BlockSpec((1, ts, w), lambda bi, ti: (bi, ti, 0)),
        out_shape=jax.ShapeDtypeStruct((b, s, w), BF16),
        scratch_shapes=[pltpu.VMEM((nh, HEAD128, HEAD128), F32), pltpu.VMEM((ts, w), F32)],
        compiler_params=pltpu.CompilerParams(
            dimension_semantics=("parallel", "arbitrary"), vmem_limit_bytes=VMEM_LIMIT),
        name="hgrn2",
    )(z, lbraw, nw)


def _rwkv7_kernel(ts, z_ref, mu_ref, w0_ref, w2_ref, a0_ref, a2_ref, g2_ref, kk_ref, ka_ref, rk_ref,
                  lnw_ref, lnb_ref, out_ref, xbuf_ref, st_ref):
    w = GROUP_WIDTH
    lp = R7_LORA_PAD
    ch = R7_CHUNK
    npair = w // (2 * R7_N)
    tstep = pl.program_id(1)

    @pl.when(tstep == 0)
    def _():
        xbuf_ref[0:HALO, :] = jnp.zeros((HALO, R7_IN_PAD), F32)
        st_ref[...] = jnp.zeros_like(st_ref)

    pw = 2 * R7_N
    n = R7_N
    gch = min(R7_GROUP, ts // ch)
    grows = gch * ch
    lane = _iota2((ch, pw), 1)
    in_h0 = lane < n
    t_row = _iota2((ch, pw), 0)
    s_col = lane % n
    seg_w = 2 * HEAD128
    ones_bd = _block_ones(seg_w, R7_N)

    def head_sum(x):
        xb = x.astype(BF16)
        return jnp.concatenate(
            [jnp.dot(xb[:, c0:c0 + seg_w], ones_bd, preferred_element_type=F32) for c0 in range(0, w, seg_w)],
            axis=1)

    tri_incl = _block_prefix(grows, ch)
    items = [(c, p) for c in range(gch) for p in range(npair)]
    sl = lambda c, p: (slice(c * ch, (c + 1) * ch), slice(p * pw, (p + 1) * pw))

    def stack_heads(x):
        return jnp.concatenate([jnp.where(in_h0, x, 0.0), jnp.where(in_h0, 0.0, x)], axis=0)

    z_all = z_ref[0]
    z_prev_all = pltpu.roll(jnp.concatenate([xbuf_ref[0:HALO, :], z_all], axis=0), 1, 0)[HALO:HALO + ts, :]
    states = [st_ref[p] for p in range(npair)]
    for g0 in range(0, ts, grows):
        z = z_all[g0:g0 + grows, :]
        z_prev = z_prev_all[g0:g0 + grows, :]
        zd = z + mu_ref[...] * (z_prev - z)
        r = zd[:, 0:w]
        k = zd[:, w:2 * w]
        v = zd[:, 2 * w:3 * w]
        w_l = zd[:, 3 * w:3 * w + lp]
        a_l = zd[:, 3 * w + lp:3 * w + 2 * lp]
        g_l = zd[:, 3 * w + 2 * lp:3 * w + 3 * lp]
        log_w = -jnp.exp(-_softplus(-(w0_ref[...] + _mm(jnp.tanh(w_l), w2_ref[...]))) - 0.5)
        a = _sigmoid(a0_ref[...] + _mm(a_l, a2_ref[...]))
        gate = _mm(_sigmoid(g_l), g2_ref[...])
        kk = k * kk_ref[...]
        kk = kk * lax.rsqrt(head_sum(kk * kk) + L2_EPS)
        k = k * (1.0 + (a - 1.0) * ka_ref[...])
        cum = _sel_mm(tri_incl, log_w)
        total = _chunk_last(cum, ch)
        e_inv = jnp.exp(-cum)
        r_t = r * jnp.exp(cum)
        a_t = -kk * jnp.exp(cum - log_w)
        kka = kk * a
        b_h = kka * e_inv
        k_h = k * e_inv
        e_suf = jnp.exp(total - cum)
        b_dc = kka * e_suf
        k_dc = k * e_suf
        d_end = jnp.exp(total)

        v_ss = [stack_heads(v[sl(c, p)]) for c, p in items]
        grams = [
            _mm_nt(jnp.concatenate([a_t[sl(c, p)], r_t[sl(c, p)]], axis=0),
                   jnp.concatenate([stack_heads(b_h[sl(c, p)]), stack_heads(k_h[sl(c, p)])], axis=0))
            for c, p in items]
        a_abs = [jnp.where(s_col < t_row, g[0:ch, 0:pw], 0.0) for g in grams]
        a_rbs = [jnp.where(s_col <= t_row, g[ch:2 * ch, 0:pw], 0.0) for g in grams]
        a_k = [jnp.concatenate([jnp.where(s_col < t_row, g[0:ch, pw:2 * pw], 0.0),
                                jnp.where(s_col <= t_row, g[ch:2 * ch, pw:2 * pw], 0.0)], axis=0)
               for g in grams]
        eye_cat = jnp.where(s_col == t_row, 1.0, 0.0).astype(F32)
        t_cats = [eye_cat + x for x in a_abs]
        p_cats = [jnp.dot(p.astype(BF16), stack_heads(p).astype(BF16), preferred_element_type=F32)
                  for p in a_abs]
        k_pow = 2
        while 2 * k_pow < ch:
            both = [jnp.dot(jnp.concatenate([p, t], axis=0).astype(BF16), stack_heads(p).astype(BF16),
                            preferred_element_type=F32) for p, t in zip(p_cats, t_cats)]
            p_cats = [x[0:ch] for x in both]
            t_cats = [t + x[ch:2 * ch] for t, x in zip(t_cats, both)]
            k_pow *= 2
        t_cats = [t + jnp.dot(t.astype(BF16), stack_heads(p).astype(BF16), preferred_element_type=F32)
                  for p, t in zip(p_cats, t_cats)]
        tinvs = [stack_heads(t) for t in t_cats]
        kv = [_mm(x, vs) for x, vs in zip(a_k, v_ss)]
        w_ss = [_mm(t, jnp.concatenate([stack_heads(a_t[sl(c, p)]), stack_heads(x[0:ch])], axis=1))
                for t, x, (c, p) in zip(tinvs, kv, items)]
        ry = [_mm(x, w_s) for x, w_s in zip(a_rbs, w_ss)]
        r_eff = [r_t[sl(c, p)] + x[:, 0:pw] for x, (c, p) in zip(ry, items)]
        y0 = [x[:, pw:2 * pw] + y[ch:2 * ch] for x, y in zip(ry, kv)]
        bdc_ss = [stack_heads(b_dc[sl(c, p)]) for c, p in items]
        phi_t = [_mm(w_s[:, 0:pw].T, bd).astype(BF16) for w_s, bd in zip(w_ss, bdc_ss)]
        gamma_t = [_mm(jnp.concatenate([w_s[:, pw:2 * pw].T, vs.T], axis=1),
                       jnp.concatenate([bd, stack_heads(k_dc[sl(c, p)])], axis=0))
                   for w_s, vs, bd, (c, p) in zip(w_ss, v_ss, bdc_ss, items)]

        ys = [[] for _ in range(npair)]
        for c in range(gch):
            st_bs = [st.astype(BF16) for st in states]
            states = [states[p] * d_end[c * ch:c * ch + 1, p * pw:(p + 1) * pw]
                      + jnp.dot(st_bs[p], phi_t[c * npair + p], preferred_element_type=F32)
                      + gamma_t[c * npair + p] for p in range(npair)]
            for p in range(npair):
                i = c * npair + p
                ys[p].append(_mm_nt(r_eff[i], st_bs[p]) + y0[i])
        y = jnp.concatenate([jnp.concatenate(yp, axis=0) if gch > 1 else yp[0] for yp in ys], axis=1)

        mean = head_sum(y) * (1.0 / R7_N)
        yc = y - mean
        var = head_sum(yc * yc) * (1.0 / R7_N)
        yn = yc * lax.rsqrt(var + GN_EPS) * lnw_ref[...] + lnb_ref[...]
        bonus = head_sum(r * k * rk_ref[...]) * v
        out_ref[0, g0:g0 + grows, :] = ((yn + bonus) * gate).astype(out_ref.dtype)
    for p in range(npair):
        st_ref[p] = states[p]
    xbuf_ref[0:HALO, :] = z_all[ts - HALO:ts, :]


def _rwkv7(z, mu, w0, w2, a0, a2, g2, k_k, k_a, r_k, ln_w, ln_b, ts):
    b, s, fin = z.shape
    w = GROUP_WIDTH
    npair = w // (2 * R7_N)
    vec = _const_spec((1, w))
    return pl.pallas_call(
        functools.partial(_rwkv7_kernel, ts),
        grid=(b, s // ts),
        in_specs=[
            pl.BlockSpec((1, ts, fin), lambda bi, ti: (bi, ti, 0)),
            _const_spec((1, fin)), vec, _const_spec(w2.shape), vec, _const_spec(a2.shape),
            _const_spec(g2.shape), vec, vec, vec, vec, vec,
        ],
        out_specs=pl.BlockSpec((1, ts, w), lambda bi, ti: (bi, ti, 0)),
        out_shape=jax.ShapeDtypeStruct((b, s, w), BF16),
        scratch_shapes=[pltpu.VMEM((HALO, fin), F32), pltpu.VMEM((npair, 2 * R7_N, 2 * R7_N), F32)],
        compiler_params=pltpu.CompilerParams(
            dimension_semantics=("parallel", "arbitrary"), vmem_limit_bytes=VMEM_LIMIT),
        name="rwkv7",
    )(z, mu, w0, w2, a0, a2, g2, k_k, k_a, r_k, ln_w, ln_b)


def _even_weights(w_in, rg_w_r, rg_w_i, rg_b_r, rg_b_i, gdn_a_log, gdn_dt_bias):
    w = GROUP_WIDTH
    nh = w // HEAD128
    w_rg = w_in[:, 0:2 * w]
    w_gdn = jnp.pad(w_in[:, 2 * w:6 * w + 2 * nh], ((0, 0), (0, HEAD128 - 2 * nh)))
    wri = jnp.concatenate([jax.scipy.linalg.block_diag(*rg_w_r), jax.scipy.linalg.block_diag(*rg_w_i)], axis=1)
    bri = jnp.concatenate([rg_b_r, rg_b_i]).reshape(1, 2 * w)
    alog = jnp.pad(gdn_a_log, (0, HEAD128 - nh)).reshape(1, HEAD128)
    dtb = jnp.pad(gdn_dt_bias, (0, HEAD128 - nh)).reshape(1, HEAD128)
    return w_rg.astype(BF16), w_gdn.astype(BF16), wri.astype(BF16), bri, alog, dtb


def _pad_groups(x, axis):
    w = GROUP_WIDTH
    sizes = (w, w, w, 64, 64, 128)
    parts = []
    start = 0
    for sz in sizes:
        piece = lax.slice_in_dim(x, start, start + sz, axis=axis)
        pad = (-sz) % R7_LORA_PAD
        if pad:
            cfg = [(0, 0)] * x.ndim
            cfg[axis] = (0, pad)
            piece = jnp.pad(piece, cfg)
        parts.append(piece)
        start += sz
    return jnp.concatenate(parts, axis=axis)


def _pad_rows(x, rows):
    return jnp.pad(x, ((0, rows - x.shape[0]), (0, 0)))


def _tiles(s):
    tm_proj = 512 if s % 512 == 0 else 256
    tm_post = 512
    ts = next(c for c in (512, 256, 128) if s % c == 0)
    ts_hg = min(ts, 256)
    return tm_proj, tm_post, ts, ts_hg


def kernel(x, p, norm_mix, norm_ffn, norm_ple, norm_final, e_w_in, rg_conv_w, rg_conv_b, rg_w_r, rg_b_r, rg_w_i, rg_b_i, rg_lambda, gdn_conv_w, gdn_a_log, gdn_dt_bias, gdn_norm_w, e_w_out, o_w_in, hg_lower_bounds, hg_norm_w, r7_mu, r7_w0, r7_w2, r7_a0, r7_a2, r7_g2, r7_k_k, r7_k_a, r7_r_k, r7_ln_w, r7_ln_b, o_w_out, ffn_w_gate, ffn_w_up, ffn_w_down, ple_w_up, ple_w_gate):
    b, s, d = x.shape
    depth = p.shape[0]
    t = b * s
    w = GROUP_WIDTH
    tm_proj, tm_post, ts, ts_hg = _tiles(s)
    h = x.reshape(t, d)
    row = lambda vec: vec.reshape(1, -1)
    post_w = [a.astype(BF16) for a in (ffn_w_gate, ffn_w_up, ffn_w_down, ple_w_gate, ple_w_up)]
    for layer in range(depth):
        j = layer // 2
        if layer % 2 == 0:
            w_rg, w_gdn, wri, bri, alog, dtb = _even_weights(
                e_w_in[j], rg_w_r[j], rg_w_i[j], rg_b_r[j], rg_b_i[j], gdn_a_log[j], gdn_dt_bias[j])
            out_a, z_b = _proj_rglru(h, norm_mix[layer], w_rg, w_gdn, rg_conv_w[j], row(rg_conv_b[j]),
                                     wri, bri, row(rg_lambda[j]), tm_proj, s)
            out_b = _gdn(z_b.reshape(b, s, -1), gdn_conv_w[j], alog, dtb, row(gdn_norm_w[j]), ts)
            w_out = e_w_out[j]
        else:
            hg_in = 4 * w
            w_hg = o_w_in[j][:, 0:hg_in].astype(BF16)
            w_r7 = _pad_groups(o_w_in[j][:, hg_in:], axis=1).astype(BF16)
            z_a, z_b = _norm_proj(h, norm_mix[layer], w_hg, w_r7, tm_proj)
            out_a = _hgrn2(z_a.reshape(b, s, -1), hg_lower_bounds, row(hg_norm_w[j]), j, ts_hg)
            out_b = _rwkv7(z_b.reshape(b, s, -1), row(_pad_groups(r7_mu[j], axis=0)), row(r7_w0[j]),
                           _pad_rows(r7_w2[j], R7_LORA_PAD).astype(BF16), row(r7_a0[j]),
                           _pad_rows(r7_a2[j], R7_LORA_PAD).astype(BF16), r7_g2[j].astype(BF16),
                           row(r7_k_k[j]), row(r7_k_a[j]), row(r7_r_k[j].reshape(-1)),
                           row(r7_ln_w[j]), row(r7_ln_b[j]), ts)
            w_out = o_w_out[j]
        h = _post(h, out_a.reshape(t, w), out_b.reshape(t, w), p.reshape(depth, t, PLE_DIM), layer,
                  w_out.astype(BF16), norm_ffn[layer], post_w[0], post_w[1], post_w[2], norm_ple[layer],
                  post_w[3], post_w[4], norm_final, layer == depth - 1, tm_post)
    return h.reshape(b, s, d)
```
